```python
import math
import jax, jax.numpy as jnp
from jax import lax
import numpy as np

D_MODEL = 2048
BATCH = 4
SEQ = 2048
DEPTH = 2
DEC_BATCH = 128
DEC_SEQ = 4
PAST_LEN = 16384
PAGE_SIZE = 128

N_MIXERS = 4
D_MIX = D_MODEL
D_GROUP = D_MIX // N_MIXERS
D_FF = 128 * ((8 * D_MODEL // 3 + 127) // 128)
CONV_W = 4
CHUNK = 64
EPS = 1e-6

SSD_HEADS = 8
SSD_HEAD_DIM = D_GROUP // SSD_HEADS
SSD_GROUPS = 2
SSD_STATE = 128
SSD_CONV_CH = D_GROUP + 2 * SSD_GROUPS * SSD_STATE

S5_CH = 16
S5_GROUPS = D_GROUP // S5_CH
S5_STATE = 64

GLA_HEADS = 4
GLA_DV = D_GROUP // GLA_HEADS
GLA_DK = GLA_DV // 2
GLA_GATE_RANK = 16
GLA_GATE_TEMP = 16.0

GDN_HEADS = 4
GDN_DK = D_GROUP // GDN_HEADS
GDN_DV = D_GROUP // GDN_HEADS
GDN_CONV_CH = 2 * GDN_HEADS * GDN_DK + GDN_HEADS * GDN_DV

SSD_SPLIT = (D_GROUP, SSD_CONV_CH, SSD_HEADS)
S5_SPLIT = (D_GROUP,)
GLA_SPLIT = (GLA_HEADS * GLA_DK, GLA_HEADS * GLA_DK, D_GROUP, GLA_GATE_RANK, D_GROUP)
GDN_SPLIT = (GDN_CONV_CH, D_GROUP, GDN_HEADS, GDN_HEADS)
IN_SPLIT = SSD_SPLIT + S5_SPLIT + GLA_SPLIT + GDN_SPLIT
IN_COLS = sum(IN_SPLIT)

kernel_name = 'hybrid_ssd_s5_gla_gdn_macaron_step'


def rmsnorm(x, g):
    xf = x.astype(jnp.float32)
    y = xf * lax.rsqrt(jnp.mean(xf * xf, axis=-1, keepdims=True) + EPS)
    return (y * g.astype(jnp.float32)).astype(x.dtype)


def group_rmsnorm(y, g, n_groups):
    shp = y.shape
    yg = y.reshape(shp[:-1] + (n_groups, shp[-1] // n_groups))
    yg = yg * lax.rsqrt(jnp.mean(yg * yg, axis=-1, keepdims=True) + EPS)
    return yg.reshape(shp) * g.astype(jnp.float32)


def l2norm(x):
    return x * lax.rsqrt(jnp.sum(x * x, axis=-1, keepdims=True) + EPS)


def split_last(a, sizes):
    cuts, acc = [], 0
    for s in sizes[:-1]:
        acc += s
        cuts.append(acc)
    return jnp.split(a, cuts, axis=-1)


def swiglu(x, w_gate, w_up, w_down):
    return (jax.nn.silu(x @ w_gate) * (x @ w_up)) @ w_down


def causal_dwconv(x, buf, w):
    xp = jnp.concatenate([buf.astype(x.dtype), x], axis=1)
    y = lax.conv_general_dilated(xp, w[:, None, :].astype(x.dtype), window_strides=(1,), padding='VALID',
                                 dimension_numbers=('NWC', 'WIO', 'NWC'), feature_group_count=x.shape[-1])
    return y, xp[:, xp.shape[1] - (CONV_W - 1):]


def chunk_layout(length):
    size = min(CHUNK, length)
    n = -(-length // size)
    return size, n, n * size - length


def to_blocks(a, n, size, pad):
    if pad:
        a = jnp.pad(a, [(0, 0), (0, pad)] + [(0, 0)] * (a.ndim - 2))
    return a.reshape((a.shape[0], n, size) + a.shape[2:])


def ssd_scan(x, dt, a, bm, cm, h0):
    bsz, length, heads, hd = x.shape
    size, n, pad = chunk_layout(length)
    xdt = to_blocks(x * dt[..., None], n, size, pad)
    da = to_blocks(dt * a, n, size, pad)
    bm = to_blocks(bm, n, size, pad)
    cm = to_blocks(cm, n, size, pad)
    a_cs = jnp.cumsum(da, axis=2)
    causal = jnp.tril(jnp.ones((size, size), bool))[:, :, None]
    decay = jnp.exp(jnp.where(causal, a_cs[:, :, :, None, :] - a_cs[:, :, None, :, :], -jnp.inf))
    scores = jnp.einsum('bclhn,bcshn->bclsh', cm, bm) * decay
    y_diag = jnp.einsum('bclsh,bcshp->bclhp', scores, xdt)
    to_end = jnp.exp(a_cs[:, :, -1:, :] - a_cs)
    chunk_states = jnp.einsum('bclhn,bclh,bclhp->bchpn', bm, to_end, xdt)
    chunk_decay = jnp.exp(a_cs[:, :, -1, :])

    def step(h, inp):
        s_c, d_c = inp
        return h * d_c[:, :, None, None] + s_c, h

    h_last, h_prev = lax.scan(step, h0, (jnp.moveaxis(chunk_states, 1, 0), jnp.moveaxis(chunk_decay, 1, 0)))
    y_off = jnp.einsum('bclhn,cbhpn,bclh->bclhp', cm, h_prev, jnp.exp(a_cs))
    y = (y_diag + y_off).reshape(bsz, n * size, heads, hd)[:, :length]
    return y, h_last


def _linear_combine(left, right):
    return (right[0] * left[0], right[0] * left[1] + right[1])


def s5_mixer(u, h0_re, h0_im, a_re, a_im, log_dt, b_re, b_im, c_re, c_im, d_skip, w_glu, b_glu):
    bsz, length, _ = u.shape
    a = lax.complex(a_re.astype(jnp.float32), a_im.astype(jnp.float32))
    dt = jnp.exp(log_dt.astype(jnp.float32))[:, None]
    a_bar = jnp.exp(a * dt)
    b = lax.complex(b_re.astype(jnp.float32), b_im.astype(jnp.float32))
    b_bar = ((a_bar - 1.0) / a)[..., None] * b
    ug = u.reshape(bsz, length, S5_GROUPS, S5_CH).astype(jnp.complex64)
    bu = jnp.einsum('blgi,gpi->blgp', ug, b_bar)
    a_cum, h = lax.associative_scan(_linear_combine, (jnp.broadcast_to(a_bar, bu.shape), bu), axis=1)
    h = h + a_cum * lax.complex(h0_re.astype(jnp.float32), h0_im.astype(jnp.float32))[:, None]
    c = lax.complex(c_re.astype(jnp.float32), c_im.astype(jnp.float32))
    y = jnp.real(jnp.einsum('blgp,gip->blgi', h, c)).reshape(bsz, length, D_GROUP) + d_skip.astype(jnp.float32) * u
    y = jax.nn.gelu(y)
    y = y * jax.nn.sigmoid(y @ w_glu.astype(jnp.float32) + b_glu.astype(jnp.float32))
    h_last = h[:, -1]
    return y, jnp.real(h_last), jnp.imag(h_last)


def gla_scan(q, k, v, log_g, s0):
    bsz, length, heads, _ = q.shape
    size, n, pad = chunk_layout(length)
    q, k, v, log_g = [to_blocks(t, n, size, pad) for t in (q, k, v, log_g)]
    b = jnp.cumsum(log_g, axis=2)
    b_last = b[:, :, -1]
    q_dec = q * jnp.exp(b)
    k_dec = k * jnp.exp(-b)
    causal = jnp.tril(jnp.ones((size, size), bool))[:, :, None]
    att = jnp.where(causal, jnp.einsum('bclhk,bcshk->bclsh', q_dec, k_dec), 0.0)
    o_intra = jnp.einsum('bclsh,bcshv->bclhv', att, v)
    k_to_end = k * jnp.exp(b_last[:, :, None] - b)
    chunk_kv = jnp.einsum('bcshk,bcshv->bchkv', k_to_end, v)

    def step(s, inp):
        kv_c, d_c = inp
        return s * d_c[..., None] + kv_c, s

    s_last, s_prev = lax.scan(step, s0, (jnp.moveaxis(chunk_kv, 1, 0), jnp.moveaxis(jnp.exp(b_last), 1, 0)))
    o_inter = jnp.einsum('bclhk,cbhkv->bclhv', q_dec, s_prev)
    o = (o_intra + o_inter).reshape(bsz, n * size, heads, -1)[:, :length]
    return o, s_last


def gdn_scan(q, k, v, g, beta, s0):
    bsz, length, heads, _ = q.shape
    size, n, pad = chunk_layout(length)
    q, k, v = [to_blocks(t, n, size, pad) for t in (q, k, v)]
    g, beta = [to_blocks(t, n, size, pad) for t in (g, beta)]
    gc = jnp.cumsum(g, axis=2)
    causal = jnp.tril(jnp.ones((size, size), bool))[:, :, None]
    strict = jnp.tril(jnp.ones((size, size), bool), -1)[:, :, None]
    decay = jnp.exp(jnp.where(causal, gc[:, :, :, None, :] - gc[:, :, None, :, :], -jnp.inf))
    kk = jnp.einsum('bclhk,bcshk->bclsh', k, k)
    a_mat = jnp.where(strict, beta[:, :, :, None, :] * kk * decay, 0.0)
    a_mat = jnp.moveaxis(a_mat, -1, 2)
    eye = jnp.eye(size, dtype=jnp.float32)
    t_mat = lax.linalg.triangular_solve(eye + a_mat, jnp.broadcast_to(eye, a_mat.shape),
                                        left_side=True, lower=True, unit_diagonal=True)
    u = jnp.einsum('bchls,bcshv->bclhv', t_mat, v * beta[..., None])
    w = jnp.einsum('bchls,bcshk->bclhk', t_mat, k * (beta * jnp.exp(gc))[..., None])
    qk = jnp.einsum('bclhk,bcshk->bclsh', q, k) * decay
    q_dec = q * jnp.exp(gc)[..., None]
    k_to_end = k * jnp.exp(gc[:, :, -1:, :] - gc)[..., None]
    g_last = jnp.exp(gc[:, :, -1])

    def step(s, inp):
        u_c, w_c, q_c, qk_c, k_c, d_c = inp
        v_new = u_c - jnp.einsum('blhk,bhkv->blhv', w_c, s)
        o_c = jnp.einsum('blhk,bhkv->blhv', q_c, s) + jnp.einsum('blsh,bshv->blhv', qk_c, v_new)
        s = s * d_c[..., None, None] + jnp.einsum('blhk,blhv->bhkv', k_c, v_new)
        return s, o_c

    s_last, o = lax.scan(step, s0, tuple(jnp.moveaxis(t, 1, 0) for t in (u, w, q_dec, qk, k_to_end, g_last)))
    o = jnp.moveaxis(o, 0, 1).reshape(bsz, n * size, heads, -1)[:, :length]
    return o, s_last


def token_mix(u, st, p):
    bsz, length, _ = u.shape
    ssd_conv0, ssd_h0, s5_re0, s5_im0, gla_s0, gdn_conv0, gdn_s0 = st
    proj = u @ p['w_in']
    (ssd_z, ssd_xbc, ssd_dt, s5_u, gla_q, gla_k, gla_v, gla_gr, gla_r,
     gdn_qkv, gdn_z, gdn_b, gdn_a) = split_last(proj, IN_SPLIT)

    xbc, ssd_conv1 = causal_dwconv(ssd_xbc, ssd_conv0, p['ssd_conv_w'])
    xbc = jax.nn.silu(xbc.astype(jnp.float32) + p['ssd_conv_b'].astype(jnp.float32))
    xs, bs, cs = split_last(xbc, (D_GROUP, SSD_GROUPS * SSD_STATE, SSD_GROUPS * SSD_STATE))
    xs = xs.reshape(bsz, length, SSD_HEADS, SSD_HEAD_DIM)
    rep = SSD_HEADS // SSD_GROUPS
    bs = jnp.repeat(bs.reshape(bsz, length, SSD_GROUPS, SSD_STATE), rep, axis=2)
    cs = jnp.repeat(cs.reshape(bsz, length, SSD_GROUPS, SSD_STATE), rep, axis=2)
    dt = jax.nn.softplus(ssd_dt.astype(jnp.float32) + p['ssd_dt_bias'].astype(jnp.float32))
    a = -jnp.exp(p['ssd_a_log'].astype(jnp.float32))
    y, ssd_h1 = ssd_scan(xs, dt, a, bs, cs, ssd_h0.astype(jnp.float32))
    y = (y + p['ssd_d'].astype(jnp.float32)[:, None] * xs).reshape(bsz, length, D_GROUP)
    y_ssd = group_rmsnorm(y * jax.nn.silu(ssd_z.astype(jnp.float32)), p['ssd_norm'], SSD_GROUPS)

    y_s5, s5_re1, s5_im1 = s5_mixer(s5_u.astype(jnp.float32), s5_re0, s5_im0, p['s5_a_re'], p['s5_a_im'],
                                    p['s5_log_dt'], p['s5_b_re'], p['s5_b_im'], p['s5_c_re'], p['s5_c_im'],
                                    p['s5_d'], p['s5_w_glu'], p['s5_b_glu'])

    q = gla_q.astype(jnp.float32).reshape(bsz, length, GLA_HEADS, GLA_DK) * GLA_DK ** -0.5
    k = gla_k.astype(jnp.float32).reshape(bsz, length, GLA_HEADS, GLA_DK)
    v = gla_v.astype(jnp.float32).reshape(bsz, length, GLA_HEADS, GLA_DV)
    log_g = jax.nn.log_sigmoid(gla_gr.astype(jnp.float32) @ p['gla_w_gate2'].astype(jnp.float32)
                               + p['gla_b_gate2'].astype(jnp.float32)) / GLA_GATE_TEMP
    o, gla_s1 = gla_scan(q, k, v, log_g.reshape(bsz, length, GLA_HEADS, GLA_DK), gla_s0.astype(jnp.float32))
    y_gla = group_rmsnorm(o.reshape(bsz, length, D_GROUP), p['gla_norm'], GLA_HEADS) * jax.nn.silu(gla_r.astype(jnp.float32))

    qkv, gdn_conv1 = causal_dwconv(gdn_qkv, gdn_conv0, p['gdn_conv_w'])
    qkv = jax.nn.silu(qkv.astype(jnp.float32))
    q, k, v = split_last(qkv, (GDN_HEADS * GDN_DK, GDN_HEADS * GDN_DK, GDN_HEADS * GDN_DV))
    q = l2norm(q.reshape(bsz, length, GDN_HEADS, GDN_DK)) * GDN_DK ** -0.5
    k = l2norm(k.reshape(bsz, length, GDN_HEADS, GDN_DK))
    v = v.reshape(bsz, length, GDN_HEADS, GDN_DV)
    beta = jax.nn.sigmoid(gdn_b.astype(jnp.float32))
    g = -jnp.exp(p['gdn_a_log'].astype(jnp.float32)) * jax.nn.softplus(gdn_a.astype(jnp.float32) + p['gdn_dt_bias'].astype(jnp.float32))
    o, gdn_s1 = gdn_scan(q, k, v, g, beta, gdn_s0.astype(jnp.float32))
    y_gdn = group_rmsnorm(o.reshape(bsz, length, D_GROUP), p['gdn_norm'], GDN_HEADS) * jax.nn.silu(gdn_z.astype(jnp.float32))

    mixed = jnp.concatenate([y_ssd, y_s5, y_gla, y_gdn], axis=-1).astype(u.dtype)
    return mixed @ p['w_out'], (ssd_conv1, ssd_h1, s5_re1, s5_im1, gla_s1, gdn_conv1, gdn_s1)


def trunk(x, states, w, final_norm):
    h = x
    new_states = [[] for _ in states]
    for i in range(DEPTH):
        p = {name: arr[i] for name, arr in w.items()}
        h = h + 0.5 * swiglu(rmsnorm(h, p['ffn1_norm']), p['ffn1_w_gate'], p['ffn1_w_up'], p['ffn1_w_down'])
        mixed, layer_states = token_mix(rmsnorm(h, p['mix_norm']), tuple(s[i] for s in states), p)
        h = h + mixed
        h = h + 0.5 * swiglu(rmsnorm(h, p['ffn2_norm']), p['ffn2_w_gate'], p['ffn2_w_up'], p['ffn2_w_down'])
        for acc, s in zip(new_states, layer_states):
            acc.append(s)
    return rmsnorm(h, final_norm), tuple(jnp.stack(acc) for acc in new_states)


def setup_inputs(seed: int = 0) -> dict:
    key = jax.random.key(seed)
    keys = iter(jax.random.split(key, 64))

    def normal(shape, scale):
        return scale * jax.random.normal(next(keys), shape, jnp.float32)

    def gain(shape):
        return 1.0 + normal(shape, 0.01)

    def uniform(shape, lo, hi):
        return jax.random.uniform(next(keys), shape, jnp.float32, lo, hi)

    def dt_bias(shape):
        dt = jnp.exp(uniform(shape, math.log(1e-3), math.log(1e-1)))
        return dt + jnp.log(-jnp.expm1(-dt))

    L = DEPTH
    return {
        'x_prompt': normal((BATCH, SEQ, D_MODEL), 1.0),
        'x_sample': normal((DEC_BATCH, DEC_SEQ, D_MODEL), 1.0),
        'state_ssd_conv': normal((L, DEC_BATCH, CONV_W - 1, SSD_CONV_CH), 1.0),
        'state_ssd': normal((L, DEC_BATCH, SSD_HEADS, SSD_HEAD_DIM, SSD_STATE), 0.1),
        'state_s5_re': normal((L, DEC_BATCH, S5_GROUPS, S5_STATE), 0.1),
        'state_s5_im': normal((L, DEC_BATCH, S5_GROUPS, S5_STATE), 0.1),
        'state_gla': normal((L, DEC_BATCH, GLA_HEADS, GLA_DK, GLA_DV), 0.1),
        'state_gdn_conv': normal((L, DEC_BATCH, CONV_W - 1, GDN_CONV_CH), 1.0),
        'state_gdn': normal((L, DEC_BATCH, GDN_HEADS, GDN_DK, GDN_DV), 0.1),
        'ffn1_norm': gain((L, D_MODEL)),
        'ffn1_w_gate': normal((L, D_MODEL, D_FF), D_MODEL ** -0.5),
        'ffn1_w_up': normal((L, D_MODEL, D_FF), D_MODEL ** -0.5),
        'ffn1_w_down': normal((L, D_FF, D_MODEL), D_FF ** -0.5),
        'mix_norm': gain((L, D_MODEL)),
        'w_in': normal((L, D_MODEL, IN_COLS), D_MODEL ** -0.5),
        'ssd_conv_w': normal((L, CONV_W, SSD_CONV_CH), CONV_W ** -0.5),
        'ssd_conv_b': normal((L, SSD_CONV_CH), 0.01),
        'ssd_dt_bias': dt_bias((L, SSD_HEADS)),
        'ssd_a_log': jnp.log(uniform((L, SSD_HEADS), 1.0, 16.0)),
        'ssd_d': gain((L, SSD_HEADS)),
        'ssd_norm': gain((L, D_GROUP)),
        's5_a_re': -0.5 + normal((L, S5_GROUPS, S5_STATE), 0.01),
        's5_a_im': math.pi * jnp.arange(S5_STATE, dtype=jnp.float32) + normal((L, S5_GROUPS, S5_STATE), 0.01),
        's5_log_dt': uniform((L, S5_GROUPS), math.log(1e-3), math.log(1e-1)),
        's5_b_re': normal((L, S5_GROUPS, S5_STATE, S5_CH), (2 * S5_CH) ** -0.5),
        's5_b_im': normal((L, S5_GROUPS, S5_STATE, S5_CH), (2 * S5_CH) ** -0.5),
        's5_c_re': normal((L, S5_GROUPS, S5_CH, S5_STATE), S5_STATE ** -0.5),
        's5_c_im': normal((L, S5_GROUPS, S5_CH, S5_STATE), S5_STATE ** -0.5),
        's5_d': normal((L, D_GROUP), 1.0),
        's5_w_glu': normal((L, D_GROUP, D_GROUP), D_GROUP ** -0.5),
        's5_b_glu': normal((L, D_GROUP), 0.01),
        'gla_w_gate2': normal((L, GLA_GATE_RANK, GLA_HEADS * GLA_DK), GLA_GATE_RANK ** -0.5),
        'gla_b_gate2': normal((L, GLA_HEADS * GLA_DK), 0.01),
        'gla_norm': gain((L, D_GROUP)),
        'gdn_conv_w': normal((L, CONV_W, GDN_CONV_CH), CONV_W ** -0.5),
        'gdn_a_log': jnp.log(uniform((L, GDN_HEADS), 1.0, 16.0)),
        'gdn_dt_bias': dt_bias((L, GDN_HEADS)),
        'gdn_norm': gain((L, D_GROUP)),
        'w_out': normal((L, D_MIX, D_MODEL), D_MIX ** -0.5),
        'ffn2_norm': gain((L, D_MODEL)),
        'ffn2_w_gate': normal((L, D_MODEL, D_FF), D_MODEL ** -0.5),
        'ffn2_w_up': normal((L, D_MODEL, D_FF), D_MODEL ** -0.5),
        'ffn2_w_down': normal((L, D_FF, D_MODEL), D_FF ** -0.5),
        'final_norm': gain((D_MODEL,)),
    }


def reference(x_prompt, x_sample, state_ssd_conv, state_ssd, state_s5_re, state_s5_im, state_gla, state_gdn_conv,
              state_gdn, ffn1_norm, ffn1_w_gate, ffn1_w_up, ffn1_w_down, mix_norm, w_in, ssd_conv_w, ssd_conv_b,
              ssd_dt_bias, ssd_a_log, ssd_d, ssd_norm, s5_a_re, s5_a_im, s5_log_dt, s5_b_re, s5_b_im, s5_c_re,
              s5_c_im, s5_d, s5_w_glu, s5_b_glu, gla_w_gate2, gla_b_gate2, gla_norm, gdn_conv_w, gdn_a_log,
              gdn_dt_bias, gdn_norm, w_out, ffn2_norm, ffn2_w_gate, ffn2_w_up, ffn2_w_down, final_norm):
    weights = {
        'ffn1_norm': ffn1_norm, 'ffn1_w_gate': ffn1_w_gate, 'ffn1_w_up': ffn1_w_up, 'ffn1_w_down': ffn1_w_down,
        'mix_norm': mix_norm, 'w_in': w_in,
        'ssd_conv_w': ssd_conv_w, 'ssd_conv_b': ssd_conv_b, 'ssd_dt_bias': ssd_dt_bias, 'ssd_a_log': ssd_a_log,
        'ssd_d': ssd_d, 'ssd_norm': ssd_norm,
        's5_a_re': s5_a_re, 's5_a_im': s5_a_im, 's5_log_dt': s5_log_dt, 's5_b_re': s5_b_re, 's5_b_im': s5_b_im,
        's5_c_re': s5_c_re, 's5_c_im': s5_c_im, 's5_d': s5_d, 's5_w_glu': s5_w_glu, 's5_b_glu': s5_b_glu,
        'gla_w_gate2': gla_w_gate2, 'gla_b_gate2': gla_b_gate2, 'gla_norm': gla_norm,
        'gdn_conv_w': gdn_conv_w, 'gdn_a_log': gdn_a_log, 'gdn_dt_bias': gdn_dt_bias, 'gdn_norm': gdn_norm,
        'w_out': w_out,
        'ffn2_norm': ffn2_norm, 'ffn2_w_gate': ffn2_w_gate, 'ffn2_w_up': ffn2_w_up, 'ffn2_w_down': ffn2_w_down,
    }
    sample_states = (state_ssd_conv, state_ssd, state_s5_re, state_s5_im, state_gla, state_gdn_conv, state_gdn)
    n_prompt = x_prompt.shape[0]
    prompt_states = tuple(jnp.zeros((DEPTH, n_prompt) + s.shape[2:], s.dtype) for s in sample_states)
    y_prompt, (p_ssd_conv, p_ssd, p_s5_re, p_s5_im, p_gla, p_gdn_conv, p_gdn) = trunk(x_prompt, prompt_states, weights, final_norm)
    y_sample, (s_ssd_conv, s_ssd, s_s5_re, s_s5_im, s_gla, s_gdn_conv, s_gdn) = trunk(x_sample, sample_states, weights, final_norm)
    return (y_prompt, y_sample, p_ssd_conv, p_ssd, p_s5_re, p_s5_im, p_gla, p_gdn_conv, p_gdn,
            s_ssd_conv, s_ssd, s_s5_re, s_s5_im, s_gla, s_gdn_conv, s_gdn)
```

```python
import functools
import math

import numpy as np
import jax
import jax.numpy as jnp
from jax import lax
from jax.experimental import pallas as pl
from jax.experimental.pallas import tpu as pltpu

F32 = jnp.float32
BF16 = jnp.bfloat16
HIGHEST = lax.Precision.HIGHEST
EPS = 1e-6

LANES = 128
SUBLANES = 8
VMEM_LIMIT_BYTES = 56 * 1024 * 1024

D_MODEL = 2048
D_GROUP = 512
D_FF = 5504
D_FF_PAD = 5632
CONV_W = 4
CHUNK = 64

SSD_HEADS = 8
SSD_HEAD_DIM = 64
SSD_GROUPS = 2
SSD_STATE = 128
S5_GROUPS = 32
S5_CH = 16
S5_STATE = 64
S5_LANES = S5_GROUPS * S5_STATE
GLA_HEADS = 4
GLA_DK = 64
GLA_DV = 128
GLA_GATE_RANK = 16
GLA_GATE_TEMP = 16.0
GDN_HEADS = 4
GDN_DK = 128

COL_XBC = 0
COL_SSD_Z = 1024
COL_GDN_QKV = 1536
COL_S5_U = 3072
COL_GLA_V = 3584
COL_GLA_R = 4096
COL_GDN_Z = 4608
COL_GLA_Q = 5120
COL_GLA_K = 5376
COL_SMALL = 5632
PROJ_COLS = 5760
SM_DT = 0
SM_GR = 8
SM_BETA = 24
SM_DECAY = 28


def _mm(a, b):
    return jnp.dot(a.astype(BF16), b.astype(BF16), preferred_element_type=F32)


def _mm_nt(a, b):
    return lax.dot_general(a.astype(BF16), b.astype(BF16), (((1,), (1,)), ((), ())), preferred_element_type=F32)


def _mm_tn(a, b):
    return lax.dot_general(a.astype(BF16), b.astype(BF16), (((0,), (0,)), ((), ())), preferred_element_type=F32)


def _mm_hi(a, b):
    return jnp.dot(a, b, precision=HIGHEST, preferred_element_type=F32)


def _mm_nt_hi(a, b):
    return lax.dot_general(a, b, (((1,), (1,)), ((), ())), precision=HIGHEST, preferred_element_type=F32)


def _sigmoid(x):
    return 1.0 / (1.0 + jnp.exp(-x))


def _silu(x):
    return x * _sigmoid(x)


def _softplus(x):
    return jnp.maximum(x, 0.0) + jnp.log1p(jnp.exp(-jnp.abs(x)))


def _iota(shape, dim):
    return lax.broadcasted_iota(jnp.int32, shape, dim)


def _tril(n, strict=False):
    r, c = _iota((n, n), 0), _iota((n, n), 1)
    return (r > c) if strict else (r >= c)


def _eye(n):
    return (_iota((n, n), 0) == _iota((n, n), 1)).astype(F32)


def _row_to_col(row):
    n = row.shape[1]
    return jnp.sum(_eye(n) * row, axis=1, keepdims=True)


def _group_rmsnorm(y, width):
    parts = []
    for g in range(y.shape[1] // width):
        yg = y[:, g * width:(g + 1) * width]
        parts.append(yg * lax.rsqrt(jnp.mean(yg * yg, axis=1, keepdims=True) + EPS))
    return jnp.concatenate(parts, axis=1)


def _causal_conv(x, cbuf, w, rows):
    cbuf[SUBLANES:SUBLANES + rows, :] = x
    y = (w[0:1, :] * cbuf[5:5 + rows, :] + w[1:2, :] * cbuf[6:6 + rows, :]
         + w[2:3, :] * cbuf[7:7 + rows, :] + w[3:4, :] * x)
    cbuf[0:SUBLANES, :] = cbuf[rows:rows + SUBLANES, :]
    return y


def _params(*sem):
    return pltpu.CompilerParams(dimension_semantics=sem, vmem_limit_bytes=VMEM_LIMIT_BYTES)


def _ffn_kernel(x_ref, g_ref, wg_ref, wu_ref, wd_ref, fg_ref, o_ref, xn_ref, *, n_f, final_norm):
    f = pl.program_id(1)

    @pl.when(f == 0)
    def _():
        x = x_ref[...]
        xn = x * lax.rsqrt(jnp.mean(x * x, axis=1, keepdims=True) + EPS) * g_ref[...]
        xn_ref[...] = xn.astype(BF16)
        o_ref[...] = jnp.zeros_like(o_ref)

    xn = xn_ref[...]
    gate = jnp.dot(xn, wg_ref[...], preferred_element_type=F32)
    up = jnp.dot(xn, wu_ref[...], preferred_element_type=F32)
    h = (_silu(gate) * up).astype(BF16)
    o_ref[...] += jnp.dot(h, wd_ref[...], preferred_element_type=F32)

    @pl.when(f == n_f - 1)
    def _():
        y = x_ref[...] + 0.5 * o_ref[...]
        if final_norm:
            y = y * lax.rsqrt(jnp.mean(y * y, axis=1, keepdims=True) + EPS) * fg_ref[...]
        o_ref[...] = y


def _ffn(x, g, wg, wu, wd, fg, final_norm):
    m = x.shape[0]
    tm = min(1024, m)
    tf = 256
    n_f = D_FF_PAD // tf
    return pl.pallas_call(
        functools.partial(_ffn_kernel, n_f=n_f, final_norm=final_norm),
        grid=(m // tm, n_f),
        in_specs=[
            pl.BlockSpec((tm, D_MODEL), lambda i, f: (i, 0)),
            pl.BlockSpec((1, D_MODEL), lambda i, f: (0, 0)),
            pl.BlockSpec((D_MODEL, tf), lambda i, f: (0, f)),
            pl.BlockSpec((D_MODEL, tf), lambda i, f: (0, f)),
            pl.BlockSpec((tf, D_MODEL), lambda i, f: (f, 0)),
            pl.BlockSpec((1, D_MODEL), lambda i, f: (0, 0)),
        ],
        out_specs=pl.BlockSpec((tm, D_MODEL), lambda i, f: (i, 0)),
        out_shape=jax.ShapeDtypeStruct((m, D_MODEL), F32),
        scratch_shapes=[pltpu.VMEM((tm, D_MODEL), BF16)],
        compiler_params=_params("arbitrary", "arbitrary"),
        name="ffn",
    )(x, g, wg, wu, wd, fg)


def _in_proj_kernel(x_ref, g_ref, w_ref, o_ref, xn_ref):
    @pl.when(pl.program_id(1) == 0)
    def _():
        x = x_ref[...]
        xn = x * lax.rsqrt(jnp.mean(x * x, axis=1, keepdims=True) + EPS) * g_ref[...]
        xn_ref[...] = xn.astype(BF16)

    o_ref[...] = jnp.dot(xn_ref[...], w_ref[...], preferred_element_type=F32)


def _in_proj(x, g, w):
    m = x.shape[0]
    tm = min(1024, m)
    tn = 640
    return pl.pallas_call(
        _in_proj_kernel,
        grid=(m // tm, PROJ_COLS // tn),
        in_specs=[
            pl.BlockSpec((tm, D_MODEL), lambda i, n: (i, 0)),
            pl.BlockSpec((1, D_MODEL), lambda i, n: (0, 0)),
            pl.BlockSpec((D_MODEL, tn), lambda i, n: (0, n)),
        ],
        out_specs=pl.BlockSpec((tm, tn), lambda i, n: (i, n)),
        out_shape=jax.ShapeDtypeStruct((m, PROJ_COLS), F32),
        scratch_shapes=[pltpu.VMEM((tm, D_MODEL), BF16)],
        compiler_params=_params("arbitrary", "arbitrary"),
        name="in_proj",
    )(x, g, w)


def _out_proj_kernel(x_ref, y0_ref, y1_ref, y2_ref, y3_ref, w_ref, o_ref):
    acc = x_ref[...]
    for j, y_ref in enumerate((y0_ref, y1_ref, y2_ref, y3_ref)):
        acc = acc + jnp.dot(y_ref[...].astype(BF16), w_ref[j * D_GROUP:(j + 1) * D_GROUP, :],
                            preferred_element_type=F32)
    o_ref[...] = acc


def _out_proj(x, ys, w):
    m = x.shape[0]
    tm = min(512, m)
    yspec = pl.BlockSpec((tm, D_GROUP), lambda i: (i, 0))
    return pl.pallas_call(
        _out_proj_kernel,
        grid=(m // tm,),
        in_specs=[pl.BlockSpec((tm, D_MODEL), lambda i: (i, 0)), yspec, yspec, yspec, yspec,
                  pl.BlockSpec((D_MODEL, D_MODEL), lambda i: (0, 0))],
        out_specs=pl.BlockSpec((tm, D_MODEL), lambda i: (i, 0)),
        out_shape=jax.ShapeDtypeStruct((m, D_MODEL), F32),
        compiler_params=_params("arbitrary"),
        name="out_proj",
    )(x, *ys, w)


def _ssd_kernel(xbc_ref, z_ref, sm_ref, conv0_ref, h0_ref, cw_ref, cb_ref, dtb_ref, alog_ref, dex_ref, nrm_ref,
                y_ref, h_ref, cbuf, abuf, *, rows, chunk, valid):
    @pl.when(pl.program_id(1) == 0)
    def _():
        cbuf[0:SUBLANES, :] = conv0_ref[...]
        h_ref[...] = h0_ref[...]

    conv = _causal_conv(xbc_ref[...], cbuf, cw_ref[...], rows) + cb_ref[...]
    abuf[...] = _silu(conv)

    dt_all = _softplus(sm_ref[...] + dtb_ref[...])
    if valid < rows:
        dt_all = jnp.where(_iota((rows, LANES), 0) < valid, dt_all, 0.0)
    a_neg = -jnp.exp(alog_ref[...])
    tril = _tril(chunk).astype(F32)
    causal = _tril(chunk)
    left = _iota((1, LANES), 1) < SSD_HEAD_DIM
    top = _iota((LANES, 1), 0) < SSD_HEAD_DIM

    def pair(v, h0):
        return jnp.where(left, v[:, h0:h0 + 1], v[:, h0 + 1:h0 + 2])

    for c in range(rows // chunk):
        r0 = c * chunk
        xs = abuf[r0:r0 + chunk, 0:D_GROUP]
        dt = dt_all[r0:r0 + chunk, :]
        acs = _mm_hi(tril, dt * a_neg)
        acs_t = _mm_nt_hi(_eye(LANES), acs)
        exp_acs = jnp.exp(acs)
        to_end = jnp.exp(acs[chunk - 1:chunk, :] - acs)
        end_decay = jnp.exp(acs[chunk - 1:chunk, :])
        for g in range(SSD_GROUPS):
            bm = abuf[r0:r0 + chunk, D_GROUP + g * SSD_STATE:D_GROUP + (g + 1) * SSD_STATE]
            cm = abuf[r0:r0 + chunk, D_GROUP + (SSD_GROUPS + g) * SSD_STATE:D_GROUP + (SSD_GROUPS + g + 1) * SSD_STATE]
            cb = _mm_nt(cm, bm)
            for j in range(2):
                h0 = 4 * g + 2 * j
                lo = (2 * g + j) * LANES
                xdt = xs[:, lo:lo + LANES] * pair(dt, h0)
                y_pair = None
                for hh, keep in ((h0, left), (h0 + 1, jnp.logical_not(left))):
                    diff = acs[:, hh:hh + 1] - acs_t[hh:hh + 1, :]
                    decay = jnp.exp(jnp.where(causal, diff, -jnp.inf))
                    part = _mm(cb * decay, jnp.where(keep, xdt, 0.0))
                    y_pair = part if y_pair is None else y_pair + part
                st = h_ref[lo:lo + LANES, :]
                y_pair = y_pair + _mm_nt(cm, st) * pair(exp_acs, h0)
                y_ref[r0:r0 + chunk, lo:lo + LANES] = y_pair
                dec = jnp.where(top, end_decay[:, h0:h0 + 1], end_decay[:, h0 + 1:h0 + 2])
                h_ref[lo:lo + LANES, :] = st * dec + _mm_tn(xdt * pair(to_end, h0), bm)

    y = y_ref[...] + dex_ref[...] * abuf[:, 0:D_GROUP]
    y = y * _silu(z_ref[...])
    y_ref[...] = _group_rmsnorm(y, D_GROUP // SSD_GROUPS) * nrm_ref[...]


def _ssd(proj, conv0, h0, p, *, rows, chunk, valid):
    b, length, _ = proj.shape
    w_xbc = D_GROUP + 2 * SSD_GROUPS * SSD_STATE
    vec = lambda n: pl.BlockSpec((1, n), lambda bi, i: (0, 0))
    return pl.pallas_call(
        functools.partial(_ssd_kernel, rows=rows, chunk=chunk, valid=valid),
        grid=(b, length // rows),
        in_specs=[
            pl.BlockSpec((None, rows, w_xbc), lambda bi, i: (bi, i, COL_XBC // w_xbc)),
            pl.BlockSpec((None, rows, D_GROUP), lambda bi, i: (bi, i, COL_SSD_Z // D_GROUP)),
            pl.BlockSpec((None, rows, LANES), lambda bi, i: (bi, i, COL_SMALL // LANES)),
            pl.BlockSpec((None, SUBLANES, w_xbc), lambda bi, i: (bi, 0, 0)),
            pl.BlockSpec((None, D_GROUP, SSD_STATE), lambda bi, i: (bi, 0, 0)),
            pl.BlockSpec((CONV_W, w_xbc), lambda bi, i: (0, 0)),
            vec(w_xbc), vec(LANES), vec(LANES), vec(D_GROUP), vec(D_GROUP),
        ],
        out_specs=[
            pl.BlockSpec((None, rows, D_GROUP), lambda bi, i: (bi, i, 0)),
            pl.BlockSpec((None, D_GROUP, SSD_STATE), lambda bi, i: (bi, 0, 0)),
        ],
        out_shape=[jax.ShapeDtypeStruct((b, length, D_GROUP), F32),
                   jax.ShapeDtypeStruct((b, D_GROUP, SSD_STATE), F32)],
        scratch_shapes=[pltpu.VMEM((rows + SUBLANES, w_xbc), F32), pltpu.VMEM((rows, w_xbc), F32)],
        compiler_params=_params("arbitrary", "arbitrary"),
        name="ssd",
    )(proj, proj, proj, conv0, h0, p["ssd_conv_w"], p["ssd_conv_b"], p["ssd_dt_bias"], p["ssd_a_log"],
      p["ssd_d"], p["ssd_norm"])


def _s5_kernel(u_ref, h0re_ref, h0im_ref, apre_ref, apim_ref, bre_ref, bim_ref, cre_ref, cim_ref, d_ref, wglu_ref,
               bglu_ref, y_ref, hre_ref, him_ref, are, aim, bre, bim, *, rows, pad, valid):
    @pl.when(pl.program_id(1) == 0)
    def _():
        hre_ref[...] = h0re_ref[...]
        him_ref[...] = h0im_ref[...]

    zeros = jnp.zeros((pad, S5_LANES), F32)
    for buf in (are, aim, bre, bim):
        buf[0:pad, :] = zeros

    u = u_ref[...]
    n_blk = D_GROUP // LANES
    w_blk = S5_LANES // n_blk
    for j in range(n_blk):
        uj = u[:, j * LANES:(j + 1) * LANES]
        are[pad:pad + rows, j * w_blk:(j + 1) * w_blk] = _mm_hi(uj, bre_ref[j])
        aim[pad:pad + rows, j * w_blk:(j + 1) * w_blk] = _mm_hi(uj, bim_ref[j])

    src, dst = (are, aim), (bre, bim)
    shift = 1
    while shift < rows:
        ar = apre_ref[shift - 1:shift, :]
        ai = apim_ref[shift - 1:shift, :]
        xr, xi = src[0][pad:pad + rows, :], src[1][pad:pad + rows, :]
        sr, si = src[0][pad - shift:pad - shift + rows, :], src[1][pad - shift:pad - shift + rows, :]
        dst[0][pad:pad + rows, :] = xr + ar * sr - ai * si
        dst[1][pad:pad + rows, :] = xi + ar * si + ai * sr
        src, dst = dst, src
        shift *= 2

    pr, pi = hre_ref[...], him_ref[...]
    apr, api = apre_ref[...], apim_ref[...]
    hr = src[0][pad:pad + rows, :] + apr * pr - api * pi
    hi = src[1][pad:pad + rows, :] + apr * pi + api * pr
    hre_ref[...] = hr[valid - 1:valid, :]
    him_ref[...] = hi[valid - 1:valid, :]

    parts = []
    for j in range(n_blk):
        sl = slice(j * w_blk, (j + 1) * w_blk)
        parts.append(_mm_hi(hr[:, sl], cre_ref[j]) - _mm_hi(hi[:, sl], cim_ref[j]))
    y = jnp.concatenate(parts, axis=1) + d_ref[...] * u
    y = 0.5 * y * (1.0 + jnp.tanh(math.sqrt(2.0 / math.pi) * (y + 0.044715 * (y * y * y))))
    y_ref[...] = y * _sigmoid(_mm(y, wglu_ref[...]) + bglu_ref[...])


def _s5(proj, h0re, h0im, p, *, rows, valid):
    b, length, _ = proj.shape
    pad = max(SUBLANES, rows // 2)
    n_blk = D_GROUP // LANES
    w_blk = S5_LANES // n_blk
    vec = lambda n: pl.BlockSpec((1, n), lambda bi, i: (0, 0))
    full3 = lambda s: pl.BlockSpec(s, lambda bi, i: (0, 0, 0))
    state = pl.BlockSpec((None, 1, S5_LANES), lambda bi, i: (bi, 0, 0))
    scan_buf = pltpu.VMEM((pad + rows, S5_LANES), F32)
    y, hre, him = pl.pallas_call(
        functools.partial(_s5_kernel, rows=rows, pad=pad, valid=valid),
        grid=(b, length // rows),
        in_specs=[
            pl.BlockSpec((None, rows, D_GROUP), lambda bi, i: (bi, i, COL_S5_U // D_GROUP)),
            state, state,
            pl.BlockSpec((rows, S5_LANES), lambda bi, i: (0, 0)),
            pl.BlockSpec((rows, S5_LANES), lambda bi, i: (0, 0)),
            full3((n_blk, LANES, w_blk)), full3((n_blk, LANES, w_blk)),
            full3((n_blk, w_blk, LANES)), full3((n_blk, w_blk, LANES)),
            vec(D_GROUP), pl.BlockSpec((D_GROUP, D_GROUP), lambda bi, i: (0, 0)), vec(D_GROUP),
        ],
        out_specs=[pl.BlockSpec((None, rows, D_GROUP), lambda bi, i: (bi, i, 0)), state, state],
        out_shape=[jax.ShapeDtypeStruct((b, length, D_GROUP), F32),
                   jax.ShapeDtypeStruct((b, 1, S5_LANES), F32),
                   jax.ShapeDtypeStruct((b, 1, S5_LANES), F32)],
        scratch_shapes=[scan_buf, scan_buf, scan_buf, scan_buf],
        compiler_params=_params("arbitrary", "arbitrary"),
        name="s5",
    )(proj, h0re.reshape(b, 1, S5_LANES), h0im.reshape(b, 1, S5_LANES), p["s5_apow_re"][:rows], p["s5_apow_im"][:rows],
      p["s5_bblk_re"], p["s5_bblk_im"], p["s5_cblk_re"], p["s5_cblk_im"], p["s5_d"], p["s5_w_glu"], p["s5_b_glu"])
    return y, hre.reshape(b, S5_GROUPS, S5_STATE), him.reshape(b, S5_GROUPS, S5_STATE)


def _gla_kernel(q_ref, k_ref, v_ref, r_ref, sm_ref, s0_ref, wg_ref, bg_ref, nrm_ref, o_ref, s_ref,
                *, rows, chunk, valid):
    @pl.when(pl.program_id(1) == 0)
    def _():
        s_ref[...] = s0_ref[...]

    hk = GLA_HEADS * GLA_DK
    lg_all = -_softplus(-(_mm(sm_ref[...], wg_ref[...]) + bg_ref[...])) * (1.0 / GLA_GATE_TEMP)
    k_all = k_ref[...]
    if valid < rows:
        live = _iota((rows, hk), 0) < valid
        lg_all = jnp.where(live, lg_all, 0.0)
        k_all = jnp.where(live, k_all, 0.0)
    tril = _tril(chunk).astype(F32)
    causal = _tril(chunk)
    lane = _iota((1, hk), 1)

    for c in range(rows // chunk):
        r0 = c * chunk
        b = _mm_hi(tril, lg_all[r0:r0 + chunk, :])
        b_last = b[chunk - 1:chunk, :]
        q_dec = q_ref[r0:r0 + chunk, :] * (GLA_DK ** -0.5) * jnp.exp(b)
        k = k_all[r0:r0 + chunk, :]
        k_dec = k * jnp.exp(-b)
        k_end = k * jnp.exp(b_last - b)
        end_decay = _row_to_col(jnp.exp(b_last))
        st = s_ref[...]
        for h in range(GLA_HEADS):
            mine = (lane >= h * GLA_DK) & (lane < (h + 1) * GLA_DK)
            qh = jnp.where(mine, q_dec, 0.0)
            att = jnp.where(causal, _mm_nt(qh, k_dec), 0.0)
            vh = v_ref[r0:r0 + chunk, h * GLA_DV:(h + 1) * GLA_DV]
            o_ref[r0:r0 + chunk, h * GLA_DV:(h + 1) * GLA_DV] = _mm(att, vh) + _mm(qh, st)
            kv = _mm_tn(k_end, vh)
            rs = slice(h * GLA_DK, (h + 1) * GLA_DK)
            s_ref[rs, :] = st[rs, :] * end_decay[rs, :] + kv[rs, :]

    o = _group_rmsnorm(o_ref[...], GLA_DV) * nrm_ref[...]
    o_ref[...] = o * _silu(r_ref[...])


def _gla(proj, s0, p, *, rows, chunk, valid):
    b, length, _ = proj.shape
    hk = GLA_HEADS * GLA_DK
    vec = lambda n: pl.BlockSpec((1, n), lambda bi, i: (0, 0))
    st = pl.BlockSpec((None, hk, GLA_DV), lambda bi, i: (bi, 0, 0))
    return pl.pallas_call(
        functools.partial(_gla_kernel, rows=rows, chunk=chunk, valid=valid),
        grid=(b, length // rows),
        in_specs=[
            pl.BlockSpec((None, rows, hk), lambda bi, i: (bi, i, COL_GLA_Q // hk)),
            pl.BlockSpec((None, rows, hk), lambda bi, i: (bi, i, COL_GLA_K // hk)),
            pl.BlockSpec((None, rows, D_GROUP), lambda bi, i: (bi, i, COL_GLA_V // D_GROUP)),
            pl.BlockSpec((None, rows, D_GROUP), lambda bi, i: (bi, i, COL_GLA_R // D_GROUP)),
            pl.BlockSpec((None, rows, LANES), lambda bi, i: (bi, i, COL_SMALL // LANES)),
            st,
            pl.BlockSpec((LANES, hk), lambda bi, i: (0, 0)), vec(hk), vec(D_GROUP),
        ],
        out_specs=[pl.BlockSpec((None, rows, D_GROUP), lambda bi, i: (bi, i, 0)), st],
        out_shape=[jax.ShapeDtypeStruct((b, length, D_GROUP), F32),
                   jax.ShapeDtypeStruct((b, hk, GLA_DV), F32)],
        compiler_params=_params("arbitrary", "arbitrary"),
        name="gla",
    )(proj, proj, proj, proj, proj, s0, p["gla_w_gate2"], p["gla_b_gate2"], p["gla_norm"])


def _gdn_kernel(qkv_ref, z_ref, sm_ref, conv0_ref, s0_ref, cw_ref, alog_ref, dtb_ref, nrm_ref, o_ref, s_ref,
                cbuf, abuf, *, rows, chunk, valid):
    @pl.when(pl.program_id(1) == 0)
    def _():
        cbuf[0:SUBLANES, :] = conv0_ref[...]
        s_ref[...] = s0_ref[...]

    abuf[...] = _silu(_causal_conv(qkv_ref[...], cbuf, cw_ref[...], rows))
    sm = sm_ref[...]
    beta_all = _sigmoid(sm)
    g_all = -jnp.exp(alog_ref[...]) * _softplus(sm + dtb_ref[...])
    if valid < rows:
        live = _iota((rows, LANES), 0) < valid
        beta_all = jnp.where(live, beta_all, 0.0)
        g_all = jnp.where(live, g_all, 0.0)
    tril = _tril(chunk).astype(F32)
    causal = _tril(chunk)
    strict = _tril(chunk, strict=True)
    eye_c = _eye(chunk)
    hd = GDN_HEADS * GDN_DK

    def l2n(x):
        return x * lax.rsqrt(jnp.sum(x * x, axis=1, keepdims=True) + EPS)

    for c in range(rows // chunk):
        r0 = c * chunk
        gc = _mm_hi(tril, g_all[r0:r0 + chunk, :])
        gc_t = _mm_nt_hi(_eye(LANES), gc)
        beta = beta_all[r0:r0 + chunk, :]
        for h in range(GDN_HEADS):
            ls = slice(h * GDN_DK, (h + 1) * GDN_DK)
            q = l2n(abuf[r0:r0 + chunk, ls]) * (GDN_DK ** -0.5)
            k = l2n(abuf[r0:r0 + chunk, hd + h * GDN_DK:hd + (h + 1) * GDN_DK])
            v = abuf[r0:r0 + chunk, 2 * hd + h * GDN_DK:2 * hd + (h + 1) * GDN_DK]
            g_col = gc[:, SM_DECAY + h:SM_DECAY + h + 1]
            g_row = gc_t[SM_DECAY + h:SM_DECAY + h + 1, :]
            g_last = gc[chunk - 1:chunk, SM_DECAY + h:SM_DECAY + h + 1]
            b_col = beta[:, SM_BETA + h:SM_BETA + h + 1]
            decay = jnp.exp(jnp.where(causal, g_col - g_row, -jnp.inf))
            a_mat = jnp.where(strict, b_col * _mm_nt(k, k) * decay, 0.0)
            npow = -a_mat
            t_mat = eye_c + npow
            span = 2
            while span < chunk:
                npow = _mm_hi(npow, npow)
                t_mat = t_mat + _mm_hi(t_mat, npow)
                span *= 2
            u = _mm(t_mat, v * b_col)
            w = _mm(t_mat, k * (b_col * jnp.exp(g_col)))
            qk = _mm_nt(q, k) * decay
            st = s_ref[ls, :]
            v_new = u - _mm(w, st)
            o_ref[r0:r0 + chunk, ls] = _mm(q * jnp.exp(g_col), st) + _mm(qk, v_new)
            s_ref[ls, :] = st * jnp.exp(g_last) + _mm_tn(k * jnp.exp(g_last - g_col), v_new)

    o = _group_rmsnorm(o_ref[...], GDN_DK) * nrm_ref[...]
    o_ref[...] = o * _silu(z_ref[...])


def _gdn(proj, conv0, s0, p, *, rows, chunk, valid):
    b, length, _ = proj.shape
    hd = GDN_HEADS * GDN_DK
    w_qkv = 3 * hd
    vec = lambda n: pl.BlockSpec((1, n), lambda bi, i: (0, 0))
    st = pl.BlockSpec((None, hd, GDN_DK), lambda bi, i: (bi, 0, 0))
    return pl.pallas_call(
        functools.partial(_gdn_kernel, rows=rows, chunk=chunk, valid=valid),
        grid=(b, length // rows),
        in_specs=[
            pl.BlockSpec((None, rows, w_qkv), lambda bi, i: (bi, i, COL_GDN_QKV // w_qkv)),
            pl.BlockSpec((None, rows, D_GROUP), lambda bi, i: (bi, i, COL_GDN_Z // D_GROUP)),
            pl.BlockSpec((None, rows, LANES), lambda bi, i: (bi, i, COL_SMALL // LANES)),
            pl.BlockSpec((None, SUBLANES, w_qkv), lambda bi, i: (bi, 0, 0)),
            st,
            pl.BlockSpec((CONV_W, w_qkv), lambda bi, i: (0, 0)),
            vec(LANES), vec(LANES), vec(D_GROUP),
        ],
        out_specs=[pl.BlockSpec((None, rows, D_GROUP), lambda bi, i: (bi, i, 0)), st],
        out_shape=[jax.ShapeDtypeStruct((b, length, D_GROUP), F32),
                   jax.ShapeDtypeStruct((b, hd, GDN_DK), F32)],
        scratch_shapes=[pltpu.VMEM((rows + SUBLANES, w_qkv), F32), pltpu.VMEM((rows, w_qkv), F32)],
        compiler_params=_params("arbitrary", "arbitrary"),
        name="gdn",
    )(proj, proj, proj, conv0, s0, p["gdn_conv_w"], p["gdn_a_log"], p["gdn_dt_bias"], p["gdn_norm"])


def _lane_row(values, offset):
    return jnp.zeros((1, LANES), F32).at[0, offset:offset + values.shape[0]].set(values.astype(F32))


def _prep_layer(w, i, max_rows):
    row = lambda a: a[i].astype(F32).reshape(1, -1)
    pad_ff = D_FF_PAD - D_FF
    p = {}
    for name in ("ffn1", "ffn2"):
        p[name + "_norm"] = row(w[name + "_norm"])
        p[name + "_w_gate"] = jnp.pad(w[name + "_w_gate"][i].astype(BF16), ((0, 0), (0, pad_ff)))
        p[name + "_w_up"] = jnp.pad(w[name + "_w_up"][i].astype(BF16), ((0, 0), (0, pad_ff)))
        p[name + "_w_down"] = jnp.pad(w[name + "_w_down"][i].astype(BF16), ((0, pad_ff), (0, 0)))
    p["mix_norm"] = row(w["mix_norm"])

    wi = w["w_in"][i]
    seg = lambda start, width: wi[:, start:start + width]
    small = jnp.concatenate([seg(1536, 8), seg(3080, 16), seg(5656, 4), seg(5660, 4),
                             jnp.zeros((D_MODEL, LANES - 32), wi.dtype)], axis=1)
    p["w_in"] = jnp.concatenate([
        seg(512, 1024), seg(0, 512), seg(3608, 1536), seg(1544, 512), seg(2568, 512), seg(3096, 512),
        seg(5144, 512), seg(2056, 256), seg(2312, 256), small], axis=1).astype(BF16)
    p["w_out"] = w["w_out"][i].astype(BF16)

    p["ssd_conv_w"] = w["ssd_conv_w"][i].astype(F32)
    p["ssd_conv_b"] = row(w["ssd_conv_b"])
    p["ssd_dt_bias"] = _lane_row(w["ssd_dt_bias"][i], SM_DT)
    p["ssd_a_log"] = _lane_row(w["ssd_a_log"][i], SM_DT)
    p["ssd_d"] = jnp.repeat(w["ssd_d"][i].astype(F32), SSD_HEAD_DIM).reshape(1, D_GROUP)
    p["ssd_norm"] = row(w["ssd_norm"])

    a = lax.complex(w["s5_a_re"][i].astype(F32), w["s5_a_im"][i].astype(F32))
    dt = jnp.exp(w["s5_log_dt"][i].astype(F32))[:, None]
    a_bar = jnp.exp(a * dt)
    b_bar = ((a_bar - 1.0) / a)[..., None] * lax.complex(w["s5_b_re"][i].astype(F32), w["s5_b_im"][i].astype(F32))
    steps = jnp.arange(1, max_rows + 1, dtype=F32)[:, None, None]
    a_pow = jnp.exp((a * dt)[None] * steps).reshape(max_rows, S5_LANES)
    p["s5_apow_re"], p["s5_apow_im"] = jnp.real(a_pow), jnp.imag(a_pow)
    n_blk = D_GROUP // LANES
    gpb = S5_GROUPS // n_blk
    eye = jnp.eye(gpb, dtype=F32)

    def b_blocks(x):
        x = jnp.swapaxes(x, 1, 2).reshape(n_blk, gpb, S5_CH, S5_STATE)
        return jnp.einsum("jgip,gh->jgihp", x, eye).reshape(n_blk, LANES, gpb * S5_STATE)

    def c_blocks(x):
        x = x.reshape(n_blk, gpb, S5_CH, S5_STATE)
        return jnp.einsum("jgip,gh->jgphi", x, eye).reshape(n_blk, gpb * S5_STATE, LANES)

    p["s5_bblk_re"], p["s5_bblk_im"] = b_blocks(jnp.real(b_bar)), b_blocks(jnp.imag(b_bar))
    p["s5_cblk_re"] = c_blocks(w["s5_c_re"][i].astype(F32))
    p["s5_cblk_im"] = c_blocks(w["s5_c_im"][i].astype(F32))
    p["s5_d"] = row(w["s5_d"])
    p["s5_w_glu"] = w["s5_w_glu"][i].astype(BF16)
    p["s5_b_glu"] = row(w["s5_b_glu"])

    hk = GLA_HEADS * GLA_DK
    p["gla_w_gate2"] = jnp.zeros((LANES, hk), F32).at[SM_GR:SM_GR + GLA_GATE_RANK].set(w["gla_w_gate2"][i].astype(F32))
    p["gla_b_gate2"] = row(w["gla_b_gate2"])
    p["gla_norm"] = row(w["gla_norm"])

    p["gdn_conv_w"] = w["gdn_conv_w"][i].astype(F32)
    p["gdn_a_log"] = _lane_row(w["gdn_a_log"][i], SM_DECAY)
    p["gdn_dt_bias"] = _lane_row(w["gdn_dt_bias"][i], SM_DECAY)
    p["gdn_norm"] = row(w["gdn_norm"])
    return p


def _mixer_geometry(length):
    if length >= CHUNK:
        assert length % CHUNK == 0
        rows = 256 if length % 256 == 0 else CHUNK
        return rows, CHUNK, length
    assert length <= SUBLANES
    return SUBLANES, SUBLANES, SUBLANES


def _conv_buffer(state):
    return jnp.pad(state.astype(F32), ((0, 0), (SUBLANES - (CONV_W - 1), 0), (0, 0)))


def _trunk(x, states, layers, final_norm):
    bsz, length, _ = x.shape
    rows, chunk, lpad = _mixer_geometry(length)
    geo = dict(rows=rows, chunk=chunk, valid=min(length, rows))
    h = x.reshape(bsz * length, D_MODEL).astype(F32)
    fg = final_norm.astype(F32).reshape(1, D_MODEL)
    new_states = []
    for i, p in enumerate(layers):
        ssd_conv0, ssd_h0, s5_re0, s5_im0, gla_s0, gdn_conv0, gdn_s0 = states[i]
        h = _ffn(h, p["ffn1_norm"], p["ffn1_w_gate"], p["ffn1_w_up"], p["ffn1_w_down"], fg, False)
        proj = _in_proj(h, p["mix_norm"], p["w_in"]).reshape(bsz, length, PROJ_COLS)
        ssd_conv1 = proj[:, length - (CONV_W - 1):, COL_XBC:COL_XBC + 1024]
        gdn_conv1 = proj[:, length - (CONV_W - 1):, COL_GDN_QKV:COL_GDN_QKV + 1536]
        if lpad != length:
            proj = jnp.pad(proj, ((0, 0), (0, lpad - length), (0, 0)))
        y_ssd, ssd_h1 = _ssd(proj, _conv_buffer(ssd_conv0), ssd_h0.astype(F32).reshape(bsz, D_GROUP, SSD_STATE), p, **geo)
        y_s5, s5_re1, s5_im1 = _s5(proj, s5_re0.astype(F32), s5_im0.astype(F32), p, rows=rows, valid=geo["valid"])
        y_gla, gla_s1 = _gla(proj, gla_s0.astype(F32).reshape(bsz, GLA_HEADS * GLA_DK, GLA_DV), p, **geo)
        y_gdn, gdn_s1 = _gdn(proj, _conv_buffer(gdn_conv0), gdn_s0.astype(F32).reshape(bsz, GDN_HEADS * GDN_DK, GDN_DK), p, **geo)
        ys = [y[:, :length].reshape(bsz * length, D_GROUP) for y in (y_ssd, y_s5, y_gla, y_gdn)]
        h = _out_proj(h, ys, p["w_out"])
        h = _ffn(h, p["ffn2_norm"], p["ffn2_w_gate"], p["ffn2_w_up"], p["ffn2_w_down"], fg, i == len(layers) - 1)
        new_states.append((ssd_conv1, ssd_h1.reshape(bsz, SSD_HEADS, SSD_HEAD_DIM, SSD_STATE), s5_re1, s5_im1,
                           gla_s1.reshape(bsz, GLA_HEADS, GLA_DK, GLA_DV), gdn_conv1,
                           gdn_s1.reshape(bsz, GDN_HEADS, GDN_DK, GDN_DK)))
    stacked = tuple(jnp.stack([s[j] for s in new_states]) for j in range(7))
    return h.reshape(bsz, length, D_MODEL), stacked


def kernel(x_prompt, x_sample, state_ssd_conv, state_ssd, state_s5_re, state_s5_im, state_gla, state_gdn_conv, state_gdn, ffn1_norm, ffn1_w_gate, ffn1_w_up, ffn1_w_down, mix_norm, w_in, ssd_conv_w, ssd_conv_b, ssd_dt_bias, ssd_a_log, ssd_d, ssd_norm, s5_a_re, s5_a_im, s5_log_dt, s5_b_re, s5_b_im, s5_c_re, s5_c_im, s5_d, s5_w_glu, s5_b_glu, gla_w_gate2, gla_b_gate2, gla_norm, gdn_conv_w, gdn_a_log, gdn_dt_bias, gdn_norm, w_out, ffn2_norm, ffn2_w_gate, ffn2_w_up, ffn2_w_down, final_norm):
    w = dict(ffn1_norm=ffn1_norm, ffn1_w_gate=ffn1_w_gate, ffn1_w_up=ffn1_w_up, ffn1_w_down=ffn1_w_down,
             mix_norm=mix_norm, w_in=w_in, ssd_conv_w=ssd_conv_w, ssd_conv_b=ssd_conv_b, ssd_dt_bias=ssd_dt_bias,
             ssd_a_log=ssd_a_log, ssd_d=ssd_d, ssd_norm=ssd_norm, s5_a_re=s5_a_re, s5_a_im=s5_a_im,
             s5_log_dt=s5_log_dt, s5_b_re=s5_b_re, s5_b_im=s5_b_im, s5_c_re=s5_c_re, s5_c_im=s5_c_im, s5_d=s5_d,
             s5_w_glu=s5_w_glu, s5_b_glu=s5_b_glu, gla_w_gate2=gla_w_gate2, gla_b_gate2=gla_b_gate2,
             gla_norm=gla_norm, gdn_conv_w=gdn_conv_w, gdn_a_log=gdn_a_log, gdn_dt_bias=gdn_dt_bias,
             gdn_norm=gdn_norm, w_out=w_out, ffn2_norm=ffn2_norm, ffn2_w_gate=ffn2_w_gate, ffn2_w_up=ffn2_w_up,
             ffn2_w_down=ffn2_w_down)
    depth = w_in.shape[0]
    max_rows = max(_mixer_geometry(x_prompt.shape[1])[0], _mixer_geometry(x_sample.shape[1])[0])
    layers = [_prep_layer(w, i, max_rows) for i in range(depth)]

    sample_states = (state_ssd_conv, state_ssd, state_s5_re, state_s5_im, state_gla, state_gdn_conv, state_gdn)
    n_prompt = x_prompt.shape[0]
    prompt_layer = tuple(jnp.zeros((n_prompt,) + s.shape[2:], F32) for s in sample_states)
    y_prompt, p_states = _trunk(x_prompt, [prompt_layer] * depth, layers, final_norm)
    y_sample, s_states = _trunk(x_sample, [tuple(s[i] for s in sample_states) for i in range(depth)], layers, final_norm)
    return (y_prompt, y_sample) + p_states + s_states
```

```python
import functools
import math

import numpy as np
import jax
import jax.numpy as jnp
from jax import lax
from jax.experimental import pallas as pl
from jax.experimental.pallas import tpu as pltpu

F32 = jnp.float32
BF16 = jnp.bfloat16
HIGHEST = lax.Precision.HIGHEST
EPS = 1e-6

LANES = 128
SUBLANES = 8
VMEM_LIMIT_BYTES = 56 * 1024 * 1024

D_MODEL = 2048
D_GROUP = 512
D_FF = 5504
D_FF_PAD = 5632
CONV_W = 4
CHUNK = 64

SSD_HEADS = 8
SSD_HEAD_DIM = 64
SSD_GROUPS = 2
SSD_STATE = 128
S5_GROUPS = 32
S5_CH = 16
S5_STATE = 64
S5_LANES = S5_GROUPS * S5_STATE
S5_TAPS = 8
GLA_HEADS = 4
GLA_DK = 64
GLA_DV = 128
GLA_GATE_RANK = 16
GLA_GATE_TEMP = 16.0
GDN_HEADS = 4
GDN_DK = 128

COL_XBC = 0
COL_SSD_Z = 1024
COL_GDN_QKV = 1536
COL_S5_U = 3072
COL_GLA_V = 3584
COL_GLA_R = 4096
COL_GDN_Z = 4608
COL_GLA_Q = 5120
COL_GLA_K = 5376
COL_SMALL = 5632
PROJ_COLS = 5760
SM_DT = 0
SM_GR = 8
SM_BETA = 24
SM_DECAY = 28


def _mm(a, b):
    return jnp.dot(a.astype(BF16), b.astype(BF16), preferred_element_type=F32)


def _mm_nt(a, b):
    return lax.dot_general(a.astype(BF16), b.astype(BF16), (((1,), (1,)), ((), ())), preferred_element_type=F32)


def _mm_tn(a, b):
    return lax.dot_general(a.astype(BF16), b.astype(BF16), (((0,), (0,)), ((), ())), preferred_element_type=F32)


def _mm_hi(a, b):
    return jnp.dot(a, b, precision=HIGHEST, preferred_element_type=F32)


def _mm_nt_hi(a, b):
    return lax.dot_general(a, b, (((1,), (1,)), ((), ())), precision=HIGHEST, preferred_element_type=F32)


def _split(a):
    hi = a.astype(BF16)
    return hi, (a - hi.astype(F32)).astype(BF16)


def _mm3(a, b):
    (ah, al), (bh, bl) = a, b
    dot = functools.partial(jnp.dot, preferred_element_type=F32)
    return dot(ah, bh) + (dot(al, bh) + dot(ah, bl))


def _head_rows(first_lane):
    return (_iota((SUBLANES, LANES), 1) == _iota((SUBLANES, LANES), 0) + first_lane).astype(F32)


def _sigmoid(x):
    return 1.0 / (1.0 + jnp.exp(-x))


def _silu(x):
    return x * _sigmoid(x)


def _softplus(x):
    return jnp.maximum(x, 0.0) + jnp.log1p(jnp.exp(-jnp.abs(x)))


def _iota(shape, dim):
    return lax.broadcasted_iota(jnp.int32, shape, dim)


def _tril(n, strict=False):
    r, c = _iota((n, n), 0), _iota((n, n), 1)
    return (r > c) if strict else (r >= c)


def _eye(n):
    return (_iota((n, n), 0) == _iota((n, n), 1)).astype(F32)


def _row_to_col(row):
    n = row.shape[1]
    return jnp.sum(_eye(n) * row, axis=1, keepdims=True)


def _group_rmsnorm(y, width):
    parts = []
    for g in range(y.shape[1] // width):
        yg = y[:, g * width:(g + 1) * width]
        parts.append(yg * lax.rsqrt(jnp.mean(yg * yg, axis=1, keepdims=True) + EPS))
    return jnp.concatenate(parts, axis=1)


def _causal_conv(x, cbuf, w, rows):
    cbuf[SUBLANES:SUBLANES + rows, :] = x
    y = (w[0:1, :] * cbuf[5:5 + rows, :] + w[1:2, :] * cbuf[6:6 + rows, :]
         + w[2:3, :] * cbuf[7:7 + rows, :] + w[3:4, :] * x)
    cbuf[0:SUBLANES, :] = cbuf[rows:rows + SUBLANES, :]
    return y


def _params(*sem):
    return pltpu.CompilerParams(dimension_semantics=sem, vmem_limit_bytes=VMEM_LIMIT_BYTES)


def _ffn_kernel(x_ref, g_ref, wg_ref, wu_ref, wd_ref, fg_ref, o_ref, xn_ref, *, n_f, final_norm):
    f = pl.program_id(1)

    @pl.when(f == 0)
    def _():
        x = x_ref[...]
        xn = x * lax.rsqrt(jnp.mean(x * x, axis=1, keepdims=True) + EPS) * g_ref[...]
        xn_ref[...] = xn.astype(BF16)
        o_ref[...] = jnp.zeros_like(o_ref)

    xn = xn_ref[...]
    gate = jnp.dot(xn, wg_ref[...], preferred_element_type=F32)
    up = jnp.dot(xn, wu_ref[...], preferred_element_type=F32)
    h = (_silu(gate) * up).astype(BF16)
    o_ref[...] += jnp.dot(h, wd_ref[...], preferred_element_type=F32)

    @pl.when(f == n_f - 1)
    def _():
        y = x_ref[...] + 0.5 * o_ref[...]
        if final_norm:
            y = y * lax.rsqrt(jnp.mean(y * y, axis=1, keepdims=True) + EPS) * fg_ref[...]
        o_ref[...] = y


def _ffn(x, g, wg, wu, wd, fg, final_norm):
    m = x.shape[0]
    tm = min(1024, m)
    tf = 256
    n_f = D_FF_PAD // tf
    return pl.pallas_call(
        functools.partial(_ffn_kernel, n_f=n_f, final_norm=final_norm),
        grid=(m // tm, n_f),
        in_specs=[
            pl.BlockSpec((tm, D_MODEL), lambda i, f: (i, 0)),
            pl.BlockSpec((1, D_MODEL), lambda i, f: (0, 0)),
            pl.BlockSpec((D_MODEL, tf), lambda i, f: (0, f)),
            pl.BlockSpec((D_MODEL, tf), lambda i, f: (0, f)),
            pl.BlockSpec((tf, D_MODEL), lambda i, f: (f, 0)),
            pl.BlockSpec((1, D_MODEL), lambda i, f: (0, 0)),
        ],
        out_specs=pl.BlockSpec((tm, D_MODEL), lambda i, f: (i, 0)),
        out_shape=jax.ShapeDtypeStruct((m, D_MODEL), F32),
        scratch_shapes=[pltpu.VMEM((tm, D_MODEL), BF16)],
        compiler_params=_params("arbitrary", "arbitrary"),
        name="ffn",
    )(x, g, wg, wu, wd, fg)


def _in_proj_kernel(x_ref, g_ref, w_ref, o_ref, xn_ref):
    @pl.when(pl.program_id(1) == 0)
    def _():
        x = x_ref[...]
        xn = x * lax.rsqrt(jnp.mean(x * x, axis=1, keepdims=True) + EPS) * g_ref[...]
        xn_ref[...] = xn.astype(BF16)

    o_ref[...] = jnp.dot(xn_ref[...], w_ref[...], preferred_element_type=F32)


def _in_proj(x, g, w):
    m = x.shape[0]
    tm = min(1024, m)
    tn = 640
    return pl.pallas_call(
        _in_proj_kernel,
        grid=(m // tm, PROJ_COLS // tn),
        in_specs=[
            pl.BlockSpec((tm, D_MODEL), lambda i, n: (i, 0)),
            pl.BlockSpec((1, D_MODEL), lambda i, n: (0, 0)),
            pl.BlockSpec((D_MODEL, tn), lambda i, n: (0, n)),
        ],
        out_specs=pl.BlockSpec((tm, tn), lambda i, n: (i, n)),
        out_shape=jax.ShapeDtypeStruct((m, PROJ_COLS), F32),
        scratch_shapes=[pltpu.VMEM((tm, D_MODEL), BF16)],
        compiler_params=_params("arbitrary", "arbitrary"),
        name="in_proj",
    )(x, g, w)


def _out_proj_kernel(x_ref, y0_ref, y1_ref, y2_ref, y3_ref, w_ref, o_ref):
    acc = x_ref[...]
    for j, y_ref in enumerate((y0_ref, y1_ref, y2_ref, y3_ref)):
        acc = acc + jnp.dot(y_ref[...].astype(BF16), w_ref[j * D_GROUP:(j + 1) * D_GROUP, :],
                            preferred_element_type=F32)
    o_ref[...] = acc


def _out_proj(x, ys, w):
    m = x.shape[0]
    tm = min(512, m)
    yspec = pl.BlockSpec((tm, D_GROUP), lambda i: (i, 0))
    return pl.pallas_call(
        _out_proj_kernel,
        grid=(m // tm,),
        in_specs=[pl.BlockSpec((tm, D_MODEL), lambda i: (i, 0)), yspec, yspec, yspec, yspec,
                  pl.BlockSpec((D_MODEL, D_MODEL), lambda i: (0, 0))],
        out_specs=pl.BlockSpec((tm, D_MODEL), lambda i: (i, 0)),
        out_shape=jax.ShapeDtypeStruct((m, D_MODEL), F32),
        compiler_params=_params("arbitrary"),
        name="out_proj",
    )(x, *ys, w)


def _ssd_kernel(xbc_ref, z_ref, sm_ref, conv0_ref, h0_ref, cw_ref, cb_ref, dtb_ref, alog_ref, dex_ref, nrm_ref,
                y_ref, h_ref, cbuf, abuf, *, bb, rows, chunk, valid):
    @pl.when(pl.program_id(1) == 0)
    def _():
        cbuf[:, 0:SUBLANES, :] = conv0_ref[...]
        h_ref[...] = h0_ref[...]

    a_neg = -jnp.exp(alog_ref[...])
    tril = _tril(chunk).astype(F32)
    causal = _tril(chunk)
    left = _iota((1, LANES), 1) < SSD_HEAD_DIM
    top = _iota((LANES, 1), 0) < SSD_HEAD_DIM
    head_rows = _head_rows(SM_DT)

    def pair(v, h0):
        return jnp.where(left, v[:, h0:h0 + 1], v[:, h0 + 1:h0 + 2])

    dt_all = []
    for b in range(bb):
        conv = _causal_conv(xbc_ref[b], cbuf.at[b], cw_ref[...], rows) + cb_ref[...]
        abuf[b] = _silu(conv)
        dt = _softplus(sm_ref[b] + dtb_ref[...])
        if valid < rows:
            dt = jnp.where(_iota((rows, LANES), 0) < valid, dt, 0.0)
        dt_all.append(dt)

    for c in range(rows // chunk):
        r0 = c * chunk
        for b in range(bb):
            xs = abuf[b, r0:r0 + chunk, 0:D_GROUP]
            dt = dt_all[b][r0:r0 + chunk, :]
            acs = _mm_hi(tril, dt * a_neg)
            acs_t = _mm_nt_hi(head_rows, acs)
            exp_acs = jnp.exp(acs)
            to_end = jnp.exp(acs[chunk - 1:chunk, :] - acs)
            end_decay = jnp.exp(acs[chunk - 1:chunk, :])
            for g in range(SSD_GROUPS):
                bm = abuf[b, r0:r0 + chunk, D_GROUP + g * SSD_STATE:D_GROUP + (g + 1) * SSD_STATE]
                cm = abuf[b, r0:r0 + chunk,
                          D_GROUP + (SSD_GROUPS + g) * SSD_STATE:D_GROUP + (SSD_GROUPS + g + 1) * SSD_STATE]
                cb = _mm_nt(cm, bm)
                for j in range(2):
                    h0 = 4 * g + 2 * j
                    lo = (2 * g + j) * LANES
                    xdt = xs[:, lo:lo + LANES] * pair(dt, h0)
                    y_pair = None
                    for hh, keep in ((h0, left), (h0 + 1, jnp.logical_not(left))):
                        diff = acs[:, hh:hh + 1] - acs_t[hh:hh + 1, :]
                        decay = jnp.exp(jnp.where(causal, diff, -jnp.inf))
                        part = _mm(cb * decay, jnp.where(keep, xdt, 0.0))
                        y_pair = part if y_pair is None else y_pair + part
                    st = h_ref[b, lo:lo + LANES, :]
                    y_pair = y_pair + _mm_nt(cm, st) * pair(exp_acs, h0)
                    y_ref[b, r0:r0 + chunk, lo:lo + LANES] = y_pair
                    dec = jnp.where(top, end_decay[:, h0:h0 + 1], end_decay[:, h0 + 1:h0 + 2])
                    h_ref[b, lo:lo + LANES, :] = st * dec + _mm_tn(xdt * pair(to_end, h0), bm)

    for b in range(bb):
        y = y_ref[b] + dex_ref[...] * abuf[b, :, 0:D_GROUP]
        y = y * _silu(z_ref[b])
        y_ref[b] = _group_rmsnorm(y, D_GROUP // SSD_GROUPS) * nrm_ref[...]


def _ssd(proj, conv0, h0, p, *, bb, rows, chunk, valid):
    b, length, _ = proj.shape
    w_xbc = D_GROUP + 2 * SSD_GROUPS * SSD_STATE
    vec = lambda n: pl.BlockSpec((1, n), lambda bi, i: (0, 0))
    return pl.pallas_call(
        functools.partial(_ssd_kernel, bb=bb, rows=rows, chunk=chunk, valid=valid),
        grid=(b // bb, length // rows),
        in_specs=[
            pl.BlockSpec((bb, rows, w_xbc), lambda bi, i: (bi, i, COL_XBC // w_xbc)),
            pl.BlockSpec((bb, rows, D_GROUP), lambda bi, i: (bi, i, COL_SSD_Z // D_GROUP)),
            pl.BlockSpec((bb, rows, LANES), lambda bi, i: (bi, i, COL_SMALL // LANES)),
            pl.BlockSpec((bb, SUBLANES, w_xbc), lambda bi, i: (bi, 0, 0)),
            pl.BlockSpec((bb, D_GROUP, SSD_STATE), lambda bi, i: (bi, 0, 0)),
            pl.BlockSpec((CONV_W, w_xbc), lambda bi, i: (0, 0)),
            vec(w_xbc), vec(LANES), vec(LANES), vec(D_GROUP), vec(D_GROUP),
        ],
        out_specs=[
            pl.BlockSpec((bb, rows, D_GROUP), lambda bi, i: (bi, i, 0)),
            pl.BlockSpec((bb, D_GROUP, SSD_STATE), lambda bi, i: (bi, 0, 0)),
        ],
        out_shape=[jax.ShapeDtypeStruct((b, length, D_GROUP), F32),
                   jax.ShapeDtypeStruct((b, D_GROUP, SSD_STATE), F32)],
        scratch_shapes=[pltpu.VMEM((bb, rows + SUBLANES, w_xbc), F32), pltpu.VMEM((bb, rows, w_xbc), F32)],
        compiler_params=_params("arbitrary", "arbitrary"),
        name="ssd",
    )(proj, proj, proj, conv0, h0, p["ssd_conv_w"], p["ssd_conv_b"], p["ssd_dt_bias"], p["ssd_a_log"],
      p["ssd_d"], p["ssd_norm"])


def _s5_kernel(u_ref, h0re_ref, h0im_ref, apre_ref, apim_ref, wre_ref, wim_ref, cre_ref, cim_ref, d_ref, wglu_ref,
               bglu_ref, y_ref, hre_ref, him_ref, ubuf, *scan, bb, rows, pad, valid):
    @pl.when(pl.program_id(1) == 0)
    def _():
        hre_ref[...] = h0re_ref[...]
        him_ref[...] = h0im_ref[...]

    m = bb * rows
    n_blk = D_GROUP // LANES
    w_blk = S5_LANES // n_blk
    ubuf[:, 0:SUBLANES, :] = jnp.zeros((bb, SUBLANES, D_GROUP), F32)
    ubuf[:, SUBLANES:SUBLANES + rows, :] = u_ref[...]
    for buf in scan:
        buf[0:pad, :] = jnp.zeros((pad, w_blk), F32)

    parts = []
    for j in range(n_blk):
        sl = slice(j * w_blk, (j + 1) * w_blk)
        taps = [ubuf[:, SUBLANES - t:SUBLANES - t + rows, j * LANES:(j + 1) * LANES].reshape(m, LANES)
                for t in range(S5_TAPS)]
        lagged = jnp.concatenate(taps, axis=1).astype(BF16)
        xr = jnp.dot(lagged, wre_ref[j], preferred_element_type=F32)
        xi = jnp.dot(lagged, wim_ref[j], preferred_element_type=F32)
        if rows > S5_TAPS:
            src, dst = scan[0:2], scan[2:4]
            src[0][pad:pad + rows, :] = xr
            src[1][pad:pad + rows, :] = xi
            shift = S5_TAPS
            while shift < rows:
                ar = apre_ref[shift - 1:shift, sl]
                ai = apim_ref[shift - 1:shift, sl]
                cr, ci = src[0][pad:pad + rows, :], src[1][pad:pad + rows, :]
                sr, si = src[0][pad - shift:pad - shift + rows, :], src[1][pad - shift:pad - shift + rows, :]
                dst[0][pad:pad + rows, :] = cr + ar * sr - ai * si
                dst[1][pad:pad + rows, :] = ci + ar * si + ai * sr
                src, dst = dst, src
                shift *= 2
            xr, xi = src[0][pad:pad + rows, :], src[1][pad:pad + rows, :]
        pr, pi = hre_ref[:, :, sl], him_ref[:, :, sl]
        apr, api = apre_ref[:, sl][None], apim_ref[:, sl][None]
        hr = xr.reshape(bb, rows, w_blk) + apr * pr - api * pi
        hi = xi.reshape(bb, rows, w_blk) + apr * pi + api * pr
        hre_ref[:, :, sl] = hr[:, valid - 1:valid, :]
        him_ref[:, :, sl] = hi[:, valid - 1:valid, :]
        parts.append(_mm(hr.reshape(m, w_blk), cre_ref[j]) - _mm(hi.reshape(m, w_blk), cim_ref[j]))

    y = jnp.concatenate(parts, axis=1) + d_ref[...] * u_ref[...].reshape(m, D_GROUP)
    y = 0.5 * y * (1.0 + jnp.tanh(math.sqrt(2.0 / math.pi) * (y + 0.044715 * (y * y * y))))
    y = y * _sigmoid(_mm(y, wglu_ref[...]) + bglu_ref[...])
    y_ref[...] = y.reshape(bb, rows, D_GROUP)


def _s5(proj, h0re, h0im, p, *, bb, rows, valid):
    b, length, _ = proj.shape
    assert rows == S5_TAPS or bb == 1
    pad = rows // 2
    n_blk = D_GROUP // LANES
    w_blk = S5_LANES // n_blk
    vec = lambda n: pl.BlockSpec((1, n), lambda bi, i: (0, 0))
    full3 = lambda s: pl.BlockSpec(s, lambda bi, i: (0, 0, 0))
    state = pl.BlockSpec((bb, 1, S5_LANES), lambda bi, i: (bi, 0, 0))
    scan = [pltpu.VMEM((pad + rows, w_blk), F32)] * 4 if rows > S5_TAPS else []
    y, hre, him = pl.pallas_call(
        functools.partial(_s5_kernel, bb=bb, rows=rows, pad=pad, valid=valid),
        grid=(b // bb, length // rows),
        in_specs=[
            pl.BlockSpec((bb, rows, D_GROUP), lambda bi, i: (bi, i, COL_S5_U // D_GROUP)),
            state, state,
            pl.BlockSpec((rows, S5_LANES), lambda bi, i: (0, 0)),
            pl.BlockSpec((rows, S5_LANES), lambda bi, i: (0, 0)),
            full3((n_blk, S5_TAPS * LANES, w_blk)), full3((n_blk, S5_TAPS * LANES, w_blk)),
            full3((n_blk, w_blk, LANES)), full3((n_blk, w_blk, LANES)),
            vec(D_GROUP), pl.BlockSpec((D_GROUP, D_GROUP), lambda bi, i: (0, 0)), vec(D_GROUP),
        ],
        out_specs=[pl.BlockSpec((bb, rows, D_GROUP), lambda bi, i: (bi, i, 0)), state, state],
        out_shape=[jax.ShapeDtypeStruct((b, length, D_GROUP), F32),
                   jax.ShapeDtypeStruct((b, 1, S5_LANES), F32),
                   jax.ShapeDtypeStruct((b, 1, S5_LANES), F32)],
        scratch_shapes=[pltpu.VMEM((bb, SUBLANES + rows, D_GROUP), F32)] + scan,
        compiler_params=_params("arbitrary", "arbitrary"),
        name="s5",
    )(proj, h0re.reshape(b, 1, S5_LANES), h0im.reshape(b, 1, S5_LANES), p["s5_apow_re"][:rows], p["s5_apow_im"][:rows],
      p["s5_wlag_re"], p["s5_wlag_im"], p["s5_cblk_re"], p["s5_cblk_im"], p["s5_d"], p["s5_w_glu"], p["s5_b_glu"])
    return y, hre.reshape(b, S5_GROUPS, S5_STATE), him.reshape(b, S5_GROUPS, S5_STATE)


def _gla_kernel(q_ref, k_ref, v_ref, r_ref, sm_ref, s0_ref, wg_ref, bg_ref, nrm_ref, o_ref, s_ref,
                *, bb, rows, chunk, valid):
    @pl.when(pl.program_id(1) == 0)
    def _():
        s_ref[...] = s0_ref[...]

    hk = GLA_HEADS * GLA_DK
    tril = _tril(chunk).astype(F32)
    causal = _tril(chunk)
    lane = _iota((1, hk), 1)
    lg_all, k_all = [], []
    for b in range(bb):
        lg = -_softplus(-(_mm(sm_ref[b], wg_ref[...]) + bg_ref[...])) * (1.0 / GLA_GATE_TEMP)
        k = k_ref[b]
        if valid < rows:
            live = _iota((rows, hk), 0) < valid
            lg = jnp.where(live, lg, 0.0)
            k = jnp.where(live, k, 0.0)
        lg_all.append(lg)
        k_all.append(k)

    for c in range(rows // chunk):
        r0 = c * chunk
        for b in range(bb):
            cum = _mm_hi(tril, lg_all[b][r0:r0 + chunk, :])
            cum_last = cum[chunk - 1:chunk, :]
            q_dec = q_ref[b, r0:r0 + chunk, :] * (GLA_DK ** -0.5) * jnp.exp(cum)
            k = k_all[b][r0:r0 + chunk, :]
            k_dec = k * jnp.exp(-cum)
            k_end = k * jnp.exp(cum_last - cum)
            end_decay = _row_to_col(jnp.exp(cum_last))
            st = s_ref[b]
            for h in range(GLA_HEADS):
                mine = (lane >= h * GLA_DK) & (lane < (h + 1) * GLA_DK)
                qh = jnp.where(mine, q_dec, 0.0)
                att = jnp.where(causal, _mm_nt(qh, k_dec), 0.0)
                vh = v_ref[b, r0:r0 + chunk, h * GLA_DV:(h + 1) * GLA_DV]
                o_ref[b, r0:r0 + chunk, h * GLA_DV:(h + 1) * GLA_DV] = _mm(att, vh) + _mm(qh, st)
                kv = _mm_tn(k_end, vh)
                rs = slice(h * GLA_DK, (h + 1) * GLA_DK)
                s_ref[b, rs, :] = st[rs, :] * end_decay[rs, :] + kv[rs, :]

    for b in range(bb):
        o = _group_rmsnorm(o_ref[b], GLA_DV) * nrm_ref[...]
        o_ref[b] = o * _silu(r_ref[b])


def _gla(proj, s0, p, *, bb, rows, chunk, valid):
    b, length, _ = proj.shape
    hk = GLA_HEADS * GLA_DK
    vec = lambda n: pl.BlockSpec((1, n), lambda bi, i: (0, 0))
    st = pl.BlockSpec((bb, hk, GLA_DV), lambda bi, i: (bi, 0, 0))
    return pl.pallas_call(
        functools.partial(_gla_kernel, bb=bb, rows=rows, chunk=chunk, valid=valid),
        grid=(b // bb, length // rows),
        in_specs=[
            pl.BlockSpec((bb, rows, hk), lambda bi, i: (bi, i, COL_GLA_Q // hk)),
            pl.BlockSpec((bb, rows, hk), lambda bi, i: (bi, i, COL_GLA_K // hk)),
            pl.BlockSpec((bb, rows, D_GROUP), lambda bi, i: (bi, i, COL_GLA_V // D_GROUP)),
            pl.BlockSpec((bb, rows, D_GROUP), lambda bi, i: (bi, i, COL_GLA_R // D_GROUP)),
            pl.BlockSpec((bb, rows, LANES), lambda bi, i: (bi, i, COL_SMALL // LANES)),
            st,
            pl.BlockSpec((LANES, hk), lambda bi, i: (0, 0)), vec(hk), vec(D_GROUP),
        ],
        out_specs=[pl.BlockSpec((bb, rows, D_GROUP), lambda bi, i: (bi, i, 0)), st],
        out_shape=[jax.ShapeDtypeStruct((b, length, D_GROUP), F32),
                   jax.ShapeDtypeStruct((b, hk, GLA_DV), F32)],
        compiler_params=_params("arbitrary", "arbitrary"),
        name="gla",
    )(proj, proj, proj, proj, proj, s0, p["gla_w_gate2"], p["gla_b_gate2"], p["gla_norm"])


def _gdn_kernel(qkv_ref, z_ref, sm_ref, conv0_ref, s0_ref, cw_ref, alog_ref, dtb_ref, nrm_ref, o_ref, s_ref,
                cbuf, abuf, *, bb, rows, chunk, valid):
    @pl.when(pl.program_id(1) == 0)
    def _():
        cbuf[:, 0:SUBLANES, :] = conv0_ref[...]
        s_ref[...] = s0_ref[...]

    a_neg = -jnp.exp(alog_ref[...])
    tril = _tril(chunk).astype(F32)
    causal = _tril(chunk)
    strict = _tril(chunk, strict=True)
    eye_c = _eye(chunk)
    gate_rows = _head_rows(SM_BETA)
    hd = GDN_HEADS * GDN_DK
    n_c = rows // chunk

    def l2n(x):
        return x * lax.rsqrt(jnp.sum(x * x, axis=1, keepdims=True) + EPS)

    beta_all, g_all = [], []
    for b in range(bb):
        abuf[b] = _silu(_causal_conv(qkv_ref[b], cbuf.at[b], cw_ref[...], rows))
        sm = sm_ref[b]
        beta = _sigmoid(sm)
        g = a_neg * _softplus(sm + dtb_ref[...])
        if valid < rows:
            live = _iota((rows, LANES), 0) < valid
            beta = jnp.where(live, beta, 0.0)
            g = jnp.where(live, g, 0.0)
        beta_all.append(beta)
        g_all.append(g)

    seqs = [(b, c) for b in range(bb) for c in range(n_c)]
    gc = {s: _mm_hi(tril, g_all[s[0]][s[1] * chunk:(s[1] + 1) * chunk, :]) for s in seqs}
    gc_t = {s: _mm_nt_hi(gate_rows, gc[s]) for s in seqs}

    units = [(b, c, h) for (b, c) in seqs for h in range(GDN_HEADS)]
    q, k, v, g_col, b_col, decay = {}, {}, {}, {}, {}, {}
    for (b, c, h) in units:
        r = slice(c * chunk, (c + 1) * chunk)
        q[b, c, h] = l2n(abuf[b, r, h * GDN_DK:(h + 1) * GDN_DK]) * (GDN_DK ** -0.5)
        k[b, c, h] = l2n(abuf[b, r, hd + h * GDN_DK:hd + (h + 1) * GDN_DK])
        v[b, c, h] = abuf[b, r, 2 * hd + h * GDN_DK:2 * hd + (h + 1) * GDN_DK]
        g_col[b, c, h] = gc[b, c][:, SM_DECAY + h:SM_DECAY + h + 1]
        g_row = gc_t[b, c][SM_DECAY - SM_BETA + h:SM_DECAY - SM_BETA + h + 1, :]
        b_col[b, c, h] = beta_all[b][r, SM_BETA + h:SM_BETA + h + 1]
        decay[b, c, h] = jnp.exp(jnp.where(causal, g_col[b, c, h] - g_row, -jnp.inf))
    kk = {x: _mm_nt(k[x], k[x]) for x in units}
    npow = {x: jnp.where(strict, -(b_col[x] * kk[x] * decay[x]), 0.0) for x in units}
    t_mat = {x: eye_c + npow[x] for x in units}
    span = 2
    parts = {x: _split(npow[x]) for x in units}
    while span < chunk:
        parts = {x: _split(_mm3(parts[x], parts[x])) for x in units}
        t_mat = {x: t_mat[x] + _mm3(_split(t_mat[x]), parts[x]) for x in units}
        span *= 2
    u = {x: _mm(t_mat[x], v[x] * b_col[x]) for x in units}
    w = {x: _mm(t_mat[x], k[x] * (b_col[x] * jnp.exp(g_col[x]))) for x in units}
    qk = {x: _mm_nt(q[x], k[x]) * decay[x] for x in units}

    for c in range(n_c):
        now = [(b, c, h) for b in range(bb) for h in range(GDN_HEADS)]
        st = {x: s_ref[x[0], x[2] * GDN_DK:(x[2] + 1) * GDN_DK, :] for x in now}
        v_new = {x: u[x] - _mm(w[x], st[x]) for x in now}
        for x in now:
            b, _, h = x
            g_last = gc[b, c][chunk - 1:chunk, SM_DECAY + h:SM_DECAY + h + 1]
            o_ref[b, c * chunk:(c + 1) * chunk, h * GDN_DK:(h + 1) * GDN_DK] = (
                _mm(q[x] * jnp.exp(g_col[x]), st[x]) + _mm(qk[x], v_new[x]))
            s_ref[b, h * GDN_DK:(h + 1) * GDN_DK, :] = (
                st[x] * jnp.exp(g_last) + _mm_tn(k[x] * jnp.exp(g_last - g_col[x]), v_new[x]))

    for b in range(bb):
        o = _group_rmsnorm(o_ref[b], GDN_DK) * nrm_ref[...]
        o_ref[b] = o * _silu(z_ref[b])


def _gdn(proj, conv0, s0, p, *, bb, rows, chunk, valid):
    b, length, _ = proj.shape
    hd = GDN_HEADS * GDN_DK
    w_qkv = 3 * hd
    vec = lambda n: pl.BlockSpec((1, n), lambda bi, i: (0, 0))
    st = pl.BlockSpec((bb, hd, GDN_DK), lambda bi, i: (bi, 0, 0))
    return pl.pallas_call(
        functools.partial(_gdn_kernel, bb=bb, rows=rows, chunk=chunk, valid=valid),
        grid=(b // bb, length // rows),
        in_specs=[
            pl.BlockSpec((bb, rows, w_qkv), lambda bi, i: (bi, i, COL_GDN_QKV // w_qkv)),
            pl.BlockSpec((bb, rows, D_GROUP), lambda bi, i: (bi, i, COL_GDN_Z // D_GROUP)),
            pl.BlockSpec((bb, rows, LANES), lambda bi, i: (bi, i, COL_SMALL // LANES)),
            pl.BlockSpec((bb, SUBLANES, w_qkv), lambda bi, i: (bi, 0, 0)),
            st,
            pl.BlockSpec((CONV_W, w_qkv), lambda bi, i: (0, 0)),
            vec(LANES), vec(LANES), vec(D_GROUP),
        ],
        out_specs=[pl.BlockSpec((bb, rows, D_GROUP), lambda bi, i: (bi, i, 0)), st],
        out_shape=[jax.ShapeDtypeStruct((b, length, D_GROUP), F32),
                   jax.ShapeDtypeStruct((b, hd, GDN_DK), F32)],
        scratch_shapes=[pltpu.VMEM((bb, rows + SUBLANES, w_qkv), F32), pltpu.VMEM((bb, rows, w_qkv), F32)],
        compiler_params=_params("arbitrary", "arbitrary"),
        name="gdn",
    )(proj, proj, proj, conv0, s0, p["gdn_conv_w"], p["gdn_a_log"], p["gdn_dt_bias"], p["gdn_norm"])


def _lane_row(values, offset):
    return jnp.zeros((1, LANES), F32).at[0, offset:offset + values.shape[0]].set(values.astype(F32))


def _prep_layer(w, i, max_rows):
    row = lambda a: a[i].astype(F32).reshape(1, -1)
    pad_ff = D_FF_PAD - D_FF
    p = {}
    for name in ("ffn1", "ffn2"):
        p[name + "_norm"] = row(w[name + "_norm"])
        p[name + "_w_gate"] = jnp.pad(w[name + "_w_gate"][i], ((0, 0), (0, pad_ff))).astype(BF16)
        p[name + "_w_up"] = jnp.pad(w[name + "_w_up"][i], ((0, 0), (0, pad_ff))).astype(BF16)
        p[name + "_w_down"] = jnp.pad(w[name + "_w_down"][i], ((0, pad_ff), (0, 0))).astype(BF16)
    p["mix_norm"] = row(w["mix_norm"])

    wi = w["w_in"][i]
    seg = lambda start, width: wi[:, start:start + width]
    small = jnp.concatenate([seg(1536, 8), seg(3080, 16), seg(5656, 4), seg(5660, 4),
                             jnp.zeros((D_MODEL, LANES - 32), wi.dtype)], axis=1)
    p["w_in"] = jnp.concatenate([
        seg(512, 1024), seg(0, 512), seg(3608, 1536), seg(1544, 512), seg(2568, 512), seg(3096, 512),
        seg(5144, 512), seg(2056, 256), seg(2312, 256), small], axis=1).astype(BF16)
    p["w_out"] = w["w_out"][i].astype(BF16)

    p["ssd_conv_w"] = w["ssd_conv_w"][i].astype(F32)
    p["ssd_conv_b"] = row(w["ssd_conv_b"])
    p["ssd_dt_bias"] = _lane_row(w["ssd_dt_bias"][i], SM_DT)
    p["ssd_a_log"] = _lane_row(w["ssd_a_log"][i], SM_DT)
    p["ssd_d"] = jnp.repeat(w["ssd_d"][i].astype(F32), SSD_HEAD_DIM).reshape(1, D_GROUP)
    p["ssd_norm"] = row(w["ssd_norm"])

    a_re, a_im = w["s5_a_re"][i].astype(F32), w["s5_a_im"][i].astype(F32)
    dt = jnp.exp(w["s5_log_dt"][i].astype(F32))[:, None]
    lam_re, lam_im = a_re * dt, a_im * dt

    def a_bar_pow(n):
        mag = jnp.exp(lam_re * n)
        return mag * jnp.cos(lam_im * n), mag * jnp.sin(lam_im * n)

    abar_re, abar_im = a_bar_pow(1.0)
    den = a_re * a_re + a_im * a_im
    coef_re = ((abar_re - 1.0) * a_re + abar_im * a_im) / den
    coef_im = (abar_im * a_re - (abar_re - 1.0) * a_im) / den
    b_re, b_im = w["s5_b_re"][i].astype(F32), w["s5_b_im"][i].astype(F32)
    bbar_re = coef_re[..., None] * b_re - coef_im[..., None] * b_im
    bbar_im = coef_re[..., None] * b_im + coef_im[..., None] * b_re
    steps = jnp.arange(1, max_rows + 1, dtype=F32)[:, None, None]
    apow_re, apow_im = a_bar_pow(steps)
    p["s5_apow_re"] = apow_re.reshape(max_rows, S5_LANES)
    p["s5_apow_im"] = apow_im.reshape(max_rows, S5_LANES)
    lag_re, lag_im = a_bar_pow(jnp.arange(S5_TAPS, dtype=F32)[:, None, None])
    wlag_re = lag_re[..., None] * bbar_re - lag_im[..., None] * bbar_im
    wlag_im = lag_re[..., None] * bbar_im + lag_im[..., None] * bbar_re
    n_blk = D_GROUP // LANES
    gpb = S5_GROUPS // n_blk
    eye = jnp.eye(gpb, dtype=F32)

    def lag_blocks(x):
        x = jnp.swapaxes(x, 2, 3).reshape(S5_TAPS, n_blk, gpb, S5_CH, S5_STATE)
        return jnp.einsum("tjgip,gh->jtgihp", x, eye).reshape(n_blk, S5_TAPS * LANES, gpb * S5_STATE).astype(BF16)

    def c_blocks(x):
        x = x.reshape(n_blk, gpb, S5_CH, S5_STATE)
        return jnp.einsum("jgip,gh->jgphi", x, eye).reshape(n_blk, gpb * S5_STATE, LANES).astype(BF16)

    p["s5_wlag_re"], p["s5_wlag_im"] = lag_blocks(wlag_re), lag_blocks(wlag_im)
    p["s5_cblk_re"] = c_blocks(w["s5_c_re"][i].astype(F32))
    p["s5_cblk_im"] = c_blocks(w["s5_c_im"][i].astype(F32))
    p["s5_d"] = row(w["s5_d"])
    p["s5_w_glu"] = w["s5_w_glu"][i].astype(BF16)
    p["s5_b_glu"] = row(w["s5_b_glu"])

    hk = GLA_HEADS * GLA_DK
    p["gla_w_gate2"] = jnp.zeros((LANES, hk), F32).at[SM_GR:SM_GR + GLA_GATE_RANK].set(w["gla_w_gate2"][i].astype(F32))
    p["gla_b_gate2"] = row(w["gla_b_gate2"])
    p["gla_norm"] = row(w["gla_norm"])

    p["gdn_conv_w"] = w["gdn_conv_w"][i].astype(F32)
    p["gdn_a_log"] = _lane_row(w["gdn_a_log"][i], SM_DECAY)
    p["gdn_dt_bias"] = _lane_row(w["gdn_dt_bias"][i], SM_DECAY)
    p["gdn_norm"] = row(w["gdn_norm"])
    return p


def _mixer_geometry(bsz, length):
    if length >= CHUNK:
        assert length % CHUNK == 0
        rows = 256 if length % 256 == 0 else CHUNK
        return 1, rows, CHUNK, length
    assert length <= SUBLANES
    bb = SUBLANES if bsz % SUBLANES == 0 else 1
    return bb, SUBLANES, SUBLANES, SUBLANES


def _conv_buffer(state):
    return jnp.pad(state.astype(F32), ((0, 0), (SUBLANES - (CONV_W - 1), 0), (0, 0)))


def _trunk(x, states, layers, final_norm):
    bsz, length, _ = x.shape
    bb, rows, chunk, lpad = _mixer_geometry(bsz, length)
    geo = dict(bb=bb, rows=rows, chunk=chunk, valid=min(length, rows))
    h = x.reshape(bsz * length, D_MODEL).astype(F32)
    fg = final_norm.astype(F32).reshape(1, D_MODEL)
    new_states = []
    for i, p in enumerate(layers):
        ssd_conv0, ssd_h0, s5_re0, s5_im0, gla_s0, gdn_conv0, gdn_s0 = states[i]
        h = _ffn(h, p["ffn1_norm"], p["ffn1_w_gate"], p["ffn1_w_up"], p["ffn1_w_down"], fg, False)
        proj = _in_proj(h, p["mix_norm"], p["w_in"]).reshape(bsz, length, PROJ_COLS)
        ssd_conv1 = proj[:, length - (CONV_W - 1):, COL_XBC:COL_XBC + 1024]
        gdn_conv1 = proj[:, length - (CONV_W - 1):, COL_GDN_QKV:COL_GDN_QKV + 1536]
        if lpad != length:
            proj = jnp.pad(proj, ((0, 0), (0, lpad - length), (0, 0)))
        y_ssd, ssd_h1 = _ssd(proj, _conv_buffer(ssd_conv0), ssd_h0.astype(F32).reshape(bsz, D_GROUP, SSD_STATE), p, **geo)
        y_s5, s5_re1, s5_im1 = _s5(proj, s5_re0.astype(F32), s5_im0.astype(F32), p, bb=bb, rows=rows, valid=geo["valid"])
        y_gla, gla_s1 = _gla(proj, gla_s0.astype(F32).reshape(bsz, GLA_HEADS * GLA_DK, GLA_DV), p, **geo)
        y_gdn, gdn_s1 = _gdn(proj, _conv_buffer(gdn_conv0), gdn_s0.astype(F32).reshape(bsz, GDN_HEADS * GDN_DK, GDN_DK), p, **geo)
        ys = [y[:, :length].reshape(bsz * length, D_GROUP) for y in (y_ssd, y_s5, y_gla, y_gdn)]
        h = _out_proj(h, ys, p["w_out"])
        h = _ffn(h, p["ffn2_norm"], p["ffn2_w_gate"], p["ffn2_w_up"], p["ffn2_w_down"], fg, i == len(layers) - 1)
        new_states.append((ssd_conv1, ssd_h1.reshape(bsz, SSD_HEADS, SSD_HEAD_DIM, SSD_STATE), s5_re1, s5_im1,
                           gla_s1.reshape(bsz, GLA_HEADS, GLA_DK, GLA_DV), gdn_conv1,
                           gdn_s1.reshape(bsz, GDN_HEADS, GDN_DK, GDN_DK)))
    stacked = tuple(jnp.stack([s[j] for s in new_states]) for j in range(7))
    return h.reshape(bsz, length, D_MODEL), stacked


def kernel(x_prompt, x_sample, state_ssd_conv, state_ssd, state_s5_re, state_s5_im, state_gla, state_gdn_conv, state_gdn, ffn1_norm, ffn1_w_gate, ffn1_w_up, ffn1_w_down, mix_norm, w_in, ssd_conv_w, ssd_conv_b, ssd_dt_bias, ssd_a_log, ssd_d, ssd_norm, s5_a_re, s5_a_im, s5_log_dt, s5_b_re, s5_b_im, s5_c_re, s5_c_im, s5_d, s5_w_glu, s5_b_glu, gla_w_gate2, gla_b_gate2, gla_norm, gdn_conv_w, gdn_a_log, gdn_dt_bias, gdn_norm, w_out, ffn2_norm, ffn2_w_gate, ffn2_w_up, ffn2_w_down, final_norm):
    w = dict(ffn1_norm=ffn1_norm, ffn1_w_gate=ffn1_w_gate, ffn1_w_up=ffn1_w_up, ffn1_w_down=ffn1_w_down,
             mix_norm=mix_norm, w_in=w_in, ssd_conv_w=ssd_conv_w, ssd_conv_b=ssd_conv_b, ssd_dt_bias=ssd_dt_bias,
             ssd_a_log=ssd_a_log, ssd_d=ssd_d, ssd_norm=ssd_norm, s5_a_re=s5_a_re, s5_a_im=s5_a_im,
             s5_log_dt=s5_log_dt, s5_b_re=s5_b_re, s5_b_im=s5_b_im, s5_c_re=s5_c_re, s5_c_im=s5_c_im, s5_d=s5_d,
             s5_w_glu=s5_w_glu, s5_b_glu=s5_b_glu, gla_w_gate2=gla_w_gate2, gla_b_gate2=gla_b_gate2,
             gla_norm=gla_norm, gdn_conv_w=gdn_conv_w, gdn_a_log=gdn_a_log, gdn_dt_bias=gdn_dt_bias,
             gdn_norm=gdn_norm, w_out=w_out, ffn2_norm=ffn2_norm, ffn2_w_gate=ffn2_w_gate, ffn2_w_up=ffn2_w_up,
             ffn2_w_down=ffn2_w_down)
    depth = w_in.shape[0]
    max_rows = max(_mixer_geometry(*x_prompt.shape[:2])[1], _mixer_geometry(*x_sample.shape[:2])[1])
    layers = [_prep_layer(w, i, max_rows) for i in range(depth)]

    sample_states = (state_ssd_conv, state_ssd, state_s5_re, state_s5_im, state_gla, state_gdn_conv, state_gdn)
    n_prompt = x_prompt.shape[0]
    prompt_layer = tuple(jnp.zeros((n_prompt,) + s.shape[2:], F32) for s in sample_states)
    y_prompt, p_states = _trunk(x_prompt, [prompt_layer] * depth, layers, final_norm)
    y_sample, s_states = _trunk(x_sample, [tuple(s[i] for s in sample_states) for i in range(depth)], layers, final_norm)
    return (y_prompt, y_sample) + p_states + s_states
```

```python
import functools
import math

import numpy as np
import jax
import jax.numpy as jnp
from jax import lax
from jax.experimental import pallas as pl
from jax.experimental.pallas import tpu as pltpu

F32 = jnp.float32
BF16 = jnp.bfloat16
HIGHEST = lax.Precision.HIGHEST
EPS = 1e-6

LANES = 128
SUBLANES = 8
VMEM_LIMIT_BYTES = 56 * 1024 * 1024

D_MODEL = 2048
D_GROUP = 512
D_FF = 5504
D_FF_PAD = 5632
CONV_W = 4
CHUNK = 64

SSD_HEADS = 8
SSD_HEAD_DIM = 64
SSD_GROUPS = 2
SSD_STATE = 128
S5_GROUPS = 32
S5_CH = 16
S5_STATE = 64
S5_LANES = S5_GROUPS * S5_STATE
S5_TAPS = 8
GLA_HEADS = 4
GLA_DK = 64
GLA_DV = 128
GLA_GATE_RANK = 16
GLA_GATE_TEMP = 16.0
GDN_HEADS = 4
GDN_DK = 128

COL_XBC = 0
COL_SSD_Z = 1024
COL_GDN_QKV = 1536
COL_S5_U = 3072
COL_GLA_V = 3584
COL_GLA_R = 4096
COL_GDN_Z = 4608
COL_GLA_Q = 5120
COL_GLA_K = 5376
COL_SMALL = 5632
PROJ_COLS = 5760
SM_DT = 0
SM_GR = 8
SM_BETA = 24
SM_DECAY = 28


def _mm(a, b):
    return jnp.dot(a.astype(BF16), b.astype(BF16), preferred_element_type=F32)


def _mm_nt(a, b):
    return lax.dot_general(a.astype(BF16), b.astype(BF16), (((1,), (1,)), ((), ())), preferred_element_type=F32)


def _mm_tn(a, b):
    return lax.dot_general(a.astype(BF16), b.astype(BF16), (((0,), (0,)), ((), ())), preferred_element_type=F32)


def _mm_hi(a, b):
    return jnp.dot(a, b, precision=HIGHEST, preferred_element_type=F32)


def _mm_nt_hi(a, b):
    return lax.dot_general(a, b, (((1,), (1,)), ((), ())), precision=HIGHEST, preferred_element_type=F32)


def _split(a):
    hi = a.astype(BF16)
    return hi, (a - hi.astype(F32)).astype(BF16)


def _mm3(a, b):
    (ah, al), (bh, bl) = a, b
    dot = functools.partial(jnp.dot, preferred_element_type=F32)
    return dot(ah, bh) + (dot(al, bh) + dot(ah, bl))


def _head_rows(first_lane):
    return (_iota((SUBLANES, LANES), 1) == _iota((SUBLANES, LANES), 0) + first_lane).astype(F32)


def _sigmoid(x):
    return 1.0 / (1.0 + jnp.exp(-x))


def _silu(x):
    return x * _sigmoid(x)


def _softplus(x):
    return jnp.maximum(x, 0.0) + jnp.log1p(jnp.exp(-jnp.abs(x)))


def _iota(shape, dim):
    return lax.broadcasted_iota(jnp.int32, shape, dim)


def _tril(n, strict=False):
    r, c = _iota((n, n), 0), _iota((n, n), 1)
    return (r > c) if strict else (r >= c)


def _eye(n):
    return (_iota((n, n), 0) == _iota((n, n), 1)).astype(F32)


def _row_to_col(row):
    n = row.shape[1]
    return jnp.sum(_eye(n) * row, axis=1, keepdims=True)


def _group_rmsnorm(y, width):
    parts = []
    for g in range(y.shape[1] // width):
        yg = y[:, g * width:(g + 1) * width]
        parts.append(yg * lax.rsqrt(jnp.mean(yg * yg, axis=1, keepdims=True) + EPS))
    return jnp.concatenate(parts, axis=1)


def _causal_conv(x, cbuf, w, rows):
    cbuf[SUBLANES:SUBLANES + rows, :] = x
    y = (w[0:1, :] * cbuf[5:5 + rows, :] + w[1:2, :] * cbuf[6:6 + rows, :]
         + w[2:3, :] * cbuf[7:7 + rows, :] + w[3:4, :] * x)
    cbuf[0:SUBLANES, :] = cbuf[rows:rows + SUBLANES, :]
    return y


def _params(*sem):
    return pltpu.CompilerParams(dimension_semantics=sem, vmem_limit_bytes=VMEM_LIMIT_BYTES)


class _Layered:
    def __init__(self, init, src, dst, depth, prev):
        self.init, self.src, self.dst, self.depth, self.prev = init, src, dst, depth, prev

    def in_spec(self, bb):
        tail = self.init.shape[2:]
        return pl.BlockSpec((None, bb) + tail, lambda bi, i, s=self.src, z=(0,) * len(tail): (s, bi) + z)

    def out_spec(self, bb):
        tail = self.init.shape[2:]
        return pl.BlockSpec((None, bb) + tail, lambda bi, i, d=self.dst, z=(0,) * len(tail): (d, bi) + z)

    def out_shape(self):
        return jax.ShapeDtypeStruct((self.depth,) + self.init.shape[1:], F32)


def _mixer_call(body, *, grid, in_specs, inputs, out_specs, out_shape, states, first_state_out, scratch, name):
    n_in = len(inputs)
    prevs = [s.prev for s in states if s.prev is not None]
    aliases = {}
    fn = body
    if prevs:
        assert len(prevs) == len(states)
        in_specs = list(in_specs) + [pl.BlockSpec(memory_space=pl.ANY)] * len(prevs)
        aliases = {n_in + j: first_state_out + j for j in range(len(prevs))}

        def fn(*refs):
            return body(*refs[:n_in], *refs[n_in + len(prevs):])

    return pl.pallas_call(
        fn, grid=grid, in_specs=in_specs, out_specs=out_specs, out_shape=out_shape, scratch_shapes=scratch,
        input_output_aliases=aliases, compiler_params=_params("arbitrary", "arbitrary"), name=name,
    )(*inputs, *prevs)


def _ffn_kernel(x_ref, g_ref, wg_ref, wu_ref, wd_ref, fg_ref, o_ref, xn_ref, *, n_f, final_norm):
    f = pl.program_id(1)

    @pl.when(f == 0)
    def _():
        x = x_ref[...]
        xn = x * lax.rsqrt(jnp.mean(x * x, axis=1, keepdims=True) + EPS) * g_ref[...]
        xn_ref[...] = xn.astype(BF16)
        o_ref[...] = jnp.zeros_like(o_ref)

    xn = xn_ref[...]
    gate = jnp.dot(xn, wg_ref[...], preferred_element_type=F32)
    up = jnp.dot(xn, wu_ref[...], preferred_element_type=F32)
    h = (_silu(gate) * up).astype(BF16)
    o_ref[...] += jnp.dot(h, wd_ref[...], preferred_element_type=F32)

    @pl.when(f == n_f - 1)
    def _():
        y = x_ref[...] + 0.5 * o_ref[...]
        if final_norm:
            y = y * lax.rsqrt(jnp.mean(y * y, axis=1, keepdims=True) + EPS) * fg_ref[...]
        o_ref[...] = y


def _ffn(x, g, wg, wu, wd, fg, final_norm):
    m = x.shape[0]
    tm = min(1024, m)
    tf = 256
    n_f = D_FF_PAD // tf
    return pl.pallas_call(
        functools.partial(_ffn_kernel, n_f=n_f, final_norm=final_norm),
        grid=(m // tm, n_f),
        in_specs=[
            pl.BlockSpec((tm, D_MODEL), lambda i, f: (i, 0)),
            pl.BlockSpec((1, D_MODEL), lambda i, f: (0, 0)),
            pl.BlockSpec((D_MODEL, tf), lambda i, f: (0, f)),
            pl.BlockSpec((D_MODEL, tf), lambda i, f: (0, f)),
            pl.BlockSpec((tf, D_MODEL), lambda i, f: (f, 0)),
            pl.BlockSpec((1, D_MODEL), lambda i, f: (0, 0)),
        ],
        out_specs=pl.BlockSpec((tm, D_MODEL), lambda i, f: (i, 0)),
        out_shape=jax.ShapeDtypeStruct((m, D_MODEL), F32),
        scratch_shapes=[pltpu.VMEM((tm, D_MODEL), BF16)],
        compiler_params=_params("arbitrary", "arbitrary"),
        name="ffn",
    )(x, g, wg, wu, wd, fg)


def _in_proj_kernel(x_ref, g_ref, w_ref, o_ref, xn_ref):
    @pl.when(pl.program_id(1) == 0)
    def _():
        x = x_ref[...]
        xn = x * lax.rsqrt(jnp.mean(x * x, axis=1, keepdims=True) + EPS) * g_ref[...]
        xn_ref[...] = xn.astype(BF16)

    o_ref[...] = jnp.dot(xn_ref[...], w_ref[...], preferred_element_type=F32)


def _in_proj(x, g, w):
    m = x.shape[0]
    tm = min(1024, m)
    tn = 1920
    return pl.pallas_call(
        _in_proj_kernel,
        grid=(m // tm, PROJ_COLS // tn),
        in_specs=[
            pl.BlockSpec((tm, D_MODEL), lambda i, n: (i, 0)),
            pl.BlockSpec((1, D_MODEL), lambda i, n: (0, 0)),
            pl.BlockSpec((D_MODEL, tn), lambda i, n: (0, n)),
        ],
        out_specs=pl.BlockSpec((tm, tn), lambda i, n: (i, n)),
        out_shape=jax.ShapeDtypeStruct((m, PROJ_COLS), F32),
        scratch_shapes=[pltpu.VMEM((tm, D_MODEL), BF16)],
        compiler_params=_params("arbitrary", "arbitrary"),
        name="in_proj",
    )(x, g, w)


def _out_proj_kernel(x_ref, y0_ref, y1_ref, y2_ref, y3_ref, w_ref, o_ref):
    acc = x_ref[...]
    for j, y_ref in enumerate((y0_ref, y1_ref, y2_ref, y3_ref)):
        acc = acc + jnp.dot(y_ref[...].astype(BF16), w_ref[j * D_GROUP:(j + 1) * D_GROUP, :],
                            preferred_element_type=F32)
    o_ref[...] = acc


def _out_proj(x, ys, w):
    m = x.shape[0]
    tm = min(512, m)
    yspec = pl.BlockSpec((tm, D_GROUP), lambda i: (i, 0))
    return pl.pallas_call(
        _out_proj_kernel,
        grid=(m // tm,),
        in_specs=[pl.BlockSpec((tm, D_MODEL), lambda i: (i, 0)), yspec, yspec, yspec, yspec,
                  pl.BlockSpec((D_MODEL, D_MODEL), lambda i: (0, 0))],
        out_specs=pl.BlockSpec((tm, D_MODEL), lambda i: (i, 0)),
        out_shape=jax.ShapeDtypeStruct((m, D_MODEL), F32),
        compiler_params=_params("arbitrary"),
        name="out_proj",
    )(x, *ys, w)


def _cast_pad_cols_kernel(w_ref, o_ref):
    cols = w_ref.shape[1]
    o_ref[:, 0:cols] = w_ref[...].astype(BF16)
    o_ref[:, cols:] = jnp.zeros((o_ref.shape[0], o_ref.shape[1] - cols), BF16)


def _cast_pad_rows_kernel(w_ref, o_ref, *, valid_blocks):
    o_ref[...] = jnp.where(pl.program_id(0) < valid_blocks, w_ref[...], 0.0).astype(BF16)


def _cast_pad(w, layer, rows_to, cols_to):
    _, rows, cols = w.shape
    if cols_to != cols:
        assert rows_to == rows and cols % LANES == 0
        tr = 256
        return pl.pallas_call(
            _cast_pad_cols_kernel,
            grid=(rows // tr,),
            in_specs=[pl.BlockSpec((None, tr, cols), lambda r: (layer, r, 0))],
            out_specs=pl.BlockSpec((tr, cols_to), lambda r: (r, 0)),
            out_shape=jax.ShapeDtypeStruct((rows, cols_to), BF16),
            compiler_params=_params("arbitrary"),
            name="cast_pad_cols",
        )(w)
    tr = LANES
    assert cols_to == cols and rows_to % tr == 0 and rows % tr == 0
    valid_blocks = rows // tr
    return pl.pallas_call(
        functools.partial(_cast_pad_rows_kernel, valid_blocks=valid_blocks),
        grid=(rows_to // tr,),
        in_specs=[pl.BlockSpec((None, tr, cols), lambda r: (layer, jnp.minimum(r, valid_blocks - 1), 0))],
        out_specs=pl.BlockSpec((tr, cols), lambda r: (r, 0)),
        out_shape=jax.ShapeDtypeStruct((rows_to, cols), BF16),
        compiler_params=_params("arbitrary"),
        name="cast_pad_rows",
    )(w)


def _ssd_kernel(xbc_ref, z_ref, sm_ref, conv0_ref, h0_ref, cw_ref, cb_ref, dtb_ref, alog_ref, dex_ref, nrm_ref,
                y_ref, h_ref, cbuf, abuf, *, bb, rows, chunk, valid):
    @pl.when(pl.program_id(1) == 0)
    def _():
        cbuf[:, 0:SUBLANES, :] = conv0_ref[...]
        h_ref[...] = h0_ref[...]

    a_neg = -jnp.exp(alog_ref[...])
    tril = _tril(chunk).astype(F32)
    causal = _tril(chunk)
    left = _iota((1, LANES), 1) < SSD_HEAD_DIM
    top = _iota((LANES, 1), 0) < SSD_HEAD_DIM
    head_rows = _head_rows(SM_DT)

    def pair(v, h0):
        return jnp.where(left, v[:, h0:h0 + 1], v[:, h0 + 1:h0 + 2])

    dt_all = []
    for b in range(bb):
        conv = _causal_conv(xbc_ref[b], cbuf.at[b], cw_ref[...], rows) + cb_ref[...]
        abuf[b] = _silu(conv)
        dt = _softplus(sm_ref[b] + dtb_ref[...])
        if valid < rows:
            dt = jnp.where(_iota((rows, LANES), 0) < valid, dt, 0.0)
        dt_all.append(dt)

    n_c = rows // chunk
    seqs = [(b, c) for b in range(bb) for c in range(n_c)]
    dt = {(b, c): dt_all[b][c * chunk:(c + 1) * chunk, :] for (b, c) in seqs}
    acs = {s: _mm_hi(tril, dt[s] * a_neg) for s in seqs}
    acs_t = {s: _mm_nt_hi(head_rows, acs[s]) for s in seqs}
    groups = [(b, c, g) for (b, c) in seqs for g in range(SSD_GROUPS)]
    bm, cm = {}, {}
    for (b, c, g) in groups:
        r = slice(c * chunk, (c + 1) * chunk)
        bm[b, c, g] = abuf[b, r, D_GROUP + g * SSD_STATE:D_GROUP + (g + 1) * SSD_STATE]
        cm[b, c, g] = abuf[b, r, D_GROUP + (SSD_GROUPS + g) * SSD_STATE:D_GROUP + (SSD_GROUPS + g + 1) * SSD_STATE]
    cb = {x: _mm_nt(cm[x], bm[x]) for x in groups}
    pairs = [(b, c, g, j) for (b, c, g) in groups for j in range(2)]
    xdt, y_diag = {}, {}
    for (b, c, g, j) in pairs:
        h0 = 4 * g + 2 * j
        lo = (2 * g + j) * LANES
        xdt[b, c, g, j] = abuf[b, c * chunk:(c + 1) * chunk, lo:lo + LANES] * pair(dt[b, c], h0)
    for (b, c, g, j) in pairs:
        h0 = 4 * g + 2 * j
        total = None
        for hh, keep in ((h0, left), (h0 + 1, jnp.logical_not(left))):
            diff = acs[b, c][:, hh:hh + 1] - acs_t[b, c][hh:hh + 1, :]
            decay = jnp.exp(jnp.where(causal, diff, -jnp.inf))
            part = _mm(cb[b, c, g] * decay, jnp.where(keep, xdt[b, c, g, j], 0.0))
            total = part if total is None else total + part
        y_diag[b, c, g, j] = total

    for c in range(n_c):
        r0 = c * chunk
        for b in range(bb):
            a_c = acs[b, c]
            exp_acs = jnp.exp(a_c)
            to_end = jnp.exp(a_c[chunk - 1:chunk, :] - a_c)
            end_decay = jnp.exp(a_c[chunk - 1:chunk, :])
            for g in range(SSD_GROUPS):
                for j in range(2):
                    h0 = 4 * g + 2 * j
                    lo = (2 * g + j) * LANES
                    st = h_ref[b, lo:lo + LANES, :]
                    y_ref[b, r0:r0 + chunk, lo:lo + LANES] = (
                        y_diag[b, c, g, j] + _mm_nt(cm[b, c, g], st) * pair(exp_acs, h0))
                    dec = jnp.where(top, end_decay[:, h0:h0 + 1], end_decay[:, h0 + 1:h0 + 2])
                    h_ref[b, lo:lo + LANES, :] = st * dec + _mm_tn(xdt[b, c, g, j] * pair(to_end, h0), bm[b, c, g])

    for b in range(bb):
        y = y_ref[b] + dex_ref[...] * abuf[b, :, 0:D_GROUP]
        y = y * _silu(z_ref[b])
        y_ref[b] = _group_rmsnorm(y, D_GROUP // SSD_GROUPS) * nrm_ref[...]


def _ssd(proj, conv0, state, p, *, bb, rows, chunk, valid):
    b, length, _ = proj.shape
    w_xbc = D_GROUP + 2 * SSD_GROUPS * SSD_STATE
    vec = lambda n: pl.BlockSpec((1, n), lambda bi, i: (0, 0))
    return _mixer_call(
        functools.partial(_ssd_kernel, bb=bb, rows=rows, chunk=chunk, valid=valid),
        grid=(b // bb, length // rows),
        in_specs=[
            pl.BlockSpec((bb, rows, w_xbc), lambda bi, i: (bi, i, COL_XBC // w_xbc)),
            pl.BlockSpec((bb, rows, D_GROUP), lambda bi, i: (bi, i, COL_SSD_Z // D_GROUP)),
            pl.BlockSpec((bb, rows, LANES), lambda bi, i: (bi, i, COL_SMALL // LANES)),
            pl.BlockSpec((bb, SUBLANES, w_xbc), lambda bi, i: (bi, 0, 0)),
            state.in_spec(bb),
            pl.BlockSpec((CONV_W, w_xbc), lambda bi, i: (0, 0)),
            vec(w_xbc), vec(LANES), vec(LANES), vec(D_GROUP), vec(D_GROUP),
        ],
        inputs=[proj, proj, proj, conv0, state.init, p["ssd_conv_w"], p["ssd_conv_b"], p["ssd_dt_bias"],
                p["ssd_a_log"], p["ssd_d"], p["ssd_norm"]],
        out_specs=[pl.BlockSpec((bb, rows, D_GROUP), lambda bi, i: (bi, i, 0)), state.out_spec(bb)],
        out_shape=[jax.ShapeDtypeStruct((b, length, D_GROUP), F32), state.out_shape()],
        states=[state], first_state_out=1,
        scratch=[pltpu.VMEM((bb, rows + SUBLANES, w_xbc), F32), pltpu.VMEM((bb, rows, w_xbc), F32)],
        name="ssd")


def _s5_kernel(u_ref, h0re_ref, h0im_ref, apre_ref, apim_ref, wre_ref, wim_ref, cre_ref, cim_ref, d_ref, wglu_ref,
               bglu_ref, y_ref, hre_ref, him_ref, ubuf, *scan, bb, rows, pad, valid):
    @pl.when(pl.program_id(1) == 0)
    def _():
        hre_ref[...] = h0re_ref[...]
        him_ref[...] = h0im_ref[...]

    m = bb * rows
    n_blk = D_GROUP // LANES
    w_blk = S5_LANES // n_blk
    ubuf[:, 0:SUBLANES, :] = jnp.zeros((bb, SUBLANES, D_GROUP), F32)
    ubuf[:, SUBLANES:SUBLANES + rows, :] = u_ref[...]
    for buf in scan:
        buf[0:pad, :] = jnp.zeros((pad, w_blk), F32)

    parts = []
    for j in range(n_blk):
        sl = slice(j * w_blk, (j + 1) * w_blk)
        taps = [ubuf[:, SUBLANES - t:SUBLANES - t + rows, j * LANES:(j + 1) * LANES].reshape(m, LANES)
                for t in range(S5_TAPS)]
        lagged = jnp.concatenate(taps, axis=1).astype(BF16)
        xr = jnp.dot(lagged, wre_ref[j], preferred_element_type=F32)
        xi = jnp.dot(lagged, wim_ref[j], preferred_element_type=F32)
        if rows > S5_TAPS:
            src, dst = scan[0:2], scan[2:4]
            src[0][pad:pad + rows, :] = xr
            src[1][pad:pad + rows, :] = xi
            shift = S5_TAPS
            while shift < rows:
                ar = apre_ref[shift - 1:shift, sl]
                ai = apim_ref[shift - 1:shift, sl]
                cr, ci = src[0][pad:pad + rows, :], src[1][pad:pad + rows, :]
                sr, si = src[0][pad - shift:pad - shift + rows, :], src[1][pad - shift:pad - shift + rows, :]
                dst[0][pad:pad + rows, :] = cr + ar * sr - ai * si
                dst[1][pad:pad + rows, :] = ci + ar * si + ai * sr
                src, dst = dst, src
                shift *= 2
            xr, xi = src[0][pad:pad + rows, :], src[1][pad:pad + rows, :]
        pr, pi = hre_ref[:, :, sl], him_ref[:, :, sl]
        apr, api = apre_ref[:, sl][None], apim_ref[:, sl][None]
        hr = xr.reshape(bb, rows, w_blk) + apr * pr - api * pi
        hi = xi.reshape(bb, rows, w_blk) + apr * pi + api * pr
        hre_ref[:, :, sl] = hr[:, valid - 1:valid, :]
        him_ref[:, :, sl] = hi[:, valid - 1:valid, :]
        parts.append(_mm(hr.reshape(m, w_blk), cre_ref[j]) - _mm(hi.reshape(m, w_blk), cim_ref[j]))

    y = jnp.concatenate(parts, axis=1) + d_ref[...] * u_ref[...].reshape(m, D_GROUP)
    y = 0.5 * y * (1.0 + jnp.tanh(math.sqrt(2.0 / math.pi) * (y + 0.044715 * (y * y * y))))
    y = y * _sigmoid(_mm(y, wglu_ref[...]) + bglu_ref[...])
    y_ref[...] = y.reshape(bb, rows, D_GROUP)


def _s5(proj, state_re, state_im, p, *, bb, rows, valid):
    b, length, _ = proj.shape
    assert rows == S5_TAPS or bb == 1
    pad = rows // 2
    n_blk = D_GROUP // LANES
    w_blk = S5_LANES // n_blk
    vec = lambda n: pl.BlockSpec((1, n), lambda bi, i: (0, 0))
    full3 = lambda s: pl.BlockSpec(s, lambda bi, i: (0, 0, 0))
    scan = [pltpu.VMEM((pad + rows, w_blk), F32)] * 4 if rows > S5_TAPS else []
    return _mixer_call(
        functools.partial(_s5_kernel, bb=bb, rows=rows, pad=pad, valid=valid),
        grid=(b // bb, length // rows),
        in_specs=[
            pl.BlockSpec((bb, rows, D_GROUP), lambda bi, i: (bi, i, COL_S5_U // D_GROUP)),
            state_re.in_spec(bb), state_im.in_spec(bb),
            pl.BlockSpec((rows, S5_LANES), lambda bi, i: (0, 0)),
            pl.BlockSpec((rows, S5_LANES), lambda bi, i: (0, 0)),
            full3((n_blk, S5_TAPS * LANES, w_blk)), full3((n_blk, S5_TAPS * LANES, w_blk)),
            full3((n_blk, w_blk, LANES)), full3((n_blk, w_blk, LANES)),
            vec(D_GROUP), pl.BlockSpec((D_GROUP, D_GROUP), lambda bi, i: (0, 0)), vec(D_GROUP),
        ],
        inputs=[proj, state_re.init, state_im.init, p["s5_apow_re"][:rows], p["s5_apow_im"][:rows], p["s5_wlag_re"],
                p["s5_wlag_im"], p["s5_cblk_re"], p["s5_cblk_im"], p["s5_d"], p["s5_w_glu"], p["s5_b_glu"]],
        out_specs=[pl.BlockSpec((bb, rows, D_GROUP), lambda bi, i: (bi, i, 0)),
                   state_re.out_spec(bb), state_im.out_spec(bb)],
        out_shape=[jax.ShapeDtypeStruct((b, length, D_GROUP), F32), state_re.out_shape(), state_im.out_shape()],
        states=[state_re, state_im], first_state_out=1,
        scratch=[pltpu.VMEM((bb, SUBLANES + rows, D_GROUP), F32)] + scan,
        name="s5")


def _gla_kernel(q_ref, k_ref, v_ref, r_ref, sm_ref, s0_ref, wg_ref, bg_ref, nrm_ref, o_ref, s_ref,
                *, bb, rows, chunk, valid):
    @pl.when(pl.program_id(1) == 0)
    def _():
        s_ref[...] = s0_ref[...]

    hk = GLA_HEADS * GLA_DK
    tril = _tril(chunk).astype(F32)
    causal = _tril(chunk)
    lane = _iota((1, hk), 1)
    lg_all, k_all = [], []
    for b in range(bb):
        lg = -_softplus(-(_mm(sm_ref[b], wg_ref[...]) + bg_ref[...])) * (1.0 / GLA_GATE_TEMP)
        k = k_ref[b]
        if valid < rows:
            live = _iota((rows, hk), 0) < valid
            lg = jnp.where(live, lg, 0.0)
            k = jnp.where(live, k, 0.0)
        lg_all.append(lg)
        k_all.append(k)

    n_c = rows // chunk
    seqs = [(b, c) for b in range(bb) for c in range(n_c)]
    cum = {(b, c): _mm_hi(tril, lg_all[b][c * chunk:(c + 1) * chunk, :]) for (b, c) in seqs}
    q_dec, k_dec, k_end, end_decay = {}, {}, {}, {}
    for (b, c) in seqs:
        r = slice(c * chunk, (c + 1) * chunk)
        cs = cum[b, c]
        cum_last = cs[chunk - 1:chunk, :]
        q_dec[b, c] = q_ref[b, r, :] * (GLA_DK ** -0.5) * jnp.exp(cs)
        k = k_all[b][r, :]
        k_dec[b, c] = k * jnp.exp(-cs)
        k_end[b, c] = k * jnp.exp(cum_last - cs)
        end_decay[b, c] = _row_to_col(jnp.exp(cum_last))
    units = [(b, c, h) for (b, c) in seqs for h in range(GLA_HEADS)]
    qh, vh = {}, {}
    for (b, c, h) in units:
        mine = (lane >= h * GLA_DK) & (lane < (h + 1) * GLA_DK)
        qh[b, c, h] = jnp.where(mine, q_dec[b, c], 0.0)
        vh[b, c, h] = v_ref[b, c * chunk:(c + 1) * chunk, h * GLA_DV:(h + 1) * GLA_DV]
    att = {x: jnp.where(causal, _mm_nt(qh[x], k_dec[x[:2]]), 0.0) for x in units}
    o_intra = {x: _mm(att[x], vh[x]) for x in units}
    kv = {x: _mm_tn(k_end[x[:2]], vh[x])[x[2] * GLA_DK:(x[2] + 1) * GLA_DK, :] for x in units}

    for c in range(n_c):
        for b in range(bb):
            st = s_ref[b]
            for h in range(GLA_HEADS):
                rs = slice(h * GLA_DK, (h + 1) * GLA_DK)
                o_ref[b, c * chunk:(c + 1) * chunk, h * GLA_DV:(h + 1) * GLA_DV] = (
                    o_intra[b, c, h] + _mm(qh[b, c, h], st))
                s_ref[b, rs, :] = st[rs, :] * end_decay[b, c][rs, :] + kv[b, c, h]

    for b in range(bb):
        o = _group_rmsnorm(o_ref[b], GLA_DV) * nrm_ref[...]
        o_ref[b] = o * _silu(r_ref[b])


def _gla(proj, state, p, *, bb, rows, chunk, valid):
    b, length, _ = proj.shape
    hk = GLA_HEADS * GLA_DK
    vec = lambda n: pl.BlockSpec((1, n), lambda bi, i: (0, 0))
    return _mixer_call(
        functools.partial(_gla_kernel, bb=bb, rows=rows, chunk=chunk, valid=valid),
        grid=(b // bb, length // rows),
        in_specs=[
            pl.BlockSpec((bb, rows, hk), lambda bi, i: (bi, i, COL_GLA_Q // hk)),
            pl.BlockSpec((bb, rows, hk), lambda bi, i: (bi, i, COL_GLA_K // hk)),
            pl.BlockSpec((bb, rows, D_GROUP), lambda bi, i: (bi, i, COL_GLA_V // D_GROUP)),
            pl.BlockSpec((bb, rows, D_GROUP), lambda bi, i: (bi, i, COL_GLA_R // D_GROUP)),
            pl.BlockSpec((bb, rows, LANES), lambda bi, i: (bi, i, COL_SMALL // LANES)),
            state.in_spec(bb),
            pl.BlockSpec((LANES, hk), lambda bi, i: (0, 0)), vec(hk), vec(D_GROUP),
        ],
        inputs=[proj, proj, proj, proj, proj, state.init, p["gla_w_gate2"], p["gla_b_gate2"], p["gla_norm"]],
        out_specs=[pl.BlockSpec((bb, rows, D_GROUP), lambda bi, i: (bi, i, 0)), state.out_spec(bb)],
        out_shape=[jax.ShapeDtypeStruct((b, length, D_GROUP), F32), state.out_shape()],
        states=[state], first_state_out=1, scratch=[], name="gla")


def _gdn_kernel(qkv_ref, z_ref, sm_ref, conv0_ref, s0_ref, cw_ref, alog_ref, dtb_ref, nrm_ref, o_ref, s_ref,
                cbuf, abuf, *, bb, rows, chunk, valid):
    @pl.when(pl.program_id(1) == 0)
    def _():
        cbuf[:, 0:SUBLANES, :] = conv0_ref[...]
        s_ref[...] = s0_ref[...]

    a_neg = -jnp.exp(alog_ref[...])
    tril = _tril(chunk).astype(F32)
    causal = _tril(chunk)
    strict = _tril(chunk, strict=True)
    eye_c = _eye(chunk)
    gate_rows = _head_rows(SM_BETA)
    hd = GDN_HEADS * GDN_DK
    n_c = rows // chunk

    def l2n(x):
        return x * lax.rsqrt(jnp.sum(x * x, axis=1, keepdims=True) + EPS)

    beta_all, g_all = [], []
    for b in range(bb):
        abuf[b] = _silu(_causal_conv(qkv_ref[b], cbuf.at[b], cw_ref[...], rows))
        sm = sm_ref[b]
        beta = _sigmoid(sm)
        g = a_neg * _softplus(sm + dtb_ref[...])
        if valid < rows:
            live = _iota((rows, LANES), 0) < valid
            beta = jnp.where(live, beta, 0.0)
            g = jnp.where(live, g, 0.0)
        beta_all.append(beta)
        g_all.append(g)

    seqs = [(b, c) for b in range(bb) for c in range(n_c)]
    gc = {s: _mm_hi(tril, g_all[s[0]][s[1] * chunk:(s[1] + 1) * chunk, :]) for s in seqs}
    gc_t = {s: _mm_nt_hi(gate_rows, gc[s]) for s in seqs}

    units = [(b, c, h) for (b, c) in seqs for h in range(GDN_HEADS)]
    q, k, v, g_col, b_col, decay = {}, {}, {}, {}, {}, {}
    for (b, c, h) in units:
        r = slice(c * chunk, (c + 1) * chunk)
        q[b, c, h] = l2n(abuf[b, r, h * GDN_DK:(h + 1) * GDN_DK]) * (GDN_DK ** -0.5)
        k[b, c, h] = l2n(abuf[b, r, hd + h * GDN_DK:hd + (h + 1) * GDN_DK])
        v[b, c, h] = abuf[b, r, 2 * hd + h * GDN_DK:2 * hd + (h + 1) * GDN_DK]
        g_col[b, c, h] = gc[b, c][:, SM_DECAY + h:SM_DECAY + h + 1]
        g_row = gc_t[b, c][SM_DECAY - SM_BETA + h:SM_DECAY - SM_BETA + h + 1, :]
        b_col[b, c, h] = beta_all[b][r, SM_BETA + h:SM_BETA + h + 1]
        decay[b, c, h] = jnp.exp(jnp.where(causal, g_col[b, c, h] - g_row, -jnp.inf))
    kk = {x: _mm_nt(k[x], k[x]) for x in units}
    npow = {x: jnp.where(strict, -(b_col[x] * kk[x] * decay[x]), 0.0) for x in units}
    t_mat = {x: eye_c + npow[x] for x in units}
    span = 2
    parts = {x: _split(npow[x]) for x in units}
    while span < chunk:
        parts = {x: _split(_mm3(parts[x], parts[x])) for x in units}
        t_mat = {x: t_mat[x] + _mm3(_split(t_mat[x]), parts[x]) for x in units}
        span *= 2
    u = {x: _mm(t_mat[x], v[x] * b_col[x]) for x in units}
    w = {x: _mm(t_mat[x], k[x] * (b_col[x] * jnp.exp(g_col[x]))) for x in units}
    qk = {x: _mm_nt(q[x], k[x]) * decay[x] for x in units}

    for c in range(n_c):
        now = [(b, c, h) for b in range(bb) for h in range(GDN_HEADS)]
        st = {x: s_ref[x[0], x[2] * GDN_DK:(x[2] + 1) * GDN_DK, :] for x in now}
        v_new = {x: u[x] - _mm(w[x], st[x]) for x in now}
        for x in now:
            b, _, h = x
            g_last = gc[b, c][chunk - 1:chunk, SM_DECAY + h:SM_DECAY + h + 1]
            o_ref[b, c * chunk:(c + 1) * chunk, h * GDN_DK:(h + 1) * GDN_DK] = (
                _mm(q[x] * jnp.exp(g_col[x]), st[x]) + _mm(qk[x], v_new[x]))
            s_ref[b, h * GDN_DK:(h + 1) * GDN_DK, :] = (
                st[x] * jnp.exp(g_last) + _mm_tn(k[x] * jnp.exp(g_last - g_col[x]), v_new[x]))

    for b in range(bb):
        o = _group_rmsnorm(o_ref[b], GDN_DK) * nrm_ref[...]
        o_ref[b] = o * _silu(z_ref[b])


def _gdn(proj, conv0, state, p, *, bb, rows, chunk, valid):
    b, length, _ = proj.shape
    w_qkv = 3 * GDN_HEADS * GDN_DK
    vec = lambda n: pl.BlockSpec((1, n), lambda bi, i: (0, 0))
    return _mixer_call(
        functools.partial(_gdn_kernel, bb=bb, rows=rows, chunk=chunk, valid=valid),
        grid=(b // bb, length // rows),
        in_specs=[
            pl.BlockSpec((bb, rows, w_qkv), lambda bi, i: (bi, i, COL_GDN_QKV // w_qkv)),
            pl.BlockSpec((bb, rows, D_GROUP), lambda bi, i: (bi, i, COL_GDN_Z // D_GROUP)),
            pl.BlockSpec((bb, rows, LANES), lambda bi, i: (bi, i, COL_SMALL // LANES)),
            pl.BlockSpec((bb, SUBLANES, w_qkv), lambda bi, i: (bi, 0, 0)),
            state.in_spec(bb),
            pl.BlockSpec((CONV_W, w_qkv), lambda bi, i: (0, 0)),
            vec(LANES), vec(LANES), vec(D_GROUP),
        ],
        inputs=[proj, proj, proj, conv0, state.init, p["gdn_conv_w"], p["gdn_a_log"], p["gdn_dt_bias"], p["gdn_norm"]],
        out_specs=[pl.BlockSpec((bb, rows, D_GROUP), lambda bi, i: (bi, i, 0)), state.out_spec(bb)],
        out_shape=[jax.ShapeDtypeStruct((b, length, D_GROUP), F32), state.out_shape()],
        states=[state], first_state_out=1,
        scratch=[pltpu.VMEM((bb, rows + SUBLANES, w_qkv), F32), pltpu.VMEM((bb, rows, w_qkv), F32)],
        name="gdn")


def _lane_row(values, offset):
    return jnp.zeros((1, LANES), F32).at[0, offset:offset + values.shape[0]].set(values.astype(F32))


def _prep_layer(w, i, max_rows):
    row = lambda a: a[i].astype(F32).reshape(1, -1)
    p = {}
    for name in ("ffn1", "ffn2"):
        p[name + "_norm"] = row(w[name + "_norm"])
        p[name + "_w_gate"] = _cast_pad(w[name + "_w_gate"], i, D_MODEL, D_FF_PAD)
        p[name + "_w_up"] = _cast_pad(w[name + "_w_up"], i, D_MODEL, D_FF_PAD)
        p[name + "_w_down"] = _cast_pad(w[name + "_w_down"], i, D_FF_PAD, D_MODEL)
    p["mix_norm"] = row(w["mix_norm"])

    wi = w["w_in"][i]
    seg = lambda start, width: wi[:, start:start + width]
    small = jnp.concatenate([seg(1536, 8), seg(3080, 16), seg(5656, 4), seg(5660, 4),
                             jnp.zeros((D_MODEL, LANES - 32), wi.dtype)], axis=1)
    p["w_in"] = jnp.concatenate([
        seg(512, 1024), seg(0, 512), seg(3608, 1536), seg(1544, 512), seg(2568, 512), seg(3096, 512),
        seg(5144, 512), seg(2056, 256), seg(2312, 256), small], axis=1).astype(BF16)
    p["w_out"] = w["w_out"][i].astype(BF16)

    p["ssd_conv_w"] = w["ssd_conv_w"][i].astype(F32)
    p["ssd_conv_b"] = row(w["ssd_conv_b"])
    p["ssd_dt_bias"] = _lane_row(w["ssd_dt_bias"][i], SM_DT)
    p["ssd_a_log"] = _lane_row(w["ssd_a_log"][i], SM_DT)
    p["ssd_d"] = jnp.repeat(w["ssd_d"][i].astype(F32), SSD_HEAD_DIM).reshape(1, D_GROUP)
    p["ssd_norm"] = row(w["ssd_norm"])

    a_re, a_im = w["s5_a_re"][i].astype(F32), w["s5_a_im"][i].astype(F32)
    dt = jnp.exp(w["s5_log_dt"][i].astype(F32))[:, None]
    lam_re, lam_im = a_re * dt, a_im * dt

    def a_bar_pow(n):
        mag = jnp.exp(lam_re * n)
        return mag * jnp.cos(lam_im * n), mag * jnp.sin(lam_im * n)

    abar_re, abar_im = a_bar_pow(1.0)
    den = a_re * a_re + a_im * a_im
    coef_re = ((abar_re - 1.0) * a_re + abar_im * a_im) / den
    coef_im = (abar_im * a_re - (abar_re - 1.0) * a_im) / den
    b_re, b_im = w["s5_b_re"][i].astype(F32), w["s5_b_im"][i].astype(F32)
    bbar_re = coef_re[..., None] * b_re - coef_im[..., None] * b_im
    bbar_im = coef_re[..., None] * b_im + coef_im[..., None] * b_re
    steps = jnp.arange(1, max_rows + 1, dtype=F32)[:, None, None]
    apow_re, apow_im = a_bar_pow(steps)
    p["s5_apow_re"] = apow_re.reshape(max_rows, S5_LANES)
    p["s5_apow_im"] = apow_im.reshape(max_rows, S5_LANES)
    lag_re, lag_im = a_bar_pow(jnp.arange(S5_TAPS, dtype=F32)[:, None, None])
    wlag_re = lag_re[..., None] * bbar_re - lag_im[..., None] * bbar_im
    wlag_im = lag_re[..., None] * bbar_im + lag_im[..., None] * bbar_re
    n_blk = D_GROUP // LANES
    gpb = S5_GROUPS // n_blk
    eye = jnp.eye(gpb, dtype=F32)

    def lag_blocks(x):
        x = jnp.swapaxes(x, 2, 3).reshape(S5_TAPS, n_blk, gpb, S5_CH, S5_STATE)
        return jnp.einsum("tjgip,gh->jtgihp", x, eye).reshape(n_blk, S5_TAPS * LANES, gpb * S5_STATE).astype(BF16)

    def c_blocks(x):
        x = x.reshape(n_blk, gpb, S5_CH, S5_STATE)
        return jnp.einsum("jgip,gh->jgphi", x, eye).reshape(n_blk, gpb * S5_STATE, LANES).astype(BF16)

    p["s5_wlag_re"], p["s5_wlag_im"] = lag_blocks(wlag_re), lag_blocks(wlag_im)
    p["s5_cblk_re"] = c_blocks(w["s5_c_re"][i].astype(F32))
    p["s5_cblk_im"] = c_blocks(w["s5_c_im"][i].astype(F32))
    p["s5_d"] = row(w["s5_d"])
    p["s5_w_glu"] = w["s5_w_glu"][i].astype(BF16)
    p["s5_b_glu"] = row(w["s5_b_glu"])

    hk = GLA_HEADS * GLA_DK
    p["gla_w_gate2"] = jnp.zeros((LANES, hk), F32).at[SM_GR:SM_GR + GLA_GATE_RANK].set(w["gla_w_gate2"][i].astype(F32))
    p["gla_b_gate2"] = row(w["gla_b_gate2"])
    p["gla_norm"] = row(w["gla_norm"])

    p["gdn_conv_w"] = w["gdn_conv_w"][i].astype(F32)
    p["gdn_a_log"] = _lane_row(w["gdn_a_log"][i], SM_DECAY)
    p["gdn_dt_bias"] = _lane_row(w["gdn_dt_bias"][i], SM_DECAY)
    p["gdn_norm"] = row(w["gdn_norm"])
    return p


def _mixer_geometry(bsz, length):
    if length >= CHUNK:
        assert length % CHUNK == 0
        rows = 256 if length % 256 == 0 else CHUNK
        return 1, rows, CHUNK, length
    assert length <= SUBLANES
    bb = SUBLANES if bsz % SUBLANES == 0 else 1
    return bb, SUBLANES, SUBLANES, SUBLANES


def _conv_buffer(state):
    return jnp.pad(state.astype(F32), ((0, 0), (SUBLANES - (CONV_W - 1), 0), (0, 0)))


def _trunk(x, states, layers, final_norm):
    bsz, length, _ = x.shape
    depth = len(layers)
    bb, rows, chunk, lpad = _mixer_geometry(bsz, length)
    geo = dict(bb=bb, rows=rows, chunk=chunk, valid=min(length, rows))
    h = x.reshape(bsz * length, D_MODEL).astype(F32)
    fg = final_norm.astype(F32).reshape(1, D_MODEL)
    ssd_conv, ssd_h, s5_re, s5_im, gla_s, gdn_conv, gdn_s = [s.astype(F32) for s in states]
    lead = ssd_h.shape[0]
    recurrent = [ssd_h.reshape(lead, bsz, D_GROUP, SSD_STATE), s5_re.reshape(lead, bsz, 1, S5_LANES),
                 s5_im.reshape(lead, bsz, 1, S5_LANES), gla_s.reshape(lead, bsz, GLA_HEADS * GLA_DK, GLA_DV),
                 gdn_s.reshape(lead, bsz, GDN_HEADS * GDN_DK, GDN_DK)]
    new = [None] * len(recurrent)
    ssd_conv_new, gdn_conv_new = [], []
    for i, p in enumerate(layers):
        src = min(i, lead - 1)
        st = [_Layered(init, src, i, depth, prev) for init, prev in zip(recurrent, new)]
        h = _ffn(h, p["ffn1_norm"], p["ffn1_w_gate"], p["ffn1_w_up"], p["ffn1_w_down"], fg, False)
        proj = _in_proj(h, p["mix_norm"], p["w_in"]).reshape(bsz, length, PROJ_COLS)
        ssd_conv_new.append(proj[:, length - (CONV_W - 1):, COL_XBC:COL_XBC + 1024])
        gdn_conv_new.append(proj[:, length - (CONV_W - 1):, COL_GDN_QKV:COL_GDN_QKV + 1536])
        if lpad != length:
            proj = jnp.pad(proj, ((0, 0), (0, lpad - length), (0, 0)))
        y_ssd, new[0] = _ssd(proj, _conv_buffer(ssd_conv[src]), st[0], p, **geo)
        y_s5, new[1], new[2] = _s5(proj, st[1], st[2], p, bb=bb, rows=rows, valid=geo["valid"])
        y_gla, new[3] = _gla(proj, st[3], p, **geo)
        y_gdn, new[4] = _gdn(proj, _conv_buffer(gdn_conv[src]), st[4], p, **geo)
        ys = [y[:, :length].reshape(bsz * length, D_GROUP) for y in (y_ssd, y_s5, y_gla, y_gdn)]
        h = _out_proj(h, ys, p["w_out"])
        h = _ffn(h, p["ffn2_norm"], p["ffn2_w_gate"], p["ffn2_w_up"], p["ffn2_w_down"], fg, i == depth - 1)
    out_states = (jnp.stack(ssd_conv_new), new[0].reshape(depth, bsz, SSD_HEADS, SSD_HEAD_DIM, SSD_STATE),
                  new[1].reshape(depth, bsz, S5_GROUPS, S5_STATE), new[2].reshape(depth, bsz, S5_GROUPS, S5_STATE),
                  new[3].reshape(depth, bsz, GLA_HEADS, GLA_DK, GLA_DV), jnp.stack(gdn_conv_new),
                  new[4].reshape(depth, bsz, GDN_HEADS, GDN_DK, GDN_DK))
    return h.reshape(bsz, length, D_MODEL), out_states


def kernel(x_prompt, x_sample, state_ssd_conv, state_ssd, state_s5_re, state_s5_im, state_gla, state_gdn_conv, state_gdn, ffn1_norm, ffn1_w_gate, ffn1_w_up, ffn1_w_down, mix_norm, w_in, ssd_conv_w, ssd_conv_b, ssd_dt_bias, ssd_a_log, ssd_d, ssd_norm, s5_a_re, s5_a_im, s5_log_dt, s5_b_re, s5_b_im, s5_c_re, s5_c_im, s5_d, s5_w_glu, s5_b_glu, gla_w_gate2, gla_b_gate2, gla_norm, gdn_conv_w, gdn_a_log, gdn_dt_bias, gdn_norm, w_out, ffn2_norm, ffn2_w_gate, ffn2_w_up, ffn2_w_down, final_norm):
    w = dict(ffn1_norm=ffn1_norm, ffn1_w_gate=ffn1_w_gate, ffn1_w_up=ffn1_w_up, ffn1_w_down=ffn1_w_down,
             mix_norm=mix_norm, w_in=w_in, ssd_conv_w=ssd_conv_w, ssd_conv_b=ssd_conv_b, ssd_dt_bias=ssd_dt_bias,
             ssd_a_log=ssd_a_log, ssd_d=ssd_d, ssd_norm=ssd_norm, s5_a_re=s5_a_re, s5_a_im=s5_a_im,
             s5_log_dt=s5_log_dt, s5_b_re=s5_b_re, s5_b_im=s5_b_im, s5_c_re=s5_c_re, s5_c_im=s5_c_im, s5_d=s5_d,
             s5_w_glu=s5_w_glu, s5_b_glu=s5_b_glu, gla_w_gate2=gla_w_gate2, gla_b_gate2=gla_b_gate2,
             gla_norm=gla_norm, gdn_conv_w=gdn_conv_w, gdn_a_log=gdn_a_log, gdn_dt_bias=gdn_dt_bias,
             gdn_norm=gdn_norm, w_out=w_out, ffn2_norm=ffn2_norm, ffn2_w_gate=ffn2_w_gate, ffn2_w_up=ffn2_w_up,
             ffn2_w_down=ffn2_w_down)
    depth = w_in.shape[0]
    max_rows = max(_mixer_geometry(*x_prompt.shape[:2])[1], _mixer_geometry(*x_sample.shape[:2])[1])
    layers = [_prep_layer(w, i, max_rows) for i in range(depth)]

    sample_states = (state_ssd_conv, state_ssd, state_s5_re, state_s5_im, state_gla, state_gdn_conv, state_gdn)
    n_prompt = x_prompt.shape[0]
    prompt_states = tuple(jnp.zeros((1, n_prompt) + s.shape[2:], F32) for s in sample_states)
    y_prompt, p_states = _trunk(x_prompt, prompt_states, layers, final_norm)
    y_sample, s_states = _trunk(x_sample, sample_states, layers, final_norm)
    return (y_prompt, y_sample) + p_states + s_states
```

```python
import functools
import math

import numpy as np
import jax
import jax.numpy as jnp
from jax import lax
from jax.experimental import pallas as pl
from jax.experimental.pallas import tpu as pltpu

F32 = jnp.float32
BF16 = jnp.bfloat16
HIGHEST = lax.Precision.HIGHEST
EPS = 1e-6

LANES = 128
SUBLANES = 8
VMEM_LIMIT_BYTES = 56 * 1024 * 1024

D_MODEL = 2048
D_GROUP = 512
D_FF = 5504
D_FF_PAD = 5632
CONV_W = 4
CHUNK = 64

SSD_HEADS = 8
SSD_HEAD_DIM = 64
SSD_GROUPS = 2
SSD_STATE = 128
S5_GROUPS = 32
S5_CH = 16
S5_STATE = 64
S5_LANES = S5_GROUPS * S5_STATE
S5_TAPS = 8
GLA_HEADS = 4
GLA_DK = 64
GLA_DV = 128
GLA_GATE_RANK = 16
GLA_GATE_TEMP = 16.0
GDN_HEADS = 4
GDN_DK = 128

COL_XBC = 0
COL_SSD_Z = 1024
COL_GDN_QKV = 1536
COL_S5_U = 3072
COL_GLA_V = 3584
COL_GLA_R = 4096
COL_GDN_Z = 4608
COL_GLA_Q = 5120
COL_GLA_K = 5376
COL_SMALL = 5632
PROJ_COLS = 5760
SM_DT = 0
SM_GR = 8
SM_BETA = 24
SM_DECAY = 28


def _mm(a, b):
    return jnp.dot(a.astype(BF16), b.astype(BF16), preferred_element_type=F32)


def _mm_nt(a, b):
    return lax.dot_general(a.astype(BF16), b.astype(BF16), (((1,), (1,)), ((), ())), preferred_element_type=F32)


def _mm_tn(a, b):
    return lax.dot_general(a.astype(BF16), b.astype(BF16), (((0,), (0,)), ((), ())), preferred_element_type=F32)


def _mm_hi(a, b):
    return jnp.dot(a, b, precision=HIGHEST, preferred_element_type=F32)


def _mm_nt_hi(a, b):
    return lax.dot_general(a, b, (((1,), (1,)), ((), ())), precision=HIGHEST, preferred_element_type=F32)


def _split(a):
    hi = a.astype(BF16)
    return hi, (a - hi.astype(F32)).astype(BF16)


def _mm3(a, b):
    (ah, al), (bh, bl) = a, b
    dot = functools.partial(jnp.dot, preferred_element_type=F32)
    return dot(ah, bh) + (dot(al, bh) + dot(ah, bl))


def _head_rows(first_lane):
    return (_iota((SUBLANES, LANES), 1) == _iota((SUBLANES, LANES), 0) + first_lane).astype(F32)


def _sigmoid(x):
    return 1.0 / (1.0 + jnp.exp(-x))


def _silu(x):
    return x * _sigmoid(x)


def _softplus(x):
    return jnp.maximum(x, 0.0) + jnp.log1p(jnp.exp(-jnp.abs(x)))


def _iota(shape, dim):
    return lax.broadcasted_iota(jnp.int32, shape, dim)


def _tril(n, strict=False):
    r, c = _iota((n, n), 0), _iota((n, n), 1)
    return (r > c) if strict else (r >= c)


def _eye(n):
    return (_iota((n, n), 0) == _iota((n, n), 1)).astype(F32)


def _row_to_col(row):
    n = row.shape[1]
    return jnp.sum(_eye(n) * row, axis=1, keepdims=True)


def _group_rmsnorm(y, width):
    parts = []
    for g in range(y.shape[1] // width):
        yg = y[:, g * width:(g + 1) * width]
        parts.append(yg * lax.rsqrt(jnp.mean(yg * yg, axis=1, keepdims=True) + EPS))
    return jnp.concatenate(parts, axis=1)


def _causal_conv(x, cbuf, w, rows):
    cbuf[SUBLANES:SUBLANES + rows, :] = x
    y = (w[0:1, :] * cbuf[5:5 + rows, :] + w[1:2, :] * cbuf[6:6 + rows, :]
         + w[2:3, :] * cbuf[7:7 + rows, :] + w[3:4, :] * x)
    cbuf[0:SUBLANES, :] = cbuf[rows:rows + SUBLANES, :]
    return y


def _params(*sem):
    return pltpu.CompilerParams(dimension_semantics=sem, vmem_limit_bytes=VMEM_LIMIT_BYTES)


class _Layered:
    def __init__(self, init, src, dst, depth, prev):
        self.init, self.src, self.dst, self.depth, self.prev = init, src, dst, depth, prev

    def in_spec(self, bb):
        tail = self.init.shape[2:]
        return pl.BlockSpec((None, bb) + tail, lambda bi, i, s=self.src, z=(0,) * len(tail): (s, bi) + z)

    def out_spec(self, bb):
        tail = self.init.shape[2:]
        return pl.BlockSpec((None, bb) + tail, lambda bi, i, d=self.dst, z=(0,) * len(tail): (d, bi) + z)

    def out_shape(self):
        return jax.ShapeDtypeStruct((self.depth,) + self.init.shape[1:], F32)


def _mixer_call(body, *, grid, in_specs, inputs, out_specs, out_shape, states, first_state_out, scratch, name):
    n_in = len(inputs)
    prevs = [s.prev for s in states if s.prev is not None]
    aliases = {}
    fn = body
    if prevs:
        assert len(prevs) == len(states)
        in_specs = list(in_specs) + [pl.BlockSpec(memory_space=pl.ANY)] * len(prevs)
        aliases = {n_in + j: first_state_out + j for j in range(len(prevs))}

        def fn(*refs):
            return body(*refs[:n_in], *refs[n_in + len(prevs):])

    return pl.pallas_call(
        fn, grid=grid, in_specs=in_specs, out_specs=out_specs, out_shape=out_shape, scratch_shapes=scratch,
        input_output_aliases=aliases, compiler_params=_params("arbitrary", "arbitrary"), name=name,
    )(*inputs, *prevs)


def _ffn_kernel(x_ref, g_ref, wg_ref, wu_ref, wd_ref, fg_ref, o_ref, xn_ref, *, n_f, final_norm):
    f = pl.program_id(1)

    @pl.when(f == 0)
    def _():
        x = x_ref[...]
        xn = x * lax.rsqrt(jnp.mean(x * x, axis=1, keepdims=True) + EPS) * g_ref[...]
        xn_ref[...] = xn.astype(BF16)
        o_ref[...] = jnp.zeros_like(o_ref)

    xn = xn_ref[...]
    gate = jnp.dot(xn, wg_ref[...], preferred_element_type=F32)
    up = jnp.dot(xn, wu_ref[...], preferred_element_type=F32)
    h = (_silu(gate) * up).astype(BF16)
    o_ref[...] += jnp.dot(h, wd_ref[...], preferred_element_type=F32)

    @pl.when(f == n_f - 1)
    def _():
        y = x_ref[...] + 0.5 * o_ref[...]
        if final_norm:
            y = y * lax.rsqrt(jnp.mean(y * y, axis=1, keepdims=True) + EPS) * fg_ref[...]
        o_ref[...] = y


def _ffn(x, g, wg, wu, wd, fg, final_norm):
    m = x.shape[0]
    tm = min(1024, m)
    tf = 256
    n_f = D_FF_PAD // tf
    return pl.pallas_call(
        functools.partial(_ffn_kernel, n_f=n_f, final_norm=final_norm),
        grid=(m // tm, n_f),
        in_specs=[
            pl.BlockSpec((tm, D_MODEL), lambda i, f: (i, 0)),
            pl.BlockSpec((1, D_MODEL), lambda i, f: (0, 0)),
            pl.BlockSpec((D_MODEL, tf), lambda i, f: (0, f)),
            pl.BlockSpec((D_MODEL, tf), lambda i, f: (0, f)),
            pl.BlockSpec((tf, D_MODEL), lambda i, f: (f, 0)),
            pl.BlockSpec((1, D_MODEL), lambda i, f: (0, 0)),
        ],
        out_specs=pl.BlockSpec((tm, D_MODEL), lambda i, f: (i, 0)),
        out_shape=jax.ShapeDtypeStruct((m, D_MODEL), F32),
        scratch_shapes=[pltpu.VMEM((tm, D_MODEL), BF16)],
        compiler_params=_params("arbitrary", "arbitrary"),
        name="ffn",
    )(x, g, wg, wu, wd, fg)


def _in_proj_kernel(x_ref, g_ref, w_ref, o_ref, xn_ref):
    @pl.when(pl.program_id(1) == 0)
    def _():
        x = x_ref[...]
        xn = x * lax.rsqrt(jnp.mean(x * x, axis=1, keepdims=True) + EPS) * g_ref[...]
        xn_ref[...] = xn.astype(BF16)

    o_ref[...] = jnp.dot(xn_ref[...], w_ref[...], preferred_element_type=F32)


def _in_proj(x, g, w):
    m = x.shape[0]
    tm = min(1024, m)
    tn = 1920
    return pl.pallas_call(
        _in_proj_kernel,
        grid=(m // tm, PROJ_COLS // tn),
        in_specs=[
            pl.BlockSpec((tm, D_MODEL), lambda i, n: (i, 0)),
            pl.BlockSpec((1, D_MODEL), lambda i, n: (0, 0)),
            pl.BlockSpec((D_MODEL, tn), lambda i, n: (0, n)),
        ],
        out_specs=pl.BlockSpec((tm, tn), lambda i, n: (i, n)),
        out_shape=jax.ShapeDtypeStruct((m, PROJ_COLS), F32),
        scratch_shapes=[pltpu.VMEM((tm, D_MODEL), BF16)],
        compiler_params=_params("arbitrary", "arbitrary"),
        name="in_proj",
    )(x, g, w)


def _out_proj_kernel(x_ref, y0_ref, y1_ref, y2_ref, y3_ref, w_ref, o_ref):
    acc = x_ref[...]
    for j, y_ref in enumerate((y0_ref, y1_ref, y2_ref, y3_ref)):
        acc = acc + jnp.dot(y_ref[...].astype(BF16), w_ref[j * D_GROUP:(j + 1) * D_GROUP, :],
                            preferred_element_type=F32)
    o_ref[...] = acc


def _out_proj(x, ys, w):
    m = x.shape[0]
    tm = min(512, m)
    yspec = pl.BlockSpec((tm, D_GROUP), lambda i: (i, 0))
    return pl.pallas_call(
        _out_proj_kernel,
        grid=(m // tm,),
        in_specs=[pl.BlockSpec((tm, D_MODEL), lambda i: (i, 0)), yspec, yspec, yspec, yspec,
                  pl.BlockSpec((D_MODEL, D_MODEL), lambda i: (0, 0))],
        out_specs=pl.BlockSpec((tm, D_MODEL), lambda i: (i, 0)),
        out_shape=jax.ShapeDtypeStruct((m, D_MODEL), F32),
        compiler_params=_params("arbitrary"),
        name="out_proj",
    )(x, *ys, w)


def _cast_pad_cols_kernel(w_ref, o_ref):
    cols = w_ref.shape[1]
    o_ref[:, 0:cols] = w_ref[...].astype(BF16)
    o_ref[:, cols:] = jnp.zeros((o_ref.shape[0], o_ref.shape[1] - cols), BF16)


def _cast_pad_rows_kernel(w_ref, o_ref):
    rows = w_ref.shape[0]
    o_ref[0:rows, :] = w_ref[...].astype(BF16)
    o_ref[rows:, :] = jnp.zeros((o_ref.shape[0] - rows, o_ref.shape[1]), BF16)


def _cast_pad(w, layer, rows_to, cols_to):
    _, rows, cols = w.shape
    if cols_to != cols:
        assert rows_to == rows and cols % LANES == 0
        tr = 256
        return pl.pallas_call(
            _cast_pad_cols_kernel,
            grid=(rows // tr,),
            in_specs=[pl.BlockSpec((None, tr, cols), lambda r: (layer, r, 0))],
            out_specs=pl.BlockSpec((tr, cols_to), lambda r: (r, 0)),
            out_shape=jax.ShapeDtypeStruct((rows, cols_to), BF16),
            compiler_params=_params("arbitrary"),
            name="cast_pad_cols",
        )(w)
    tc = 512
    assert cols_to == cols and cols % tc == 0 and rows % 16 == 0
    return pl.pallas_call(
        _cast_pad_rows_kernel,
        grid=(cols // tc,),
        in_specs=[pl.BlockSpec((None, rows, tc), lambda c: (layer, 0, c))],
        out_specs=pl.BlockSpec((rows_to, tc), lambda c: (0, c)),
        out_shape=jax.ShapeDtypeStruct((rows_to, cols), BF16),
        compiler_params=_params("arbitrary"),
        name="cast_pad_rows",
    )(w)


def _ssd_kernel(xbc_ref, z_ref, sm_ref, conv0_ref, h0_ref, cw_ref, cb_ref, dtb_ref, alog_ref, dex_ref, nrm_ref,
                y_ref, h_ref, cbuf, abuf, *, bb, rows, chunk, valid):
    @pl.when(pl.program_id(1) == 0)
    def _():
        cbuf[:, 0:SUBLANES, :] = conv0_ref[...]
        h_ref[...] = h0_ref[...]

    a_neg = -jnp.exp(alog_ref[...])
    tril = _tril(chunk).astype(F32)
    causal = _tril(chunk)
    left = _iota((1, LANES), 1) < SSD_HEAD_DIM
    top = _iota((LANES, 1), 0) < SSD_HEAD_DIM
    head_rows = _head_rows(SM_DT)

    def pair(v, h0):
        return jnp.where(left, v[:, h0:h0 + 1], v[:, h0 + 1:h0 + 2])

    dt_all = []
    for b in range(bb):
        conv = _causal_conv(xbc_ref[b], cbuf.at[b], cw_ref[...], rows) + cb_ref[...]
        abuf[b] = _silu(conv)
        dt = _softplus(sm_ref[b] + dtb_ref[...])
        if valid < rows:
            dt = jnp.where(_iota((rows, LANES), 0) < valid, dt, 0.0)
        dt_all.append(dt)

    n_c = rows // chunk
    seqs = [(b, c) for b in range(bb) for c in range(n_c)]
    dt = {(b, c): dt_all[b][c * chunk:(c + 1) * chunk, :] for (b, c) in seqs}
    acs = {s: _mm_hi(tril, dt[s] * a_neg) for s in seqs}
    acs_t = {s: _mm_nt_hi(head_rows, acs[s]) for s in seqs}
    groups = [(b, c, g) for (b, c) in seqs for g in range(SSD_GROUPS)]
    bm, cm = {}, {}
    for (b, c, g) in groups:
        r = slice(c * chunk, (c + 1) * chunk)
        bm[b, c, g] = abuf[b, r, D_GROUP + g * SSD_STATE:D_GROUP + (g + 1) * SSD_STATE]
        cm[b, c, g] = abuf[b, r, D_GROUP + (SSD_GROUPS + g) * SSD_STATE:D_GROUP + (SSD_GROUPS + g + 1) * SSD_STATE]
    cb = {x: _mm_nt(cm[x], bm[x]) for x in groups}
    pairs = [(b, c, g, j) for (b, c, g) in groups for j in range(2)]
    xdt, y_diag = {}, {}
    for (b, c, g, j) in pairs:
        h0 = 4 * g + 2 * j
        lo = (2 * g + j) * LANES
        xdt[b, c, g, j] = abuf[b, c * chunk:(c + 1) * chunk, lo:lo + LANES] * pair(dt[b, c], h0)
    for (b, c, g, j) in pairs:
        h0 = 4 * g + 2 * j
        total = None
        for hh, keep in ((h0, left), (h0 + 1, jnp.logical_not(left))):
            diff = acs[b, c][:, hh:hh + 1] - acs_t[b, c][hh:hh + 1, :]
            decay = jnp.exp(jnp.where(causal, diff, -jnp.inf))
            part = _mm(cb[b, c, g] * decay, jnp.where(keep, xdt[b, c, g, j], 0.0))
            total = part if total is None else total + part
        y_diag[b, c, g, j] = total

    for c in range(n_c):
        r0 = c * chunk
        for b in range(bb):
            a_c = acs[b, c]
            exp_acs = jnp.exp(a_c)
            to_end = jnp.exp(a_c[chunk - 1:chunk, :] - a_c)
            end_decay = jnp.exp(a_c[chunk - 1:chunk, :])
            for g in range(SSD_GROUPS):
                for j in range(2):
                    h0 = 4 * g + 2 * j
                    lo = (2 * g + j) * LANES
                    st = h_ref[b, lo:lo + LANES, :]
                    y_ref[b, r0:r0 + chunk, lo:lo + LANES] = (
                        y_diag[b, c, g, j] + _mm_nt(cm[b, c, g], st) * pair(exp_acs, h0))
                    dec = jnp.where(top, end_decay[:, h0:h0 + 1], end_decay[:, h0 + 1:h0 + 2])
                    h_ref[b, lo:lo + LANES, :] = st * dec + _mm_tn(xdt[b, c, g, j] * pair(to_end, h0), bm[b, c, g])

    for b in range(bb):
        y = y_ref[b] + dex_ref[...] * abuf[b, :, 0:D_GROUP]
        y = y * _silu(z_ref[b])
        y_ref[b] = _group_rmsnorm(y, D_GROUP // SSD_GROUPS) * nrm_ref[...]


def _ssd(proj, conv0, state, p, *, bb, rows, chunk, valid):
    b, length, _ = proj.shape
    w_xbc = D_GROUP + 2 * SSD_GROUPS * SSD_STATE
    vec = lambda n: pl.BlockSpec((1, n), lambda bi, i: (0, 0))
    return _mixer_call(
        functools.partial(_ssd_kernel, bb=bb, rows=rows, chunk=chunk, valid=valid),
        grid=(b // bb, length // rows),
        in_specs=[
            pl.BlockSpec((bb, rows, w_xbc), lambda bi, i: (bi, i, COL_XBC // w_xbc)),
            pl.BlockSpec((bb, rows, D_GROUP), lambda bi, i: (bi, i, COL_SSD_Z // D_GROUP)),
            pl.BlockSpec((bb, rows, LANES), lambda bi, i: (bi, i, COL_SMALL // LANES)),
            pl.BlockSpec((bb, SUBLANES, w_xbc), lambda bi, i: (bi, 0, 0)),
            state.in_spec(bb),
            pl.BlockSpec((CONV_W, w_xbc), lambda bi, i: (0, 0)),
            vec(w_xbc), vec(LANES), vec(LANES), vec(D_GROUP), vec(D_GROUP),
        ],
        inputs=[proj, proj, proj, conv0, state.init, p["ssd_conv_w"], p["ssd_conv_b"], p["ssd_dt_bias"],
                p["ssd_a_log"], p["ssd_d"], p["ssd_norm"]],
        out_specs=[pl.BlockSpec((bb, rows, D_GROUP), lambda bi, i: (bi, i, 0)), state.out_spec(bb)],
        out_shape=[jax.ShapeDtypeStruct((b, length, D_GROUP), F32), state.out_shape()],
        states=[state], first_state_out=1,
        scratch=[pltpu.VMEM((bb, rows + SUBLANES, w_xbc), F32), pltpu.VMEM((bb, rows, w_xbc), F32)],
        name="ssd")


def _s5_kernel(u_ref, h0re_ref, h0im_ref, apre_ref, apim_ref, wre_ref, wim_ref, cre_ref, cim_ref, d_ref, wglu_ref,
               bglu_ref, y_ref, hre_ref, him_ref, ubuf, *carry, bb, rows, valid):
    first = pl.program_id(1) == 0
    m = bb * rows
    n_blk = D_GROUP // LANES
    w_blk = S5_LANES // n_blk
    n_tiles = rows // SUBLANES

    @pl.when(first)
    def _():
        ubuf[:, 0:SUBLANES, :] = jnp.zeros((bb, SUBLANES, D_GROUP), F32)
        for buf in carry[2:]:
            buf[...] = jnp.zeros(buf.shape, F32)

    ubuf[:, SUBLANES:SUBLANES + rows, :] = u_ref[...]

    parts = []
    for j in range(n_blk):
        sl = slice(j * w_blk, (j + 1) * w_blk)
        taps = [ubuf[:, SUBLANES - t:SUBLANES - t + rows, j * LANES:(j + 1) * LANES].reshape(m, LANES)
                for t in range(S5_TAPS)]
        lagged = jnp.concatenate(taps, axis=1).astype(BF16)
        xr = jnp.dot(lagged, wre_ref[j], preferred_element_type=F32)
        xi = jnp.dot(lagged, wim_ref[j], preferred_element_type=F32)
        if n_tiles > 1:
            hbuf_re, hbuf_im, tail_re, tail_im = carry
            ar, ai = apre_ref[S5_TAPS - 1:S5_TAPS, sl], apim_ref[S5_TAPS - 1:S5_TAPS, sl]
            pr, pi = tail_re[:, sl], tail_im[:, sl]
            for q in range(n_tiles):
                rs = slice(q * SUBLANES, (q + 1) * SUBLANES)
                pr, pi = xr[rs, :] + ar * pr - ai * pi, xi[rs, :] + ar * pi + ai * pr
                hbuf_re[rs, :] = pr
                hbuf_im[rs, :] = pi

            @pl.when(first)
            def _():
                h0r, h0i = h0re_ref[0, :, sl], h0im_ref[0, :, sl]
                apr, api = apre_ref[:, sl], apim_ref[:, sl]
                hbuf_re[...] = hbuf_re[...] + apr * h0r - api * h0i
                hbuf_im[...] = hbuf_im[...] + apr * h0i + api * h0r

            hr, hi = hbuf_re[...], hbuf_im[...]
            tail_re[:, sl] = hr[rows - SUBLANES:rows, :]
            tail_im[:, sl] = hi[rows - SUBLANES:rows, :]
            hre_ref[0, :, sl] = hr[valid - 1:valid, :]
            him_ref[0, :, sl] = hi[valid - 1:valid, :]
        else:
            h0r, h0i = h0re_ref[:, :, sl], h0im_ref[:, :, sl]
            apr, api = apre_ref[:, sl][None], apim_ref[:, sl][None]
            hr3 = xr.reshape(bb, rows, w_blk) + apr * h0r - api * h0i
            hi3 = xi.reshape(bb, rows, w_blk) + apr * h0i + api * h0r
            hre_ref[:, :, sl] = hr3[:, valid - 1:valid, :]
            him_ref[:, :, sl] = hi3[:, valid - 1:valid, :]
            hr, hi = hr3.reshape(m, w_blk), hi3.reshape(m, w_blk)
        parts.append(_mm(hr, cre_ref[j]) - _mm(hi, cim_ref[j]))

    if n_tiles > 1:
        ubuf[:, 0:SUBLANES, :] = ubuf[:, rows:rows + SUBLANES, :]

    y = jnp.concatenate(parts, axis=1) + d_ref[...] * u_ref[...].reshape(m, D_GROUP)
    y = 0.5 * y * (1.0 + jnp.tanh(math.sqrt(2.0 / math.pi) * (y + 0.044715 * (y * y * y))))
    y = y * _sigmoid(_mm(y, wglu_ref[...]) + bglu_ref[...])
    y_ref[...] = y.reshape(bb, rows, D_GROUP)


def _s5_lag_kernel(pre_ref, pim_ref, bre_ref, bim_ref, ore_ref, oim_ref):
    br, bi = bre_ref[...], bim_ref[...]
    for t in range(S5_TAPS):
        pr, pi = pre_ref[t:t + 1, :], pim_ref[t:t + 1, :]
        ore_ref[t * LANES:(t + 1) * LANES, :] = (pr * br - pi * bi).astype(BF16)
        oim_ref[t * LANES:(t + 1) * LANES, :] = (pr * bi + pi * br).astype(BF16)


def _s5_lag_weights(pow_re, pow_im, b_re, b_im):
    n_blk, _, w_blk = b_re.shape
    pw = pl.BlockSpec((S5_TAPS, w_blk), lambda j: (0, j))
    bs = pl.BlockSpec((None, LANES, w_blk), lambda j: (j, 0, 0))
    out = pl.BlockSpec((None, S5_TAPS * LANES, w_blk), lambda j: (j, 0, 0))
    shape = jax.ShapeDtypeStruct((n_blk, S5_TAPS * LANES, w_blk), BF16)
    return pl.pallas_call(
        _s5_lag_kernel, grid=(n_blk,), in_specs=[pw, pw, bs, bs], out_specs=[out, out], out_shape=[shape, shape],
        compiler_params=_params("arbitrary"), name="s5_lag_weights",
    )(pow_re, pow_im, b_re, b_im)


def _s5(proj, state_re, state_im, p, *, bb, rows, valid):
    b, length, _ = proj.shape
    assert rows == S5_TAPS or bb == 1
    n_blk = D_GROUP // LANES
    w_blk = S5_LANES // n_blk
    vec = lambda n: pl.BlockSpec((1, n), lambda bi, i: (0, 0))
    full3 = lambda s: pl.BlockSpec(s, lambda bi, i: (0, 0, 0))
    carry = []
    if rows > S5_TAPS:
        carry = [pltpu.VMEM((rows, w_blk), F32)] * 2 + [pltpu.VMEM((SUBLANES, S5_LANES), F32)] * 2
    return _mixer_call(
        functools.partial(_s5_kernel, bb=bb, rows=rows, valid=valid),
        grid=(b // bb, length // rows),
        in_specs=[
            pl.BlockSpec((bb, rows, D_GROUP), lambda bi, i: (bi, i, COL_S5_U // D_GROUP)),
            state_re.in_spec(bb), state_im.in_spec(bb),
            pl.BlockSpec((rows, S5_LANES), lambda bi, i: (0, 0)),
            pl.BlockSpec((rows, S5_LANES), lambda bi, i: (0, 0)),
            full3((n_blk, S5_TAPS * LANES, w_blk)), full3((n_blk, S5_TAPS * LANES, w_blk)),
            full3((n_blk, w_blk, LANES)), full3((n_blk, w_blk, LANES)),
            vec(D_GROUP), pl.BlockSpec((D_GROUP, D_GROUP), lambda bi, i: (0, 0)), vec(D_GROUP),
        ],
        inputs=[proj, state_re.init, state_im.init, p["s5_apow_re"][:rows], p["s5_apow_im"][:rows], p["s5_wlag_re"],
                p["s5_wlag_im"], p["s5_cblk_re"], p["s5_cblk_im"], p["s5_d"], p["s5_w_glu"], p["s5_b_glu"]],
        out_specs=[pl.BlockSpec((bb, rows, D_GROUP), lambda bi, i: (bi, i, 0)),
                   state_re.out_spec(bb), state_im.out_spec(bb)],
        out_shape=[jax.ShapeDtypeStruct((b, length, D_GROUP), F32), state_re.out_shape(), state_im.out_shape()],
        states=[state_re, state_im], first_state_out=1,
        scratch=[pltpu.VMEM((bb, SUBLANES + rows, D_GROUP), F32)] + carry,
        name="s5")


def _gla_kernel(q_ref, k_ref, v_ref, r_ref, sm_ref, s0_ref, wg_ref, bg_ref, nrm_ref, o_ref, s_ref,
                *, bb, rows, chunk, valid):
    @pl.when(pl.program_id(1) == 0)
    def _():
        s_ref[...] = s0_ref[...]

    hk = GLA_HEADS * GLA_DK
    tril = _tril(chunk).astype(F32)
    causal = _tril(chunk)
    lane = _iota((1, hk), 1)
    lg_all, k_all = [], []
    for b in range(bb):
        lg = -_softplus(-(_mm(sm_ref[b], wg_ref[...]) + bg_ref[...])) * (1.0 / GLA_GATE_TEMP)
        k = k_ref[b]
        if valid < rows:
            live = _iota((rows, hk), 0) < valid
            lg = jnp.where(live, lg, 0.0)
            k = jnp.where(live, k, 0.0)
        lg_all.append(lg)
        k_all.append(k)

    n_c = rows // chunk
    seqs = [(b, c) for b in range(bb) for c in range(n_c)]
    cum = {(b, c): _mm_hi(tril, lg_all[b][c * chunk:(c + 1) * chunk, :]) for (b, c) in seqs}
    q_dec, k_dec, k_end, end_decay = {}, {}, {}, {}
    for (b, c) in seqs:
        r = slice(c * chunk, (c + 1) * chunk)
        cs = cum[b, c]
        cum_last = cs[chunk - 1:chunk, :]
        q_dec[b, c] = q_ref[b, r, :] * (GLA_DK ** -0.5) * jnp.exp(cs)
        k = k_all[b][r, :]
        k_dec[b, c] = k * jnp.exp(-cs)
        k_end[b, c] = k * jnp.exp(cum_last - cs)
        end_decay[b, c] = _row_to_col(jnp.exp(cum_last))
    units = [(b, c, h) for (b, c) in seqs for h in range(GLA_HEADS)]
    qh, vh = {}, {}
    for (b, c, h) in units:
        mine = (lane >= h * GLA_DK) & (lane < (h + 1) * GLA_DK)
        qh[b, c, h] = jnp.where(mine, q_dec[b, c], 0.0)
        vh[b, c, h] = v_ref[b, c * chunk:(c + 1) * chunk, h * GLA_DV:(h + 1) * GLA_DV]
    att = {x: jnp.where(causal, _mm_nt(qh[x], k_dec[x[:2]]), 0.0) for x in units}
    o_intra = {x: _mm(att[x], vh[x]) for x in units}
    kv = {x: _mm_tn(k_end[x[:2]], vh[x])[x[2] * GLA_DK:(x[2] + 1) * GLA_DK, :] for x in units}

    for c in range(n_c):
        for b in range(bb):
            st = s_ref[b]
            for h in range(GLA_HEADS):
                rs = slice(h * GLA_DK, (h + 1) * GLA_DK)
                o_ref[b, c * chunk:(c + 1) * chunk, h * GLA_DV:(h + 1) * GLA_DV] = (
                    o_intra[b, c, h] + _mm(qh[b, c, h], st))
                s_ref[b, rs, :] = st[rs, :] * end_decay[b, c][rs, :] + kv[b, c, h]

    for b in range(bb):
        o = _group_rmsnorm(o_ref[b], GLA_DV) * nrm_ref[...]
        o_ref[b] = o * _silu(r_ref[b])


def _gla(proj, state, p, *, bb, rows, chunk, valid):
    b, length, _ = proj.shape
    hk = GLA_HEADS * GLA_DK
    vec = lambda n: pl.BlockSpec((1, n), lambda bi, i: (0, 0))
    return _mixer_call(
        functools.partial(_gla_kernel, bb=bb, rows=rows, chunk=chunk, valid=valid),
        grid=(b // bb, length // rows),
        in_specs=[
            pl.BlockSpec((bb, rows, hk), lambda bi, i: (bi, i, COL_GLA_Q // hk)),
            pl.BlockSpec((bb, rows, hk), lambda bi, i: (bi, i, COL_GLA_K // hk)),
            pl.BlockSpec((bb, rows, D_GROUP), lambda bi, i: (bi, i, COL_GLA_V // D_GROUP)),
            pl.BlockSpec((bb, rows, D_GROUP), lambda bi, i: (bi, i, COL_GLA_R // D_GROUP)),
            pl.BlockSpec((bb, rows, LANES), lambda bi, i: (bi, i, COL_SMALL // LANES)),
            state.in_spec(bb),
            pl.BlockSpec((LANES, hk), lambda bi, i: (0, 0)), vec(hk), vec(D_GROUP),
        ],
        inputs=[proj, proj, proj, proj, proj, state.init, p["gla_w_gate2"], p["gla_b_gate2"], p["gla_norm"]],
        out_specs=[pl.BlockSpec((bb, rows, D_GROUP), lambda bi, i: (bi, i, 0)), state.out_spec(bb)],
        out_shape=[jax.ShapeDtypeStruct((b, length, D_GROUP), F32), state.out_shape()],
        states=[state], first_state_out=1, scratch=[], name="gla")


def _gdn_kernel(qkv_ref, z_ref, sm_ref, conv0_ref, s0_ref, cw_ref, alog_ref, dtb_ref, nrm_ref, o_ref, s_ref,
                cbuf, abuf, *, bb, rows, chunk, valid):
    @pl.when(pl.program_id(1) == 0)
    def _():
        cbuf[:, 0:SUBLANES, :] = conv0_ref[...]
        s_ref[...] = s0_ref[...]

    a_neg = -jnp.exp(alog_ref[...])
    tril = _tril(chunk).astype(F32)
    causal = _tril(chunk)
    strict = _tril(chunk, strict=True)
    eye_c = _eye(chunk)
    gate_rows = _head_rows(SM_BETA)
    hd = GDN_HEADS * GDN_DK
    n_c = rows // chunk

    def l2n(x):
        return x * lax.rsqrt(jnp.sum(x * x, axis=1, keepdims=True) + EPS)

    beta_all, g_all = [], []
    for b in range(bb):
        abuf[b] = _silu(_causal_conv(qkv_ref[b], cbuf.at[b], cw_ref[...], rows))
        sm = sm_ref[b]
        beta = _sigmoid(sm)
        g = a_neg * _softplus(sm + dtb_ref[...])
        if valid < rows:
            live = _iota((rows, LANES), 0) < valid
            beta = jnp.where(live, beta, 0.0)
            g = jnp.where(live, g, 0.0)
        beta_all.append(beta)
        g_all.append(g)

    seqs = [(b, c) for b in range(bb) for c in range(n_c)]
    gc = {s: _mm_hi(tril, g_all[s[0]][s[1] * chunk:(s[1] + 1) * chunk, :]) for s in seqs}
    gc_t = {s: _mm_nt_hi(gate_rows, gc[s]) for s in seqs}

    units = [(b, c, h) for (b, c) in seqs for h in range(GDN_HEADS)]
    q, k, v, g_col, b_col, decay = {}, {}, {}, {}, {}, {}
    for (b, c, h) in units:
        r = slice(c * chunk, (c + 1) * chunk)
        q[b, c, h] = l2n(abuf[b, r, h * GDN_DK:(h + 1) * GDN_DK]) * (GDN_DK ** -0.5)
        k[b, c, h] = l2n(abuf[b, r, hd + h * GDN_DK:hd + (h + 1) * GDN_DK])
        v[b, c, h] = abuf[b, r, 2 * hd + h * GDN_DK:2 * hd + (h + 1) * GDN_DK]
        g_col[b, c, h] = gc[b, c][:, SM_DECAY + h:SM_DECAY + h + 1]
        g_row = gc_t[b, c][SM_DECAY - SM_BETA + h:SM_DECAY - SM_BETA + h + 1, :]
        b_col[b, c, h] = beta_all[b][r, SM_BETA + h:SM_BETA + h + 1]
        decay[b, c, h] = jnp.exp(jnp.where(causal, g_col[b, c, h] - g_row, -jnp.inf))
    kk = {x: _mm_nt(k[x], k[x]) for x in units}
    npow = {x: jnp.where(strict, -(b_col[x] * kk[x] * decay[x]), 0.0) for x in units}
    t_mat = {x: eye_c + npow[x] for x in units}
    span = 2
    parts = {x: _split(npow[x]) for x in units}
    while span < chunk:
        parts = {x: _split(_mm3(parts[x], parts[x])) for x in units}
        t_mat = {x: t_mat[x] + _mm3(_split(t_mat[x]), parts[x]) for x in units}
        span *= 2
    u = {x: _mm(t_mat[x], v[x] * b_col[x]) for x in units}
    w = {x: _mm(t_mat[x], k[x] * (b_col[x] * jnp.exp(g_col[x]))) for x in units}
    qk = {x: _mm_nt(q[x], k[x]) * decay[x] for x in units}

    for c in range(n_c):
        now = [(b, c, h) for b in range(bb) for h in range(GDN_HEADS)]
        st = {x: s_ref[x[0], x[2] * GDN_DK:(x[2] + 1) * GDN_DK, :] for x in now}
        v_new = {x: u[x] - _mm(w[x], st[x]) for x in now}
        for x in now:
            b, _, h = x
            g_last = gc[b, c][chunk - 1:chunk, SM_DECAY + h:SM_DECAY + h + 1]
            o_ref[b, c * chunk:(c + 1) * chunk, h * GDN_DK:(h + 1) * GDN_DK] = (
                _mm(q[x] * jnp.exp(g_col[x]), st[x]) + _mm(qk[x], v_new[x]))
            s_ref[b, h * GDN_DK:(h + 1) * GDN_DK, :] = (
                st[x] * jnp.exp(g_last) + _mm_tn(k[x] * jnp.exp(g_last - g_col[x]), v_new[x]))

    for b in range(bb):
        o = _group_rmsnorm(o_ref[b], GDN_DK) * nrm_ref[...]
        o_ref[b] = o * _silu(z_ref[b])


def _gdn(proj, conv0, state, p, *, bb, rows, chunk, valid):
    b, length, _ = proj.shape
    w_qkv = 3 * GDN_HEADS * GDN_DK
    vec = lambda n: pl.BlockSpec((1, n), lambda bi, i: (0, 0))
    return _mixer_call(
        functools.partial(_gdn_kernel, bb=bb, rows=rows, chunk=chunk, valid=valid),
        grid=(b // bb, length // rows),
        in_specs=[
            pl.BlockSpec((bb, rows, w_qkv), lambda bi, i: (bi, i, COL_GDN_QKV // w_qkv)),
            pl.BlockSpec((bb, rows, D_GROUP), lambda bi, i: (bi, i, COL_GDN_Z // D_GROUP)),
            pl.BlockSpec((bb, rows, LANES), lambda bi, i: (bi, i, COL_SMALL // LANES)),
            pl.BlockSpec((bb, SUBLANES, w_qkv), lambda bi, i: (bi, 0, 0)),
            state.in_spec(bb),
            pl.BlockSpec((CONV_W, w_qkv), lambda bi, i: (0, 0)),
            vec(LANES), vec(LANES), vec(D_GROUP),
        ],
        inputs=[proj, proj, proj, conv0, state.init, p["gdn_conv_w"], p["gdn_a_log"], p["gdn_dt_bias"], p["gdn_norm"]],
        out_specs=[pl.BlockSpec((bb, rows, D_GROUP), lambda bi, i: (bi, i, 0)), state.out_spec(bb)],
        out_shape=[jax.ShapeDtypeStruct((b, length, D_GROUP), F32), state.out_shape()],
        states=[state], first_state_out=1,
        scratch=[pltpu.VMEM((bb, rows + SUBLANES, w_qkv), F32), pltpu.VMEM((bb, rows, w_qkv), F32)],
        name="gdn")


def _lane_row(values, offset):
    return jnp.zeros((1, LANES), F32).at[0, offset:offset + values.shape[0]].set(values.astype(F32))


def _prep_layer(w, i, max_rows):
    row = lambda a: a[i].astype(F32).reshape(1, -1)
    p = {}
    for name in ("ffn1", "ffn2"):
        p[name + "_norm"] = row(w[name + "_norm"])
        p[name + "_w_gate"] = _cast_pad(w[name + "_w_gate"], i, D_MODEL, D_FF_PAD)
        p[name + "_w_up"] = _cast_pad(w[name + "_w_up"], i, D_MODEL, D_FF_PAD)
        p[name + "_w_down"] = _cast_pad(w[name + "_w_down"], i, D_FF_PAD, D_MODEL)
    p["mix_norm"] = row(w["mix_norm"])

    wi = w["w_in"][i]
    seg = lambda start, width: wi[:, start:start + width]
    small = jnp.concatenate([seg(1536, 8), seg(3080, 16), seg(5656, 4), seg(5660, 4),
                             jnp.zeros((D_MODEL, LANES - 32), wi.dtype)], axis=1)
    p["w_in"] = jnp.concatenate([
        seg(512, 1024), seg(0, 512), seg(3608, 1536), seg(1544, 512), seg(2568, 512), seg(3096, 512),
        seg(5144, 512), seg(2056, 256), seg(2312, 256), small], axis=1).astype(BF16)
    p["w_out"] = w["w_out"][i].astype(BF16)

    p["ssd_conv_w"] = w["ssd_conv_w"][i].astype(F32)
    p["ssd_conv_b"] = row(w["ssd_conv_b"])
    p["ssd_dt_bias"] = _lane_row(w["ssd_dt_bias"][i], SM_DT)
    p["ssd_a_log"] = _lane_row(w["ssd_a_log"][i], SM_DT)
    p["ssd_d"] = jnp.repeat(w["ssd_d"][i].astype(F32), SSD_HEAD_DIM).reshape(1, D_GROUP)
    p["ssd_norm"] = row(w["ssd_norm"])

    a_re, a_im = w["s5_a_re"][i].astype(F32), w["s5_a_im"][i].astype(F32)
    dt = jnp.exp(w["s5_log_dt"][i].astype(F32))[:, None]
    lam_re, lam_im = a_re * dt, a_im * dt

    def a_bar_pow(n):
        mag = jnp.exp(lam_re * n)
        return mag * jnp.cos(lam_im * n), mag * jnp.sin(lam_im * n)

    abar_re, abar_im = a_bar_pow(1.0)
    den = a_re * a_re + a_im * a_im
    coef_re = ((abar_re - 1.0) * a_re + abar_im * a_im) / den
    coef_im = (abar_im * a_re - (abar_re - 1.0) * a_im) / den
    b_re, b_im = w["s5_b_re"][i].astype(F32), w["s5_b_im"][i].astype(F32)
    bbar_re = coef_re[..., None] * b_re - coef_im[..., None] * b_im
    bbar_im = coef_re[..., None] * b_im + coef_im[..., None] * b_re
    steps = jnp.arange(1, max_rows + 1, dtype=F32)[:, None, None]
    apow_re, apow_im = a_bar_pow(steps)
    p["s5_apow_re"] = apow_re.reshape(max_rows, S5_LANES)
    p["s5_apow_im"] = apow_im.reshape(max_rows, S5_LANES)
    lag_re, lag_im = a_bar_pow(jnp.arange(S5_TAPS, dtype=F32)[:, None, None])
    n_blk = D_GROUP // LANES
    gpb = S5_GROUPS // n_blk
    eye = jnp.eye(gpb, dtype=F32)

    def b_blocks(x):
        x = jnp.swapaxes(x, 1, 2).reshape(n_blk, gpb, S5_CH, S5_STATE)
        return jnp.einsum("jgip,gh->jgihp", x, eye).reshape(n_blk, LANES, gpb * S5_STATE)

    def c_blocks(x):
        x = x.reshape(n_blk, gpb, S5_CH, S5_STATE)
        return jnp.einsum("jgip,gh->jgphi", x, eye).reshape(n_blk, gpb * S5_STATE, LANES).astype(BF16)

    p["s5_wlag_re"], p["s5_wlag_im"] = _s5_lag_weights(
        lag_re.reshape(S5_TAPS, S5_LANES), lag_im.reshape(S5_TAPS, S5_LANES), b_blocks(bbar_re), b_blocks(bbar_im))
    p["s5_cblk_re"] = c_blocks(w["s5_c_re"][i].astype(F32))
    p["s5_cblk_im"] = c_blocks(w["s5_c_im"][i].astype(F32))
    p["s5_d"] = row(w["s5_d"])
    p["s5_w_glu"] = w["s5_w_glu"][i].astype(BF16)
    p["s5_b_glu"] = row(w["s5_b_glu"])

    hk = GLA_HEADS * GLA_DK
    p["gla_w_gate2"] = jnp.zeros((LANES, hk), F32).at[SM_GR:SM_GR + GLA_GATE_RANK].set(w["gla_w_gate2"][i].astype(F32))
    p["gla_b_gate2"] = row(w["gla_b_gate2"])
    p["gla_norm"] = row(w["gla_norm"])

    p["gdn_conv_w"] = w["gdn_conv_w"][i].astype(F32)
    p["gdn_a_log"] = _lane_row(w["gdn_a_log"][i], SM_DECAY)
    p["gdn_dt_bias"] = _lane_row(w["gdn_dt_bias"][i], SM_DECAY)
    p["gdn_norm"] = row(w["gdn_norm"])
    return p


def _mixer_geometry(bsz, length):
    if length >= CHUNK:
        assert length % CHUNK == 0
        rows = 256 if length % 256 == 0 else CHUNK
        return 1, rows, CHUNK, length
    assert length <= SUBLANES
    bb = SUBLANES if bsz % SUBLANES == 0 else 1
    return bb, SUBLANES, SUBLANES, SUBLANES


def _conv_buffer(state):
    return jnp.pad(state.astype(F32), ((0, 0), (SUBLANES - (CONV_W - 1), 0), (0, 0)))


def _trunk(x, states, layers, final_norm):
    bsz, length, _ = x.shape
    depth = len(layers)
    bb, rows, chunk, lpad = _mixer_geometry(bsz, length)
    geo = dict(bb=bb, rows=rows, chunk=chunk, valid=min(length, rows))
    h = x.reshape(bsz * length, D_MODEL).astype(F32)
    fg = final_norm.astype(F32).reshape(1, D_MODEL)
    ssd_conv, ssd_h, s5_re, s5_im, gla_s, gdn_conv, gdn_s = [s.astype(F32) for s in states]
    lead = ssd_h.shape[0]
    recurrent = [ssd_h.reshape(lead, bsz, D_GROUP, SSD_STATE), s5_re.reshape(lead, bsz, 1, S5_LANES),
                 s5_im.reshape(lead, bsz, 1, S5_LANES), gla_s.reshape(lead, bsz, GLA_HEADS * GLA_DK, GLA_DV),
                 gdn_s.reshape(lead, bsz, GDN_HEADS * GDN_DK, GDN_DK)]
    new = [None] * len(recurrent)
    ssd_conv_new, gdn_conv_new = [], []
    for i, p in enumerate(layers):
        src = min(i, lead - 1)
        st = [_Layered(init, src, i, depth, prev) for init, prev in zip(recurrent, new)]
        h = _ffn(h, p["ffn1_norm"], p["ffn1_w_gate"], p["ffn1_w_up"], p["ffn1_w_down"], fg, False)
        proj = _in_proj(h, p["mix_norm"], p["w_in"]).reshape(bsz, length, PROJ_COLS)
        ssd_conv_new.append(proj[:, length - (CONV_W - 1):, COL_XBC:COL_XBC + 1024])
        gdn_conv_new.append(proj[:, length - (CONV_W - 1):, COL_GDN_QKV:COL_GDN_QKV + 1536])
        if lpad != length:
            proj = jnp.pad(proj, ((0, 0), (0, lpad - length), (0, 0)))
        y_ssd, new[0] = _ssd(proj, _conv_buffer(ssd_conv[src]), st[0], p, **geo)
        s5_bb = bsz if rows == S5_TAPS else bb
        y_s5, new[1], new[2] = _s5(proj, st[1], st[2], p, bb=s5_bb, rows=rows, valid=geo["valid"])
        y_gla, new[3] = _gla(proj, st[3], p, **geo)
        y_gdn, new[4] = _gdn(proj, _conv_buffer(gdn_conv[src]), st[4], p, **geo)
        ys = [y[:, :length].reshape(bsz * length, D_GROUP) for y in (y_ssd, y_s5, y_gla, y_gdn)]
        h = _out_proj(h, ys, p["w_out"])
        h = _ffn(h, p["ffn2_norm"], p["ffn2_w_gate"], p["ffn2_w_up"], p["ffn2_w_down"], fg, i == depth - 1)
    out_states = (jnp.stack(ssd_conv_new), new[0].reshape(depth, bsz, SSD_HEADS, SSD_HEAD_DIM, SSD_STATE),
                  new[1].reshape(depth, bsz, S5_GROUPS, S5_STATE), new[2].reshape(depth, bsz, S5_GROUPS, S5_STATE),
                  new[3].reshape(depth, bsz, GLA_HEADS, GLA_DK, GLA_DV), jnp.stack(gdn_conv_new),
                  new[4].reshape(depth, bsz, GDN_HEADS, GDN_DK, GDN_DK))
    return h.reshape(bsz, length, D_MODEL), out_states


def kernel(x_prompt, x_sample, state_ssd_conv, state_ssd, state_s5_re, state_s5_im, state_gla, state_gdn_conv, state_gdn, ffn1_norm, ffn1_w_gate, ffn1_w_up, ffn1_w_down, mix_norm, w_in, ssd_conv_w, ssd_conv_b, ssd_dt_bias, ssd_a_log, ssd_d, ssd_norm, s5_a_re, s5_a_im, s5_log_dt, s5_b_re, s5_b_im, s5_c_re, s5_c_im, s5_d, s5_w_glu, s5_b_glu, gla_w_gate2, gla_b_gate2, gla_norm, gdn_conv_w, gdn_a_log, gdn_dt_bias, gdn_norm, w_out, ffn2_norm, ffn2_w_gate, ffn2_w_up, ffn2_w_down, final_norm):
    w = dict(ffn1_norm=ffn1_norm, ffn1_w_gate=ffn1_w_gate, ffn1_w_up=ffn1_w_up, ffn1_w_down=ffn1_w_down,
             mix_norm=mix_norm, w_in=w_in, ssd_conv_w=ssd_conv_w, ssd_conv_b=ssd_conv_b, ssd_dt_bias=ssd_dt_bias,
             ssd_a_log=ssd_a_log, ssd_d=ssd_d, ssd_norm=ssd_norm, s5_a_re=s5_a_re, s5_a_im=s5_a_im,
             s5_log_dt=s5_log_dt, s5_b_re=s5_b_re, s5_b_im=s5_b_im, s5_c_re=s5_c_re, s5_c_im=s5_c_im, s5_d=s5_d,
             s5_w_glu=s5_w_glu, s5_b_glu=s5_b_glu, gla_w_gate2=gla_w_gate2, gla_b_gate2=gla_b_gate2,
             gla_norm=gla_norm, gdn_conv_w=gdn_conv_w, gdn_a_log=gdn_a_log, gdn_dt_bias=gdn_dt_bias,
             gdn_norm=gdn_norm, w_out=w_out, ffn2_norm=ffn2_norm, ffn2_w_gate=ffn2_w_gate, ffn2_w_up=ffn2_w_up,
             ffn2_w_down=ffn2_w_down)
    depth = w_in.shape[0]
    max_rows = max(_mixer_geometry(*x_prompt.shape[:2])[1], _mixer_geometry(*x_sample.shape[:2])[1])
    layers = [_prep_layer(w, i, max_rows) for i in range(depth)]

    sample_states = (state_ssd_conv, state_ssd, state_s5_re, state_s5_im, state_gla, state_gdn_conv, state_gdn)
    n_prompt = x_prompt.shape[0]
    prompt_states = tuple(jnp.zeros((1, n_prompt) + s.shape[2:], F32) for s in sample_states)
    y_prompt, p_states = _trunk(x_prompt, prompt_states, layers, final_norm)
    y_sample, s_states = _trunk(x_sample, sample_states, layers, final_norm)
    return (y_prompt, y_sample) + p_states + s_states
```

```python
import functools
import math

import numpy as np
import jax
import jax.numpy as jnp
from jax import lax
from jax.experimental import pallas as pl
from jax.experimental.pallas import tpu as pltpu

F32 = jnp.float32
BF16 = jnp.bfloat16
HIGHEST = lax.Precision.HIGHEST
EPS = 1e-6

LANES = 128
SUBLANES = 8
VMEM_LIMIT_BYTES = 56 * 1024 * 1024
FFN_VMEM_LIMIT_BYTES = 60 * 1024 * 1024

D_MODEL = 2048
D_GROUP = 512
D_FF = 5504
D_FF_PAD = 5632
CONV_W = 4
CHUNK = 64

SSD_HEADS = 8
SSD_HEAD_DIM = 64
SSD_GROUPS = 2
SSD_STATE = 128
S5_GROUPS = 32
S5_CH = 16
S5_STATE = 64
S5_LANES = S5_GROUPS * S5_STATE
S5_TAPS = 8
GLA_HEADS = 4
GLA_DK = 64
GLA_DV = 128
GLA_GATE_RANK = 16
GLA_GATE_TEMP = 16.0
GDN_HEADS = 4
GDN_DK = 128

COL_XBC = 0
COL_SSD_Z = 1024
COL_GDN_QKV = 1536
COL_S5_U = 3072
COL_GLA_V = 3584
COL_GLA_R = 4096
COL_GDN_Z = 4608
COL_GLA_Q = 5120
COL_GLA_K = 5376
COL_SMALL = 5632
PROJ_COLS = 5760
SM_DT = 0
SM_GR = 8
SM_BETA = 24
SM_DECAY = 28


def _mm(a, b):
    return jnp.dot(a.astype(BF16), b.astype(BF16), preferred_element_type=F32)


def _mm_nt(a, b):
    return lax.dot_general(a.astype(BF16), b.astype(BF16), (((1,), (1,)), ((), ())), preferred_element_type=F32)


def _mm_tn(a, b):
    return lax.dot_general(a.astype(BF16), b.astype(BF16), (((0,), (0,)), ((), ())), preferred_element_type=F32)


def _mm_hi(a, b):
    return jnp.dot(a, b, precision=HIGHEST, preferred_element_type=F32)


def _mm_nt_hi(a, b):
    return lax.dot_general(a, b, (((1,), (1,)), ((), ())), precision=HIGHEST, preferred_element_type=F32)


def _split(a):
    hi = a.astype(BF16)
    return hi, (a - hi.astype(F32)).astype(BF16)


def _mm3(a, b):
    (ah, al), (bh, bl) = a, b
    dot = functools.partial(jnp.dot, preferred_element_type=F32)
    return dot(ah, bh) + (dot(al, bh) + dot(ah, bl))


def _head_rows(first_lane):
    return (_iota((SUBLANES, LANES), 1) == _iota((SUBLANES, LANES), 0) + first_lane).astype(F32)


def _sigmoid(x):
    return 1.0 / (1.0 + jnp.exp(-x))


def _silu(x):
    return x * _sigmoid(x)


def _softplus(x):
    return jnp.maximum(x, 0.0) + jnp.log1p(jnp.exp(-jnp.abs(x)))


def _iota(shape, dim):
    return lax.broadcasted_iota(jnp.int32, shape, dim)


def _tril(n, strict=False):
    r, c = _iota((n, n), 0), _iota((n, n), 1)
    return (r > c) if strict else (r >= c)


def _eye(n):
    return (_iota((n, n), 0) == _iota((n, n), 1)).astype(F32)


def _row_to_col(row):
    n = row.shape[1]
    return jnp.sum(_eye(n) * row, axis=1, keepdims=True)


def _group_rmsnorm(y, width):
    parts = []
    for g in range(y.shape[1] // width):
        yg = y[:, g * width:(g + 1) * width]
        parts.append(yg * lax.rsqrt(jnp.mean(yg * yg, axis=1, keepdims=True) + EPS))
    return jnp.concatenate(parts, axis=1)


def _causal_conv(x, cbuf, w, rows):
    cbuf[SUBLANES:SUBLANES + rows, :] = x
    y = (w[0:1, :] * cbuf[5:5 + rows, :] + w[1:2, :] * cbuf[6:6 + rows, :]
         + w[2:3, :] * cbuf[7:7 + rows, :] + w[3:4, :] * x)
    cbuf[0:SUBLANES, :] = cbuf[rows:rows + SUBLANES, :]
    return y


def _params(*sem, vmem_limit_bytes=VMEM_LIMIT_BYTES):
    return pltpu.CompilerParams(dimension_semantics=sem, vmem_limit_bytes=vmem_limit_bytes)


class _Layered:
    def __init__(self, init, src, dst, depth, prev):
        self.init, self.src, self.dst, self.depth, self.prev = init, src, dst, depth, prev

    def in_spec(self, bb):
        tail = self.init.shape[2:]
        return pl.BlockSpec((None, bb) + tail, lambda bi, i, s=self.src, z=(0,) * len(tail): (s, bi) + z)

    def out_spec(self, bb):
        tail = self.init.shape[2:]
        return pl.BlockSpec((None, bb) + tail, lambda bi, i, d=self.dst, z=(0,) * len(tail): (d, bi) + z)

    def out_shape(self):
        return jax.ShapeDtypeStruct((self.depth,) + self.init.shape[1:], F32)


def _mixer_call(body, *, grid, in_specs, inputs, out_specs, out_shape, states, first_state_out, scratch, name):
    n_in = len(inputs)
    prevs = [s.prev for s in states if s.prev is not None]
    aliases = {}
    fn = body
    if prevs:
        assert len(prevs) == len(states)
        in_specs = list(in_specs) + [pl.BlockSpec(memory_space=pl.ANY)] * len(prevs)
        aliases = {n_in + j: first_state_out + j for j in range(len(prevs))}

        def fn(*refs):
            return body(*refs[:n_in], *refs[n_in + len(prevs):])

    return pl.pallas_call(
        fn, grid=grid, in_specs=in_specs, out_specs=out_specs, out_shape=out_shape, scratch_shapes=scratch,
        input_output_aliases=aliases, compiler_params=_params("arbitrary", "arbitrary"), name=name,
    )(*inputs, *prevs)


def _ffn_kernel(x_ref, g_ref, wg_ref, wu_ref, wd_ref, fg_ref, o_ref, xn_ref, *, n_f, tf, d_ff, final_norm):
    f = pl.program_id(1)

    @pl.when(f == 0)
    def _():
        x = x_ref[...]
        xn = x * lax.rsqrt(jnp.mean(x * x, axis=1, keepdims=True) + EPS) * g_ref[...]
        xn_ref[...] = xn.astype(BF16)
        o_ref[...] = jnp.zeros_like(o_ref)

    xn = xn_ref[...]
    gate = jnp.dot(xn, wg_ref[0].astype(BF16), preferred_element_type=F32)
    up = jnp.dot(xn, wu_ref[0].astype(BF16), preferred_element_type=F32)
    h = _silu(gate) * up
    if d_ff % tf:
        repeat = f * tf - jnp.minimum(f * tf, d_ff - tf)
        h = jnp.where(_iota((1, tf), 1) >= repeat, h, 0.0)
    o_ref[...] += jnp.dot(h.astype(BF16), wd_ref[0].astype(BF16), preferred_element_type=F32)

    @pl.when(f == n_f - 1)
    def _():
        y = x_ref[...] + 0.5 * o_ref[...]
        if final_norm:
            y = y * lax.rsqrt(jnp.mean(y * y, axis=1, keepdims=True) + EPS) * fg_ref[...]
        o_ref[...] = y


def _ffn(x, g, wg, wu, wd, layer, fg, final_norm):
    m = x.shape[0]
    d_ff = wg.shape[2]
    tm = min(1024, m)
    tf = 256
    n_f = pl.cdiv(d_ff, tf)
    assert d_ff % LANES == 0 and tf % LANES == 0
    start = lambda f: pl.multiple_of(jnp.minimum(f * tf, d_ff - tf), LANES)
    return pl.pallas_call(
        functools.partial(_ffn_kernel, n_f=n_f, tf=tf, d_ff=d_ff, final_norm=final_norm),
        grid=(m // tm, n_f),
        in_specs=[
            pl.BlockSpec((tm, D_MODEL), lambda i, f: (i, 0)),
            pl.BlockSpec((1, D_MODEL), lambda i, f: (0, 0)),
            pl.BlockSpec((pl.Element(1), pl.Element(D_MODEL), pl.Element(tf)), lambda i, f: (layer, 0, start(f))),
            pl.BlockSpec((pl.Element(1), pl.Element(D_MODEL), pl.Element(tf)), lambda i, f: (layer, 0, start(f))),
            pl.BlockSpec((pl.Element(1), pl.Element(tf), pl.Element(D_MODEL)), lambda i, f: (layer, start(f), 0)),
            pl.BlockSpec((1, D_MODEL), lambda i, f: (0, 0)),
        ],
        out_specs=pl.BlockSpec((tm, D_MODEL), lambda i, f: (i, 0)),
        out_shape=jax.ShapeDtypeStruct((m, D_MODEL), F32),
        scratch_shapes=[pltpu.VMEM((tm, D_MODEL), BF16)],
        compiler_params=_params("arbitrary", "arbitrary", vmem_limit_bytes=FFN_VMEM_LIMIT_BYTES),
        name="ffn",
    )(x, g, wg, wu, wd, fg)


def _in_proj_kernel(x_ref, g_ref, w_ref, o_ref, xn_ref):
    @pl.when(pl.program_id(1) == 0)
    def _():
        x = x_ref[...]
        xn = x * lax.rsqrt(jnp.mean(x * x, axis=1, keepdims=True) + EPS) * g_ref[...]
        xn_ref[...] = xn.astype(BF16)

    o_ref[...] = jnp.dot(xn_ref[...], w_ref[...], preferred_element_type=F32)


def _in_proj(x, g, w):
    m = x.shape[0]
    tm = min(1024, m)
    tn = 1920
    return pl.pallas_call(
        _in_proj_kernel,
        grid=(m // tm, PROJ_COLS // tn),
        in_specs=[
            pl.BlockSpec((tm, D_MODEL), lambda i, n: (i, 0)),
            pl.BlockSpec((1, D_MODEL), lambda i, n: (0, 0)),
            pl.BlockSpec((D_MODEL, tn), lambda i, n: (0, n)),
        ],
        out_specs=pl.BlockSpec((tm, tn), lambda i, n: (i, n)),
        out_shape=jax.ShapeDtypeStruct((m, PROJ_COLS), F32),
        scratch_shapes=[pltpu.VMEM((tm, D_MODEL), BF16)],
        compiler_params=_params("arbitrary", "arbitrary"),
        name="in_proj",
    )(x, g, w)


def _out_proj_kernel(x_ref, y0_ref, y1_ref, y2_ref, y3_ref, w_ref, o_ref):
    acc = x_ref[...]
    for j, y_ref in enumerate((y0_ref, y1_ref, y2_ref, y3_ref)):
        acc = acc + jnp.dot(y_ref[...].astype(BF16), w_ref[j * D_GROUP:(j + 1) * D_GROUP, :],
                            preferred_element_type=F32)
    o_ref[...] = acc


def _out_proj(x, ys, w):
    m = x.shape[0]
    tm = min(512, m)
    yspec = pl.BlockSpec((tm, D_GROUP), lambda i: (i, 0))
    return pl.pallas_call(
        _out_proj_kernel,
        grid=(m // tm,),
        in_specs=[pl.BlockSpec((tm, D_MODEL), lambda i: (i, 0)), yspec, yspec, yspec, yspec,
                  pl.BlockSpec((D_MODEL, D_MODEL), lambda i: (0, 0))],
        out_specs=pl.BlockSpec((tm, D_MODEL), lambda i: (i, 0)),
        out_shape=jax.ShapeDtypeStruct((m, D_MODEL), F32),
        compiler_params=_params("arbitrary"),
        name="out_proj",
    )(x, *ys, w)


W_IN_SEGMENTS = ((512, COL_XBC, 1024), (0, COL_SSD_Z, 512), (3608, COL_GDN_QKV, 1536), (1544, COL_S5_U, 512),
                 (2568, COL_GLA_V, 512), (3096, COL_GLA_R, 512), (5144, COL_GDN_Z, 512), (2056, COL_GLA_Q, 256),
                 (2312, COL_GLA_K, 256), (1536, COL_SMALL + SM_DT, 8), (3080, COL_SMALL + SM_GR, 16),
                 (5656, COL_SMALL + SM_BETA, 4), (5660, COL_SMALL + SM_DECAY, 4))


def _permute_w_in_kernel(w_ref, o_ref):
    o_ref[:, COL_SMALL:] = jnp.zeros((o_ref.shape[0], PROJ_COLS - COL_SMALL), BF16)
    for src, dst, width in W_IN_SEGMENTS:
        o_ref[:, dst:dst + width] = w_ref[:, src:src + width].astype(BF16)


def _permute_w_in(w, layer):
    _, rows, cols = w.shape
    tr = 256
    return pl.pallas_call(
        _permute_w_in_kernel,
        grid=(rows // tr,),
        in_specs=[pl.BlockSpec((None, tr, cols), lambda r: (layer, r, 0))],
        out_specs=pl.BlockSpec((tr, PROJ_COLS), lambda r: (r, 0)),
        out_shape=jax.ShapeDtypeStruct((rows, PROJ_COLS), BF16),
        compiler_params=_params("arbitrary"),
        name="permute_w_in",
    )(w)


def _ssd_kernel(xbc_ref, z_ref, sm_ref, conv0_ref, h0_ref, cw_ref, cb_ref, dtb_ref, alog_ref, dex_ref, nrm_ref,
                y_ref, h_ref, cbuf, abuf, *, bb, rows, chunk, valid):
    @pl.when(pl.program_id(1) == 0)
    def _():
        cbuf[:, 0:SUBLANES, :] = conv0_ref[...]
        h_ref[...] = h0_ref[...]

    a_neg = -jnp.exp(alog_ref[...])
    tril = _tril(chunk).astype(F32)
    causal = _tril(chunk)
    left = _iota((1, LANES), 1) < SSD_HEAD_DIM
    top = _iota((LANES, 1), 0) < SSD_HEAD_DIM
    head_rows = _head_rows(SM_DT)

    def pair(v, h0):
        return jnp.where(left, v[:, h0:h0 + 1], v[:, h0 + 1:h0 + 2])

    dt_all = []
    for b in range(bb):
        conv = _causal_conv(xbc_ref[b], cbuf.at[b], cw_ref[...], rows) + cb_ref[...]
        abuf[b] = _silu(conv)
        dt = _softplus(sm_ref[b] + dtb_ref[...])
        if valid < rows:
            dt = jnp.where(_iota((rows, LANES), 0) < valid, dt, 0.0)
        dt_all.append(dt)

    n_c = rows // chunk
    seqs = [(b, c) for b in range(bb) for c in range(n_c)]
    dt = {(b, c): dt_all[b][c * chunk:(c + 1) * chunk, :] for (b, c) in seqs}
    acs = {s: _mm_hi(tril, dt[s] * a_neg) for s in seqs}
    acs_t = {s: _mm_nt_hi(head_rows, acs[s]) for s in seqs}
    groups = [(b, c, g) for (b, c) in seqs for g in range(SSD_GROUPS)]
    bm, cm = {}, {}
    for (b, c, g) in groups:
        r = slice(c * chunk, (c + 1) * chunk)
        bm[b, c, g] = abuf[b, r, D_GROUP + g * SSD_STATE:D_GROUP + (g + 1) * SSD_STATE]
        cm[b, c, g] = abuf[b, r, D_GROUP + (SSD_GROUPS + g) * SSD_STATE:D_GROUP + (SSD_GROUPS + g + 1) * SSD_STATE]
    cb = {x: _mm_nt(cm[x], bm[x]) for x in groups}
    pairs = [(b, c, g, j) for (b, c, g) in groups for j in range(2)]
    xdt, y_diag = {}, {}
    for (b, c, g, j) in pairs:
        h0 = 4 * g + 2 * j
        lo = (2 * g + j) * LANES
        xdt[b, c, g, j] = abuf[b, c * chunk:(c + 1) * chunk, lo:lo + LANES] * pair(dt[b, c], h0)
    for (b, c, g, j) in pairs:
        h0 = 4 * g + 2 * j
        total = None
        for hh, keep in ((h0, left), (h0 + 1, jnp.logical_not(left))):
            diff = acs[b, c][:, hh:hh + 1] - acs_t[b, c][hh:hh + 1, :]
            decay = jnp.exp(jnp.where(causal, diff, -jnp.inf))
            part = _mm(cb[b, c, g] * decay, jnp.where(keep, xdt[b, c, g, j], 0.0))
            total = part if total is None else total + part
        y_diag[b, c, g, j] = total

    for c in range(n_c):
        r0 = c * chunk
        for b in range(bb):
            a_c = acs[b, c]
            exp_acs = jnp.exp(a_c)
            to_end = jnp.exp(a_c[chunk - 1:chunk, :] - a_c)
            end_decay = jnp.exp(a_c[chunk - 1:chunk, :])
            for g in range(SSD_GROUPS):
                for j in range(2):
                    h0 = 4 * g + 2 * j
                    lo = (2 * g + j) * LANES
                    st = h_ref[b, lo:lo + LANES, :]
                    y_ref[b, r0:r0 + chunk, lo:lo + LANES] = (
                        y_diag[b, c, g, j] + _mm_nt(cm[b, c, g], st) * pair(exp_acs, h0))
                    dec = jnp.where(top, end_decay[:, h0:h0 + 1], end_decay[:, h0 + 1:h0 + 2])
                    h_ref[b, lo:lo + LANES, :] = st * dec + _mm_tn(xdt[b, c, g, j] * pair(to_end, h0), bm[b, c, g])

    for b in range(bb):
        y = y_ref[b] + dex_ref[...] * abuf[b, :, 0:D_GROUP]
        y = y * _silu(z_ref[b])
        y_ref[b] = _group_rmsnorm(y, D_GROUP // SSD_GROUPS) * nrm_ref[...]


def _ssd(proj, conv0, state, p, *, bb, rows, chunk, valid):
    b, length, _ = proj.shape
    w_xbc = D_GROUP + 2 * SSD_GROUPS * SSD_STATE
    vec = lambda n: pl.BlockSpec((1, n), lambda bi, i: (0, 0))
    return _mixer_call(
        functools.partial(_ssd_kernel, bb=bb, rows=rows, chunk=chunk, valid=valid),
        grid=(b // bb, length // rows),
        in_specs=[
            pl.BlockSpec((bb, rows, w_xbc), lambda bi, i: (bi, i, COL_XBC // w_xbc)),
            pl.BlockSpec((bb, rows, D_GROUP), lambda bi, i: (bi, i, COL_SSD_Z // D_GROUP)),
            pl.BlockSpec((bb, rows, LANES), lambda bi, i: (bi, i, COL_SMALL // LANES)),
            pl.BlockSpec((bb, SUBLANES, w_xbc), lambda bi, i: (bi, 0, 0)),
            state.in_spec(bb),
            pl.BlockSpec((CONV_W, w_xbc), lambda bi, i: (0, 0)),
            vec(w_xbc), vec(LANES), vec(LANES), vec(D_GROUP), vec(D_GROUP),
        ],
        inputs=[proj, proj, proj, conv0, state.init, p["ssd_conv_w"], p["ssd_conv_b"], p["ssd_dt_bias"],
                p["ssd_a_log"], p["ssd_d"], p["ssd_norm"]],
        out_specs=[pl.BlockSpec((bb, rows, D_GROUP), lambda bi, i: (bi, i, 0)), state.out_spec(bb)],
        out_shape=[jax.ShapeDtypeStruct((b, length, D_GROUP), F32), state.out_shape()],
        states=[state], first_state_out=1,
        scratch=[pltpu.VMEM((bb, rows + SUBLANES, w_xbc), F32), pltpu.VMEM((bb, rows, w_xbc), F32)],
        name="ssd")


def _s5_kernel(u_ref, h0re_ref, h0im_ref, apre_ref, apim_ref, wre_ref, wim_ref, cre_ref, cim_ref, d_ref, wglu_ref,
               bglu_ref, y_ref, hre_ref, him_ref, ubuf, *carry, bb, rows, valid):
    first = pl.program_id(1) == 0
    m = bb * rows
    n_blk = D_GROUP // LANES
    w_blk = S5_LANES // n_blk
    n_tiles = rows // SUBLANES

    @pl.when(first)
    def _():
        ubuf[:, 0:SUBLANES, :] = jnp.zeros((bb, SUBLANES, D_GROUP), F32)
        for buf in carry[2:]:
            buf[...] = jnp.zeros(buf.shape, F32)

    ubuf[:, SUBLANES:SUBLANES + rows, :] = u_ref[...]

    parts = []
    for j in range(n_blk):
        sl = slice(j * w_blk, (j + 1) * w_blk)
        taps = [ubuf[:, SUBLANES - t:SUBLANES - t + rows, j * LANES:(j + 1) * LANES].reshape(m, LANES)
                for t in range(S5_TAPS)]
        lagged = jnp.concatenate(taps, axis=1).astype(BF16)
        xr = jnp.dot(lagged, wre_ref[j], preferred_element_type=F32)
        xi = jnp.dot(lagged, wim_ref[j], preferred_element_type=F32)
        if n_tiles > 1:
            hbuf_re, hbuf_im, tail_re, tail_im = carry
            ar, ai = apre_ref[S5_TAPS - 1:S5_TAPS, sl], apim_ref[S5_TAPS - 1:S5_TAPS, sl]
            pr, pi = tail_re[:, sl], tail_im[:, sl]
            for q in range(n_tiles):
                rs = slice(q * SUBLANES, (q + 1) * SUBLANES)
                pr, pi = xr[rs, :] + ar * pr - ai * pi, xi[rs, :] + ar * pi + ai * pr
                hbuf_re[rs, :] = pr
                hbuf_im[rs, :] = pi

            @pl.when(first)
            def _():
                h0r, h0i = h0re_ref[0, :, sl], h0im_ref[0, :, sl]
                apr, api = apre_ref[:, sl], apim_ref[:, sl]
                hbuf_re[...] = hbuf_re[...] + apr * h0r - api * h0i
                hbuf_im[...] = hbuf_im[...] + apr * h0i + api * h0r

            hr, hi = hbuf_re[...], hbuf_im[...]
            tail_re[:, sl] = hr[rows - SUBLANES:rows, :]
            tail_im[:, sl] = hi[rows - SUBLANES:rows, :]
            hre_ref[0, :, sl] = hr[valid - 1:valid, :]
            him_ref[0, :, sl] = hi[valid - 1:valid, :]
        else:
            h0r, h0i = h0re_ref[:, :, sl], h0im_ref[:, :, sl]
            apr, api = apre_ref[:, sl][None], apim_ref[:, sl][None]
            hr3 = xr.reshape(bb, rows, w_blk) + apr * h0r - api * h0i
            hi3 = xi.reshape(bb, rows, w_blk) + apr * h0i + api * h0r
            hre_ref[:, :, sl] = hr3[:, valid - 1:valid, :]
            him_ref[:, :, sl] = hi3[:, valid - 1:valid, :]
            hr, hi = hr3.reshape(m, w_blk), hi3.reshape(m, w_blk)
        parts.append(_mm(hr, cre_ref[j]) - _mm(hi, cim_ref[j]))

    if n_tiles > 1:
        ubuf[:, 0:SUBLANES, :] = ubuf[:, rows:rows + SUBLANES, :]

    y = jnp.concatenate(parts, axis=1) + d_ref[...] * u_ref[...].reshape(m, D_GROUP)
    y = 0.5 * y * (1.0 + jnp.tanh(math.sqrt(2.0 / math.pi) * (y + 0.044715 * (y * y * y))))
    y = y * _sigmoid(_mm(y, wglu_ref[...]) + bglu_ref[...])
    y_ref[...] = y.reshape(bb, rows, D_GROUP)


def _s5_lag_kernel(pre_ref, pim_ref, bre_ref, bim_ref, ore_ref, oim_ref):
    br, bi = bre_ref[...], bim_ref[...]
    for t in range(S5_TAPS):
        pr, pi = pre_ref[t:t + 1, :], pim_ref[t:t + 1, :]
        ore_ref[t * LANES:(t + 1) * LANES, :] = (pr * br - pi * bi).astype(BF16)
        oim_ref[t * LANES:(t + 1) * LANES, :] = (pr * bi + pi * br).astype(BF16)


def _s5_lag_weights(pow_re, pow_im, b_re, b_im):
    n_blk, _, w_blk = b_re.shape
    pw = pl.BlockSpec((S5_TAPS, w_blk), lambda j: (0, j))
    bs = pl.BlockSpec((None, LANES, w_blk), lambda j: (j, 0, 0))
    out = pl.BlockSpec((None, S5_TAPS * LANES, w_blk), lambda j: (j, 0, 0))
    shape = jax.ShapeDtypeStruct((n_blk, S5_TAPS * LANES, w_blk), BF16)
    return pl.pallas_call(
        _s5_lag_kernel, grid=(n_blk,), in_specs=[pw, pw, bs, bs], out_specs=[out, out], out_shape=[shape, shape],
        compiler_params=_params("arbitrary"), name="s5_lag_weights",
    )(pow_re, pow_im, b_re, b_im)


def _s5(proj, state_re, state_im, p, *, bb, rows, valid):
    b, length, _ = proj.shape
    assert rows == S5_TAPS or bb == 1
    n_blk = D_GROUP // LANES
    w_blk = S5_LANES // n_blk
    vec = lambda n: pl.BlockSpec((1, n), lambda bi, i: (0, 0))
    full3 = lambda s: pl.BlockSpec(s, lambda bi, i: (0, 0, 0))
    carry = []
    if rows > S5_TAPS:
        carry = [pltpu.VMEM((rows, w_blk), F32)] * 2 + [pltpu.VMEM((SUBLANES, S5_LANES), F32)] * 2
    return _mixer_call(
        functools.partial(_s5_kernel, bb=bb, rows=rows, valid=valid),
        grid=(b // bb, length // rows),
        in_specs=[
            pl.BlockSpec((bb, rows, D_GROUP), lambda bi, i: (bi, i, COL_S5_U // D_GROUP)),
            state_re.in_spec(bb), state_im.in_spec(bb),
            pl.BlockSpec((rows, S5_LANES), lambda bi, i: (0, 0)),
            pl.BlockSpec((rows, S5_LANES), lambda bi, i: (0, 0)),
            full3((n_blk, S5_TAPS * LANES, w_blk)), full3((n_blk, S5_TAPS * LANES, w_blk)),
            full3((n_blk, w_blk, LANES)), full3((n_blk, w_blk, LANES)),
            vec(D_GROUP), pl.BlockSpec((D_GROUP, D_GROUP), lambda bi, i: (0, 0)), vec(D_GROUP),
        ],
        inputs=[proj, state_re.init, state_im.init, p["s5_apow_re"][:rows], p["s5_apow_im"][:rows], p["s5_wlag_re"],
                p["s5_wlag_im"], p["s5_cblk_re"], p["s5_cblk_im"], p["s5_d"], p["s5_w_glu"], p["s5_b_glu"]],
        out_specs=[pl.BlockSpec((bb, rows, D_GROUP), lambda bi, i: (bi, i, 0)),
                   state_re.out_spec(bb), state_im.out_spec(bb)],
        out_shape=[jax.ShapeDtypeStruct((b, length, D_GROUP), F32), state_re.out_shape(), state_im.out_shape()],
        states=[state_re, state_im], first_state_out=1,
        scratch=[pltpu.VMEM((bb, SUBLANES + rows, D_GROUP), F32)] + carry,
        name="s5")


def _gla_kernel(q_ref, k_ref, v_ref, r_ref, sm_ref, s0_ref, wg_ref, bg_ref, nrm_ref, o_ref, s_ref,
                *, bb, rows, chunk, valid):
    @pl.when(pl.program_id(1) == 0)
    def _():
        s_ref[...] = s0_ref[...]

    hk = GLA_HEADS * GLA_DK
    tril = _tril(chunk).astype(F32)
    causal = _tril(chunk)
    lane = _iota((1, hk), 1)
    lg_all, k_all = [], []
    for b in range(bb):
        lg = -_softplus(-(_mm(sm_ref[b], wg_ref[...]) + bg_ref[...])) * (1.0 / GLA_GATE_TEMP)
        k = k_ref[b]
        if valid < rows:
            live = _iota((rows, hk), 0) < valid
            lg = jnp.where(live, lg, 0.0)
            k = jnp.where(live, k, 0.0)
        lg_all.append(lg)
        k_all.append(k)

    n_c = rows // chunk
    seqs = [(b, c) for b in range(bb) for c in range(n_c)]
    cum = {(b, c): _mm_hi(tril, lg_all[b][c * chunk:(c + 1) * chunk, :]) for (b, c) in seqs}
    q_dec, k_dec, k_end, end_decay = {}, {}, {}, {}
    for (b, c) in seqs:
        r = slice(c * chunk, (c + 1) * chunk)
        cs = cum[b, c]
        cum_last = cs[chunk - 1:chunk, :]
        q_dec[b, c] = q_ref[b, r, :] * (GLA_DK ** -0.5) * jnp.exp(cs)
        k = k_all[b][r, :]
        k_dec[b, c] = k * jnp.exp(-cs)
        k_end[b, c] = k * jnp.exp(cum_last - cs)
        end_decay[b, c] = _row_to_col(jnp.exp(cum_last))
    units = [(b, c, h) for (b, c) in seqs for h in range(GLA_HEADS)]
    qh, vh = {}, {}
    for (b, c, h) in units:
        mine = (lane >= h * GLA_DK) & (lane < (h + 1) * GLA_DK)
        qh[b, c, h] = jnp.where(mine, q_dec[b, c], 0.0)
        vh[b, c, h] = v_ref[b, c * chunk:(c + 1) * chunk, h * GLA_DV:(h + 1) * GLA_DV]
    att = {x: jnp.where(causal, _mm_nt(qh[x], k_dec[x[:2]]), 0.0) for x in units}
    o_intra = {x: _mm(att[x], vh[x]) for x in units}
    kv = {x: _mm_tn(k_end[x[:2]], vh[x])[x[2] * GLA_DK:(x[2] + 1) * GLA_DK, :] for x in units}

    for c in range(n_c):
        for b in range(bb):
            st = s_ref[b]
            for h in range(GLA_HEADS):
                rs = slice(h * GLA_DK, (h + 1) * GLA_DK)
                o_ref[b, c * chunk:(c + 1) * chunk, h * GLA_DV:(h + 1) * GLA_DV] = (
                    o_intra[b, c, h] + _mm(qh[b, c, h], st))
                s_ref[b, rs, :] = st[rs, :] * end_decay[b, c][rs, :] + kv[b, c, h]

    for b in range(bb):
        o = _group_rmsnorm(o_ref[b], GLA_DV) * nrm_ref[...]
        o_ref[b] = o * _silu(r_ref[b])


def _gla(proj, state, p, *, bb, rows, chunk, valid):
    b, length, _ = proj.shape
    hk = GLA_HEADS * GLA_DK
    vec = lambda n: pl.BlockSpec((1, n), lambda bi, i: (0, 0))
    return _mixer_call(
        functools.partial(_gla_kernel, bb=bb, rows=rows, chunk=chunk, valid=valid),
        grid=(b // bb, length // rows),
        in_specs=[
            pl.BlockSpec((bb, rows, hk), lambda bi, i: (bi, i, COL_GLA_Q // hk)),
            pl.BlockSpec((bb, rows, hk), lambda bi, i: (bi, i, COL_GLA_K // hk)),
            pl.BlockSpec((bb, rows, D_GROUP), lambda bi, i: (bi, i, COL_GLA_V // D_GROUP)),
            pl.BlockSpec((bb, rows, D_GROUP), lambda bi, i: (bi, i, COL_GLA_R // D_GROUP)),
            pl.BlockSpec((bb, rows, LANES), lambda bi, i: (bi, i, COL_SMALL // LANES)),
            state.in_spec(bb),
            pl.BlockSpec((LANES, hk), lambda bi, i: (0, 0)), vec(hk), vec(D_GROUP),
        ],
        inputs=[proj, proj, proj, proj, proj, state.init, p["gla_w_gate2"], p["gla_b_gate2"], p["gla_norm"]],
        out_specs=[pl.BlockSpec((bb, rows, D_GROUP), lambda bi, i: (bi, i, 0)), state.out_spec(bb)],
        out_shape=[jax.ShapeDtypeStruct((b, length, D_GROUP), F32), state.out_shape()],
        states=[state], first_state_out=1, scratch=[], name="gla")


def _gdn_kernel(qkv_ref, z_ref, sm_ref, conv0_ref, s0_ref, cw_ref, alog_ref, dtb_ref, nrm_ref, o_ref, s_ref,
                cbuf, abuf, *, bb, rows, chunk, valid):
    @pl.when(pl.program_id(1) == 0)
    def _():
        cbuf[:, 0:SUBLANES, :] = conv0_ref[...]
        s_ref[...] = s0_ref[...]

    a_neg = -jnp.exp(alog_ref[...])
    tril = _tril(chunk).astype(F32)
    causal = _tril(chunk)
    strict = _tril(chunk, strict=True)
    eye_c = _eye(chunk)
    gate_rows = _head_rows(SM_BETA)
    hd = GDN_HEADS * GDN_DK
    n_c = rows // chunk

    def l2n(x):
        return x * lax.rsqrt(jnp.sum(x * x, axis=1, keepdims=True) + EPS)

    beta_all, g_all = [], []
    for b in range(bb):
        abuf[b] = _silu(_causal_conv(qkv_ref[b], cbuf.at[b], cw_ref[...], rows))
        sm = sm_ref[b]
        beta = _sigmoid(sm)
        g = a_neg * _softplus(sm + dtb_ref[...])
        if valid < rows:
            live = _iota((rows, LANES), 0) < valid
            beta = jnp.where(live, beta, 0.0)
            g = jnp.where(live, g, 0.0)
        beta_all.append(beta)
        g_all.append(g)

    seqs = [(b, c) for b in range(bb) for c in range(n_c)]
    gc = {s: _mm_hi(tril, g_all[s[0]][s[1] * chunk:(s[1] + 1) * chunk, :]) for s in seqs}
    gc_t = {s: _mm_nt_hi(gate_rows, gc[s]) for s in seqs}

    units = [(b, c, h) for (b, c) in seqs for h in range(GDN_HEADS)]
    q, k, v, g_col, b_col, decay = {}, {}, {}, {}, {}, {}
    for (b, c, h) in units:
        r = slice(c * chunk, (c + 1) * chunk)
        q[b, c, h] = l2n(abuf[b, r, h * GDN_DK:(h + 1) * GDN_DK]) * (GDN_DK ** -0.5)
        k[b, c, h] = l2n(abuf[b, r, hd + h * GDN_DK:hd + (h + 1) * GDN_DK])
        v[b, c, h] = abuf[b, r, 2 * hd + h * GDN_DK:2 * hd + (h + 1) * GDN_DK]
        g_col[b, c, h] = gc[b, c][:, SM_DECAY + h:SM_DECAY + h + 1]
        g_row = gc_t[b, c][SM_DECAY - SM_BETA + h:SM_DECAY - SM_BETA + h + 1, :]
        b_col[b, c, h] = beta_all[b][r, SM_BETA + h:SM_BETA + h + 1]
        decay[b, c, h] = jnp.exp(jnp.where(causal, g_col[b, c, h] - g_row, -jnp.inf))
    kk = {x: _mm_nt(k[x], k[x]) for x in units}
    npow = {x: jnp.where(strict, -(b_col[x] * kk[x] * decay[x]), 0.0) for x in units}
    t_mat = {x: eye_c + npow[x] for x in units}
    span = 2
    parts = {x: _split(npow[x]) for x in units}
    while span < chunk:
        parts = {x: _split(_mm3(parts[x], parts[x])) for x in units}
        t_mat = {x: t_mat[x] + _mm3(_split(t_mat[x]), parts[x]) for x in units}
        span *= 2
    u = {x: _mm(t_mat[x], v[x] * b_col[x]) for x in units}
    w = {x: _mm(t_mat[x], k[x] * (b_col[x] * jnp.exp(g_col[x]))) for x in units}
    qk = {x: _mm_nt(q[x], k[x]) * decay[x] for x in units}

    for c in range(n_c):
        now = [(b, c, h) for b in range(bb) for h in range(GDN_HEADS)]
        st = {x: s_ref[x[0], x[2] * GDN_DK:(x[2] + 1) * GDN_DK, :] for x in now}
        v_new = {x: u[x] - _mm(w[x], st[x]) for x in now}
        for x in now:
            b, _, h = x
            g_last = gc[b, c][chunk - 1:chunk, SM_DECAY + h:SM_DECAY + h + 1]
            o_ref[b, c * chunk:(c + 1) * chunk, h * GDN_DK:(h + 1) * GDN_DK] = (
                _mm(q[x] * jnp.exp(g_col[x]), st[x]) + _mm(qk[x], v_new[x]))
            s_ref[b, h * GDN_DK:(h + 1) * GDN_DK, :] = (
                st[x] * jnp.exp(g_last) + _mm_tn(k[x] * jnp.exp(g_last - g_col[x]), v_new[x]))

    for b in range(bb):
        o = _group_rmsnorm(o_ref[b], GDN_DK) * nrm_ref[...]
        o_ref[b] = o * _silu(z_ref[b])


def _gdn(proj, conv0, state, p, *, bb, rows, chunk, valid):
    b, length, _ = proj.shape
    w_qkv = 3 * GDN_HEADS * GDN_DK
    vec = lambda n: pl.BlockSpec((1, n), lambda bi, i: (0, 0))
    return _mixer_call(
        functools.partial(_gdn_kernel, bb=bb, rows=rows, chunk=chunk, valid=valid),
        grid=(b // bb, length // rows),
        in_specs=[
            pl.BlockSpec((bb, rows, w_qkv), lambda bi, i: (bi, i, COL_GDN_QKV // w_qkv)),
            pl.BlockSpec((bb, rows, D_GROUP), lambda bi, i: (bi, i, COL_GDN_Z // D_GROUP)),
            pl.BlockSpec((bb, rows, LANES), lambda bi, i: (bi, i, COL_SMALL // LANES)),
            pl.BlockSpec((bb, SUBLANES, w_qkv), lambda bi, i: (bi, 0, 0)),
            state.in_spec(bb),
            pl.BlockSpec((CONV_W, w_qkv), lambda bi, i: (0, 0)),
            vec(LANES), vec(LANES), vec(D_GROUP),
        ],
        inputs=[proj, proj, proj, conv0, state.init, p["gdn_conv_w"], p["gdn_a_log"], p["gdn_dt_bias"], p["gdn_norm"]],
        out_specs=[pl.BlockSpec((bb, rows, D_GROUP), lambda bi, i: (bi, i, 0)), state.out_spec(bb)],
        out_shape=[jax.ShapeDtypeStruct((b, length, D_GROUP), F32), state.out_shape()],
        states=[state], first_state_out=1,
        scratch=[pltpu.VMEM((bb, rows + SUBLANES, w_qkv), F32), pltpu.VMEM((bb, rows, w_qkv), F32)],
        name="gdn")


def _lane_row(values, offset):
    return jnp.zeros((1, LANES), F32).at[0, offset:offset + values.shape[0]].set(values.astype(F32))


def _prep_layer(w, i, max_rows):
    row = lambda a: a[i].astype(F32).reshape(1, -1)
    p = {}
    for name in ("ffn1", "ffn2"):
        p[name + "_norm"] = row(w[name + "_norm"])
        for part in ("_w_gate", "_w_up", "_w_down"):
            p[name + part] = w[name + part].astype(F32)
    p["mix_norm"] = row(w["mix_norm"])

    p["w_in"] = _permute_w_in(w["w_in"].astype(F32), i)
    p["w_out"] = w["w_out"][i].astype(BF16)

    p["ssd_conv_w"] = w["ssd_conv_w"][i].astype(F32)
    p["ssd_conv_b"] = row(w["ssd_conv_b"])
    p["ssd_dt_bias"] = _lane_row(w["ssd_dt_bias"][i], SM_DT)
    p["ssd_a_log"] = _lane_row(w["ssd_a_log"][i], SM_DT)
    p["ssd_d"] = jnp.repeat(w["ssd_d"][i].astype(F32), SSD_HEAD_DIM).reshape(1, D_GROUP)
    p["ssd_norm"] = row(w["ssd_norm"])

    a_re, a_im = w["s5_a_re"][i].astype(F32), w["s5_a_im"][i].astype(F32)
    dt = jnp.exp(w["s5_log_dt"][i].astype(F32))[:, None]
    lam_re, lam_im = a_re * dt, a_im * dt

    def a_bar_pow(n):
        mag = jnp.exp(lam_re * n)
        return mag * jnp.cos(lam_im * n), mag * jnp.sin(lam_im * n)

    abar_re, abar_im = a_bar_pow(1.0)
    den = a_re * a_re + a_im * a_im
    coef_re = ((abar_re - 1.0) * a_re + abar_im * a_im) / den
    coef_im = (abar_im * a_re - (abar_re - 1.0) * a_im) / den
    b_re, b_im = w["s5_b_re"][i].astype(F32), w["s5_b_im"][i].astype(F32)
    bbar_re = coef_re[..., None] * b_re - coef_im[..., None] * b_im
    bbar_im = coef_re[..., None] * b_im + coef_im[..., None] * b_re
    steps = jnp.arange(1, max_rows + 1, dtype=F32)[:, None, None]
    apow_re, apow_im = a_bar_pow(steps)
    p["s5_apow_re"] = apow_re.reshape(max_rows, S5_LANES)
    p["s5_apow_im"] = apow_im.reshape(max_rows, S5_LANES)
    lag_re, lag_im = a_bar_pow(jnp.arange(S5_TAPS, dtype=F32)[:, None, None])
    n_blk = D_GROUP // LANES
    gpb = S5_GROUPS // n_blk
    eye = jnp.eye(gpb, dtype=F32)

    def b_blocks(x):
        x = jnp.swapaxes(x, 1, 2).reshape(n_blk, gpb, S5_CH, S5_STATE)
        return jnp.einsum("jgip,gh->jgihp", x, eye).reshape(n_blk, LANES, gpb * S5_STATE)

    def c_blocks(x):
        x = x.reshape(n_blk, gpb, S5_CH, S5_STATE)
        return jnp.einsum("jgip,gh->jgphi", x, eye).reshape(n_blk, gpb * S5_STATE, LANES).astype(BF16)

    p["s5_wlag_re"], p["s5_wlag_im"] = _s5_lag_weights(
        lag_re.reshape(S5_TAPS, S5_LANES), lag_im.reshape(S5_TAPS, S5_LANES), b_blocks(bbar_re), b_blocks(bbar_im))
    p["s5_cblk_re"] = c_blocks(w["s5_c_re"][i].astype(F32))
    p["s5_cblk_im"] = c_blocks(w["s5_c_im"][i].astype(F32))
    p["s5_d"] = row(w["s5_d"])
    p["s5_w_glu"] = w["s5_w_glu"][i].astype(BF16)
    p["s5_b_glu"] = row(w["s5_b_glu"])

    hk = GLA_HEADS * GLA_DK
    p["gla_w_gate2"] = jnp.zeros((LANES, hk), F32).at[SM_GR:SM_GR + GLA_GATE_RANK].set(w["gla_w_gate2"][i].astype(F32))
    p["gla_b_gate2"] = row(w["gla_b_gate2"])
    p["gla_norm"] = row(w["gla_norm"])

    p["gdn_conv_w"] = w["gdn_conv_w"][i].astype(F32)
    p["gdn_a_log"] = _lane_row(w["gdn_a_log"][i], SM_DECAY)
    p["gdn_dt_bias"] = _lane_row(w["gdn_dt_bias"][i], SM_DECAY)
    p["gdn_norm"] = row(w["gdn_norm"])
    return p


def _mixer_geometry(bsz, length):
    if length >= CHUNK:
        assert length % CHUNK == 0
        rows = 256 if length % 256 == 0 else CHUNK
        return 1, rows, CHUNK, length
    assert length <= SUBLANES
    bb = SUBLANES if bsz % SUBLANES == 0 else 1
    return bb, SUBLANES, SUBLANES, SUBLANES


def _conv_buffer(state):
    return jnp.pad(state.astype(F32), ((0, 0), (SUBLANES - (CONV_W - 1), 0), (0, 0)))


def _trunk(x, states, layers, final_norm):
    bsz, length, _ = x.shape
    depth = len(layers)
    bb, rows, chunk, lpad = _mixer_geometry(bsz, length)
    geo = dict(bb=bb, rows=rows, chunk=chunk, valid=min(length, rows))
    h = x.reshape(bsz * length, D_MODEL).astype(F32)
    fg = final_norm.astype(F32).reshape(1, D_MODEL)
    ssd_conv, ssd_h, s5_re, s5_im, gla_s, gdn_conv, gdn_s = [s.astype(F32) for s in states]
    lead = ssd_h.shape[0]
    recurrent = [ssd_h.reshape(lead, bsz, D_GROUP, SSD_STATE), s5_re.reshape(lead, bsz, 1, S5_LANES),
                 s5_im.reshape(lead, bsz, 1, S5_LANES), gla_s.reshape(lead, bsz, GLA_HEADS * GLA_DK, GLA_DV),
                 gdn_s.reshape(lead, bsz, GDN_HEADS * GDN_DK, GDN_DK)]
    new = [None] * len(recurrent)
    ssd_conv_new, gdn_conv_new = [], []
    for i, p in enumerate(layers):
        src = min(i, lead - 1)
        st = [_Layered(init, src, i, depth, prev) for init, prev in zip(recurrent, new)]
        h = _ffn(h, p["ffn1_norm"], p["ffn1_w_gate"], p["ffn1_w_up"], p["ffn1_w_down"], i, fg, False)
        proj = _in_proj(h, p["mix_norm"], p["w_in"]).reshape(bsz, length, PROJ_COLS)
        ssd_conv_new.append(proj[:, length - (CONV_W - 1):, COL_XBC:COL_XBC + 1024])
        gdn_conv_new.append(proj[:, length - (CONV_W - 1):, COL_GDN_QKV:COL_GDN_QKV + 1536])
        if lpad != length:
            proj = jnp.pad(proj, ((0, 0), (0, lpad - length), (0, 0)))
        y_ssd, new[0] = _ssd(proj, _conv_buffer(ssd_conv[src]), st[0], p, **geo)
        s5_bb = bsz if rows == S5_TAPS else bb
        y_s5, new[1], new[2] = _s5(proj, st[1], st[2], p, bb=s5_bb, rows=rows, valid=geo["valid"])
        y_gla, new[3] = _gla(proj, st[3], p, **geo)
        y_gdn, new[4] = _gdn(proj, _conv_buffer(gdn_conv[src]), st[4], p, **geo)
        ys = [y[:, :length].reshape(bsz * length, D_GROUP) for y in (y_ssd, y_s5, y_gla, y_gdn)]
        h = _out_proj(h, ys, p["w_out"])
        h = _ffn(h, p["ffn2_norm"], p["ffn2_w_gate"], p["ffn2_w_up"], p["ffn2_w_down"], i, fg, i == depth - 1)
    out_states = (jnp.stack(ssd_conv_new), new[0].reshape(depth, bsz, SSD_HEADS, SSD_HEAD_DIM, SSD_STATE),
                  new[1].reshape(depth, bsz, S5_GROUPS, S5_STATE), new[2].reshape(depth, bsz, S5_GROUPS, S5_STATE),
                  new[3].reshape(depth, bsz, GLA_HEADS, GLA_DK, GLA_DV), jnp.stack(gdn_conv_new),
                  new[4].reshape(depth, bsz, GDN_HEADS, GDN_DK, GDN_DK))
    return h.reshape(bsz, length, D_MODEL), out_states


def kernel(x_prompt, x_sample, state_ssd_conv, state_ssd, state_s5_re, state_s5_im, state_gla, state_gdn_conv, state_gdn, ffn1_norm, ffn1_w_gate, ffn1_w_up, ffn1_w_down, mix_norm, w_in, ssd_conv_w, ssd_conv_b, ssd_dt_bias, ssd_a_log, ssd_d, ssd_norm, s5_a_re, s5_a_im, s5_log_dt, s5_b_re, s5_b_im, s5_c_re, s5_c_im, s5_d, s5_w_glu, s5_b_glu, gla_w_gate2, gla_b_gate2, gla_norm, gdn_conv_w, gdn_a_log, gdn_dt_bias, gdn_norm, w_out, ffn2_norm, ffn2_w_gate, ffn2_w_up, ffn2_w_down, final_norm):
    w = dict(ffn1_norm=ffn1_norm, ffn1_w_gate=ffn1_w_gate, ffn1_w_up=ffn1_w_up, ffn1_w_down=ffn1_w_down,
             mix_norm=mix_norm, w_in=w_in, ssd_conv_w=ssd_conv_w, ssd_conv_b=ssd_conv_b, ssd_dt_bias=ssd_dt_bias,
             ssd_a_log=ssd_a_log, ssd_d=ssd_d, ssd_norm=ssd_norm, s5_a_re=s5_a_re, s5_a_im=s5_a_im,
             s5_log_dt=s5_log_dt, s5_b_re=s5_b_re, s5_b_im=s5_b_im, s5_c_re=s5_c_re, s5_c_im=s5_c_im, s5_d=s5_d,
             s5_w_glu=s5_w_glu, s5_b_glu=s5_b_glu, gla_w_gate2=gla_w_gate2, gla_b_gate2=gla_b_gate2,
             gla_norm=gla_norm, gdn_conv_w=gdn_conv_w, gdn_a_log=gdn_a_log, gdn_dt_bias=gdn_dt_bias,
             gdn_norm=gdn_norm, w_out=w_out, ffn2_norm=ffn2_norm, ffn2_w_gate=ffn2_w_gate, ffn2_w_up=ffn2_w_up,
             ffn2_w_down=ffn2_w_down)
    depth = w_in.shape[0]
    max_rows = max(_mixer_geometry(*x_prompt.shape[:2])[1], _mixer_geometry(*x_sample.shape[:2])[1])
    layers = [_prep_layer(w, i, max_rows) for i in range(depth)]

    sample_states = (state_ssd_conv, state_ssd, state_s5_re, state_s5_im, state_gla, state_gdn_conv, state_gdn)
    n_prompt = x_prompt.shape[0]
    prompt_states = tuple(jnp.zeros((1, n_prompt) + s.shape[2:], F32) for s in sample_states)
    y_prompt, p_states = _trunk(x_prompt, prompt_states, layers, final_norm)
    y_sample, s_states = _trunk(x_sample, sample_states, layers, final_norm)
    return (y_prompt, y_sample) + p_states + s_states
```

```python
import functools
import math

import numpy as np
import jax
import jax.numpy as jnp
from jax import lax
from jax.experimental import pallas as pl
from jax.experimental.pallas import tpu as pltpu

F32 = jnp.float32
BF16 = jnp.bfloat16
HIGHEST = lax.Precision.HIGHEST
EPS = 1e-6

LANES = 128
SUBLANES = 8
VMEM_LIMIT_BYTES = 56 * 1024 * 1024
FFN_VMEM_LIMIT_BYTES = 60 * 1024 * 1024

D_MODEL = 2048
D_GROUP = 512
D_FF = 5504
D_FF_PAD = 5632
CONV_W = 4
CHUNK = 64

SSD_HEADS = 8
SSD_HEAD_DIM = 64
SSD_GROUPS = 2
SSD_STATE = 128
S5_GROUPS = 32
S5_CH = 16
S5_STATE = 64
S5_LANES = S5_GROUPS * S5_STATE
S5_TAPS = 8
GLA_HEADS = 4
GLA_DK = 64
GLA_DV = 128
GLA_GATE_RANK = 16
GLA_GATE_TEMP = 16.0
GDN_HEADS = 4
GDN_DK = 128

COL_XBC = 0
COL_SSD_Z = 1024
COL_GDN_QKV = 1536
COL_S5_U = 3072
COL_GLA_V = 3584
COL_GLA_R = 4096
COL_GDN_Z = 4608
COL_GLA_Q = 5120
COL_GLA_K = 5376
COL_SMALL = 5632
PROJ_COLS = 5760
SM_DT = 0
SM_GR = 8
SM_BETA = 24
SM_DECAY = 28


def _mm(a, b):
    return jnp.dot(a.astype(BF16), b.astype(BF16), preferred_element_type=F32)


def _mm_nt(a, b):
    return lax.dot_general(a.astype(BF16), b.astype(BF16), (((1,), (1,)), ((), ())), preferred_element_type=F32)


def _mm_tn(a, b):
    return lax.dot_general(a.astype(BF16), b.astype(BF16), (((0,), (0,)), ((), ())), preferred_element_type=F32)


def _mm_hi(a, b):
    return jnp.dot(a, b, precision=HIGHEST, preferred_element_type=F32)


def _mm_nt_hi(a, b):
    return lax.dot_general(a, b, (((1,), (1,)), ((), ())), precision=HIGHEST, preferred_element_type=F32)


def _split(a):
    hi = a.astype(BF16)
    return hi, (a - hi.astype(F32)).astype(BF16)


def _mm3(a, b):
    (ah, al), (bh, bl) = a, b
    dot = functools.partial(jnp.dot, preferred_element_type=F32)
    return dot(ah, bh) + (dot(al, bh) + dot(ah, bl))


def _head_rows(first_lane):
    return (_iota((SUBLANES, LANES), 1) == _iota((SUBLANES, LANES), 0) + first_lane).astype(F32)


def _sigmoid(x):
    return 1.0 / (1.0 + jnp.exp(-x))


def _silu(x):
    return x * _sigmoid(x)


def _softplus(x):
    return jnp.maximum(x, 0.0) + jnp.log1p(jnp.exp(-jnp.abs(x)))


def _iota(shape, dim):
    return lax.broadcasted_iota(jnp.int32, shape, dim)


def _tril(n, strict=False):
    r, c = _iota((n, n), 0), _iota((n, n), 1)
    return (r > c) if strict else (r >= c)


def _eye(n):
    return (_iota((n, n), 0) == _iota((n, n), 1)).astype(F32)


def _row_to_col(row):
    n = row.shape[1]
    return jnp.sum(_eye(n) * row, axis=1, keepdims=True)


def _group_rmsnorm(y, width):
    parts = []
    for g in range(y.shape[1] // width):
        yg = y[:, g * width:(g + 1) * width]
        parts.append(yg * lax.rsqrt(jnp.mean(yg * yg, axis=1, keepdims=True) + EPS))
    return jnp.concatenate(parts, axis=1)


def _causal_conv(x, cbuf, w, rows):
    cbuf[SUBLANES:SUBLANES + rows, :] = x
    y = (w[0:1, :] * cbuf[5:5 + rows, :] + w[1:2, :] * cbuf[6:6 + rows, :]
         + w[2:3, :] * cbuf[7:7 + rows, :] + w[3:4, :] * x)
    cbuf[0:SUBLANES, :] = cbuf[rows:rows + SUBLANES, :]
    return y


def _params(*sem, vmem_limit_bytes=VMEM_LIMIT_BYTES):
    return pltpu.CompilerParams(dimension_semantics=sem, vmem_limit_bytes=vmem_limit_bytes)


class _Layered:
    def __init__(self, init, src, dst, depth, prev):
        self.init, self.src, self.dst, self.depth, self.prev = init, src, dst, depth, prev

    def in_spec(self, bb):
        tail = self.init.shape[2:]
        return pl.BlockSpec((None, bb) + tail, lambda bi, i, s=self.src, z=(0,) * len(tail): (s, bi) + z)

    def out_spec(self, bb):
        tail = self.init.shape[2:]
        return pl.BlockSpec((None, bb) + tail, lambda bi, i, d=self.dst, z=(0,) * len(tail): (d, bi) + z)

    def out_shape(self):
        return jax.ShapeDtypeStruct((self.depth,) + self.init.shape[1:], F32)


def _mixer_call(body, *, grid, in_specs, inputs, out_specs, out_shape, states, first_state_out, scratch, name):
    n_in = len(inputs)
    prevs = [s.prev for s in states if s.prev is not None]
    aliases = {}
    fn = body
    if prevs:
        assert len(prevs) == len(states)
        in_specs = list(in_specs) + [pl.BlockSpec(memory_space=pl.ANY)] * len(prevs)
        aliases = {n_in + j: first_state_out + j for j in range(len(prevs))}

        def fn(*refs):
            return body(*refs[:n_in], *refs[n_in + len(prevs):])

    return pl.pallas_call(
        fn, grid=grid, in_specs=in_specs, out_specs=out_specs, out_shape=out_shape, scratch_shapes=scratch,
        input_output_aliases=aliases, compiler_params=_params("arbitrary", "arbitrary"), name=name,
    )(*inputs, *prevs)


def _ffn_kernel(x_ref, g_ref, wg_ref, wu_ref, wd_ref, fg_ref, o_ref, xn_ref, *, n_f, tf, d_ff, final_norm):
    f = pl.program_id(1)

    @pl.when(f == 0)
    def _():
        x = x_ref[...]
        xn = x * lax.rsqrt(jnp.mean(x * x, axis=1, keepdims=True) + EPS) * g_ref[...]
        xn_ref[...] = xn.astype(BF16)
        o_ref[...] = jnp.zeros_like(o_ref)

    xn = xn_ref[...]
    gate = jnp.dot(xn, wg_ref[0].astype(BF16), preferred_element_type=F32)
    up = jnp.dot(xn, wu_ref[0].astype(BF16), preferred_element_type=F32)
    h = _silu(gate) * up
    if d_ff % tf:
        repeat = f * tf - jnp.minimum(f * tf, d_ff - tf)
        h = jnp.where(_iota((1, tf), 1) >= repeat, h, 0.0)
    o_ref[...] += jnp.dot(h.astype(BF16), wd_ref[0].astype(BF16), preferred_element_type=F32)

    @pl.when(f == n_f - 1)
    def _():
        y = x_ref[...] + 0.5 * o_ref[...]
        if final_norm:
            y = y * lax.rsqrt(jnp.mean(y * y, axis=1, keepdims=True) + EPS) * fg_ref[...]
        o_ref[...] = y


def _ffn(x, g, wg, wu, wd, layer, fg, final_norm):
    m = x.shape[0]
    d_ff = wg.shape[2]
    tm = min(1024, m)
    tf = 256
    n_f = pl.cdiv(d_ff, tf)
    assert d_ff % LANES == 0 and tf % LANES == 0
    start = lambda f: pl.multiple_of(jnp.minimum(f * tf, d_ff - tf), LANES)
    return pl.pallas_call(
        functools.partial(_ffn_kernel, n_f=n_f, tf=tf, d_ff=d_ff, final_norm=final_norm),
        grid=(m // tm, n_f),
        in_specs=[
            pl.BlockSpec((tm, D_MODEL), lambda i, f: (i, 0)),
            pl.BlockSpec((1, D_MODEL), lambda i, f: (0, 0)),
            pl.BlockSpec((pl.Element(1), pl.Element(D_MODEL), pl.Element(tf)), lambda i, f: (layer, 0, start(f))),
            pl.BlockSpec((pl.Element(1), pl.Element(D_MODEL), pl.Element(tf)), lambda i, f: (layer, 0, start(f))),
            pl.BlockSpec((pl.Element(1), pl.Element(tf), pl.Element(D_MODEL)), lambda i, f: (layer, start(f), 0)),
            pl.BlockSpec((1, D_MODEL), lambda i, f: (0, 0)),
        ],
        out_specs=pl.BlockSpec((tm, D_MODEL), lambda i, f: (i, 0)),
        out_shape=jax.ShapeDtypeStruct((m, D_MODEL), F32),
        scratch_shapes=[pltpu.VMEM((tm, D_MODEL), BF16)],
        compiler_params=_params("arbitrary", "arbitrary", vmem_limit_bytes=FFN_VMEM_LIMIT_BYTES),
        name="ffn",
    )(x, g, wg, wu, wd, fg)


def _in_proj_kernel(x_ref, g_ref, w_ref, o_ref, xn_ref):
    @pl.when(pl.program_id(1) == 0)
    def _():
        x = x_ref[...]
        xn = x * lax.rsqrt(jnp.mean(x * x, axis=1, keepdims=True) + EPS) * g_ref[...]
        xn_ref[...] = xn.astype(BF16)

    o_ref[...] = jnp.dot(xn_ref[...], w_ref[...], preferred_element_type=F32)


def _in_proj(x, g, w):
    m = x.shape[0]
    tm = min(1024, m)
    tn = 1920
    return pl.pallas_call(
        _in_proj_kernel,
        grid=(m // tm, PROJ_COLS // tn),
        in_specs=[
            pl.BlockSpec((tm, D_MODEL), lambda i, n: (i, 0)),
            pl.BlockSpec((1, D_MODEL), lambda i, n: (0, 0)),
            pl.BlockSpec((D_MODEL, tn), lambda i, n: (0, n)),
        ],
        out_specs=pl.BlockSpec((tm, tn), lambda i, n: (i, n)),
        out_shape=jax.ShapeDtypeStruct((m, PROJ_COLS), F32),
        scratch_shapes=[pltpu.VMEM((tm, D_MODEL), BF16)],
        compiler_params=_params("arbitrary", "arbitrary"),
        name="in_proj",
    )(x, g, w)


def _out_proj_kernel(x_ref, y0_ref, y1_ref, y2_ref, y3_ref, w_ref, o_ref):
    acc = x_ref[...]
    for j, y_ref in enumerate((y0_ref, y1_ref, y2_ref, y3_ref)):
        acc = acc + jnp.dot(y_ref[...].astype(BF16), w_ref[j * D_GROUP:(j + 1) * D_GROUP, :],
                            preferred_element_type=F32)
    o_ref[...] = acc


def _out_proj(x, ys, w):
    m = x.shape[0]
    tm = min(512, m)
    yspec = pl.BlockSpec((tm, D_GROUP), lambda i: (i, 0))
    return pl.pallas_call(
        _out_proj_kernel,
        grid=(m // tm,),
        in_specs=[pl.BlockSpec((tm, D_MODEL), lambda i: (i, 0)), yspec, yspec, yspec, yspec,
                  pl.BlockSpec((D_MODEL, D_MODEL), lambda i: (0, 0))],
        out_specs=pl.BlockSpec((tm, D_MODEL), lambda i: (i, 0)),
        out_shape=jax.ShapeDtypeStruct((m, D_MODEL), F32),
        compiler_params=_params("arbitrary"),
        name="out_proj",
    )(x, *ys, w)


W_IN_SEGMENTS = ((512, COL_XBC, 1024), (0, COL_SSD_Z, 512), (3608, COL_GDN_QKV, 1536), (1544, COL_S5_U, 512),
                 (2568, COL_GLA_V, 512), (3096, COL_GLA_R, 512), (5144, COL_GDN_Z, 512), (2056, COL_GLA_Q, 256),
                 (2312, COL_GLA_K, 256), (1536, COL_SMALL + SM_DT, 8), (3080, COL_SMALL + SM_GR, 16),
                 (5656, COL_SMALL + SM_BETA, 8))


def _permute_w_in_kernel(wt_ref, o_ref):
    tr = o_ref.shape[0]
    eye = _eye(tr).astype(BF16)

    def transposed(rows_bf16):
        return lax.dot_general(eye, rows_bf16, (((1,), (1,)), ((), ())), preferred_element_type=F32).astype(BF16)

    small = []
    for src, dst, width in W_IN_SEGMENTS:
        if width % LANES == 0:
            o_ref[:, dst:dst + width] = transposed(wt_ref[src:src + width, :].astype(BF16))
        else:
            small.append(wt_ref[src:src + width, :])
    used = sum(s.shape[0] for s in small)
    small.append(jnp.zeros((LANES - used, tr), F32))
    o_ref[:, COL_SMALL:COL_SMALL + LANES] = transposed(jnp.concatenate(small, axis=0).astype(BF16))


def _permute_w_in(w, layer):
    _, rows, cols = w.shape
    tr = 256
    return pl.pallas_call(
        _permute_w_in_kernel,
        grid=(rows // tr,),
        in_specs=[pl.BlockSpec((None, cols, tr), lambda r: (layer, 0, r))],
        out_specs=pl.BlockSpec((tr, PROJ_COLS), lambda r: (r, 0)),
        out_shape=jax.ShapeDtypeStruct((rows, PROJ_COLS), BF16),
        compiler_params=_params("arbitrary"),
        name="permute_w_in",
    )(jnp.swapaxes(w, 1, 2))


def _ssd_kernel(xbc_ref, z_ref, sm_ref, conv0_ref, h0_ref, cw_ref, cb_ref, dtb_ref, alog_ref, dex_ref, nrm_ref,
                y_ref, h_ref, cbuf, abuf, *, bb, rows, chunk, valid):
    @pl.when(pl.program_id(1) == 0)
    def _():
        cbuf[:, 0:SUBLANES, :] = conv0_ref[...]
        h_ref[...] = h0_ref[...]

    a_neg = -jnp.exp(alog_ref[...])
    tril = _tril(chunk).astype(F32)
    causal = _tril(chunk)
    left = _iota((1, LANES), 1) < SSD_HEAD_DIM
    top = _iota((LANES, 1), 0) < SSD_HEAD_DIM
    head_rows = _head_rows(SM_DT)

    def pair(v, h0):
        return jnp.where(left, v[:, h0:h0 + 1], v[:, h0 + 1:h0 + 2])

    dt_all = []
    for b in range(bb):
        conv = _causal_conv(xbc_ref[b], cbuf.at[b], cw_ref[...], rows) + cb_ref[...]
        abuf[b] = _silu(conv)
        dt = _softplus(sm_ref[b] + dtb_ref[...])
        if valid < rows:
            dt = jnp.where(_iota((rows, LANES), 0) < valid, dt, 0.0)
        dt_all.append(dt)

    n_c = rows // chunk
    seqs = [(b, c) for b in range(bb) for c in range(n_c)]
    dt = {(b, c): dt_all[b][c * chunk:(c + 1) * chunk, :] for (b, c) in seqs}
    acs = {s: _mm_hi(tril, dt[s] * a_neg) for s in seqs}
    acs_t = {s: _mm_nt_hi(head_rows, acs[s]) for s in seqs}
    groups = [(b, c, g) for (b, c) in seqs for g in range(SSD_GROUPS)]
    bm, cm = {}, {}
    for (b, c, g) in groups:
        r = slice(c * chunk, (c + 1) * chunk)
        bm[b, c, g] = abuf[b, r, D_GROUP + g * SSD_STATE:D_GROUP + (g + 1) * SSD_STATE]
        cm[b, c, g] = abuf[b, r, D_GROUP + (SSD_GROUPS + g) * SSD_STATE:D_GROUP + (SSD_GROUPS + g + 1) * SSD_STATE]
    cb = {x: _mm_nt(cm[x], bm[x]) for x in groups}
    pairs = [(b, c, g, j) for (b, c, g) in groups for j in range(2)]
    xdt, y_diag = {}, {}
    for (b, c, g, j) in pairs:
        h0 = 4 * g + 2 * j
        lo = (2 * g + j) * LANES
        xdt[b, c, g, j] = abuf[b, c * chunk:(c + 1) * chunk, lo:lo + LANES] * pair(dt[b, c], h0)
    for (b, c, g, j) in pairs:
        h0 = 4 * g + 2 * j
        total = None
        for hh, keep in ((h0, left), (h0 + 1, jnp.logical_not(left))):
            diff = acs[b, c][:, hh:hh + 1] - acs_t[b, c][hh:hh + 1, :]
            decay = jnp.exp(jnp.where(causal, diff, -jnp.inf))
            part = _mm(cb[b, c, g] * decay, jnp.where(keep, xdt[b, c, g, j], 0.0))
            total = part if total is None else total + part
        y_diag[b, c, g, j] = total

    for c in range(n_c):
        r0 = c * chunk
        for b in range(bb):
            a_c = acs[b, c]
            exp_acs = jnp.exp(a_c)
            to_end = jnp.exp(a_c[chunk - 1:chunk, :] - a_c)
            end_decay = jnp.exp(a_c[chunk - 1:chunk, :])
            for g in range(SSD_GROUPS):
                for j in range(2):
                    h0 = 4 * g + 2 * j
                    lo = (2 * g + j) * LANES
                    st = h_ref[b, lo:lo + LANES, :]
                    y_ref[b, r0:r0 + chunk, lo:lo + LANES] = (
                        y_diag[b, c, g, j] + _mm_nt(cm[b, c, g], st) * pair(exp_acs, h0))
                    dec = jnp.where(top, end_decay[:, h0:h0 + 1], end_decay[:, h0 + 1:h0 + 2])
                    h_ref[b, lo:lo + LANES, :] = st * dec + _mm_tn(xdt[b, c, g, j] * pair(to_end, h0), bm[b, c, g])

    for b in range(bb):
        y = y_ref[b] + dex_ref[...] * abuf[b, :, 0:D_GROUP]
        y = y * _silu(z_ref[b])
        y_ref[b] = _group_rmsnorm(y, D_GROUP // SSD_GROUPS) * nrm_ref[...]


def _ssd(proj, conv0, state, p, *, bb, rows, chunk, valid):
    b, length, _ = proj.shape
    w_xbc = D_GROUP + 2 * SSD_GROUPS * SSD_STATE
    vec = lambda n: pl.BlockSpec((1, n), lambda bi, i: (0, 0))
    return _mixer_call(
        functools.partial(_ssd_kernel, bb=bb, rows=rows, chunk=chunk, valid=valid),
        grid=(b // bb, length // rows),
        in_specs=[
            pl.BlockSpec((bb, rows, w_xbc), lambda bi, i: (bi, i, COL_XBC // w_xbc)),
            pl.BlockSpec((bb, rows, D_GROUP), lambda bi, i: (bi, i, COL_SSD_Z // D_GROUP)),
            pl.BlockSpec((bb, rows, LANES), lambda bi, i: (bi, i, COL_SMALL // LANES)),
            pl.BlockSpec((bb, SUBLANES, w_xbc), lambda bi, i: (bi, 0, 0)),
            state.in_spec(bb),
            pl.BlockSpec((CONV_W, w_xbc), lambda bi, i: (0, 0)),
            vec(w_xbc), vec(LANES), vec(LANES), vec(D_GROUP), vec(D_GROUP),
        ],
        inputs=[proj, proj, proj, conv0, state.init, p["ssd_conv_w"], p["ssd_conv_b"], p["ssd_dt_bias"],
                p["ssd_a_log"], p["ssd_d"], p["ssd_norm"]],
        out_specs=[pl.BlockSpec((bb, rows, D_GROUP), lambda bi, i: (bi, i, 0)), state.out_spec(bb)],
        out_shape=[jax.ShapeDtypeStruct((b, length, D_GROUP), F32), state.out_shape()],
        states=[state], first_state_out=1,
        scratch=[pltpu.VMEM((bb, rows + SUBLANES, w_xbc), F32), pltpu.VMEM((bb, rows, w_xbc), F32)],
        name="ssd")


def _s5_kernel(u_ref, h0re_ref, h0im_ref, apre_ref, apim_ref, wre_ref, wim_ref, cre_ref, cim_ref, d_ref, wglu_ref,
               bglu_ref, y_ref, hre_ref, him_ref, ubuf, *carry, bb, rows, valid):
    first = pl.program_id(1) == 0
    m = bb * rows
    n_blk = D_GROUP // LANES
    w_blk = S5_LANES // n_blk
    n_tiles = rows // SUBLANES

    @pl.when(first)
    def _():
        ubuf[:, 0:SUBLANES, :] = jnp.zeros((bb, SUBLANES, D_GROUP), F32)
        for buf in carry[2:]:
            buf[...] = jnp.zeros(buf.shape, F32)

    ubuf[:, SUBLANES:SUBLANES + rows, :] = u_ref[...]

    half = LANES // 2
    w_half = w_blk // 2
    left = _iota((1, LANES), 1) < half
    parts = []
    for j in range(n_blk):
        taps = [ubuf[:, SUBLANES - t:SUBLANES - t + rows, j * LANES:(j + 1) * LANES].reshape(m, LANES)
                for t in range(S5_TAPS)]
        packed = ([], [])
        for t in range(0, S5_TAPS, 2):
            packed[0].append(jnp.where(left, taps[t], pltpu.roll(taps[t + 1], half, 1)))
            packed[1].append(jnp.where(left, pltpu.roll(taps[t], half, 1), taps[t + 1]))
        y_j = None
        for hf in range(2):
            k = 2 * j + hf
            sl = slice(k * w_half, (k + 1) * w_half)
            lagged = jnp.concatenate(packed[hf], axis=1).astype(BF16)
            xr = jnp.dot(lagged, wre_ref[k], preferred_element_type=F32)
            xi = jnp.dot(lagged, wim_ref[k], preferred_element_type=F32)
            if n_tiles > 1:
                hbuf_re, hbuf_im, tail_re, tail_im = carry
                ar, ai = apre_ref[S5_TAPS - 1:S5_TAPS, sl], apim_ref[S5_TAPS - 1:S5_TAPS, sl]
                pr, pi = tail_re[:, sl], tail_im[:, sl]
                for q in range(n_tiles):
                    rs = slice(q * SUBLANES, (q + 1) * SUBLANES)
                    pr, pi = xr[rs, :] + ar * pr - ai * pi, xi[rs, :] + ar * pi + ai * pr
                    hbuf_re[rs, :] = pr
                    hbuf_im[rs, :] = pi

                @pl.when(first)
                def _():
                    h0r, h0i = h0re_ref[0, :, sl], h0im_ref[0, :, sl]
                    apr, api = apre_ref[:, sl], apim_ref[:, sl]
                    hbuf_re[...] = hbuf_re[...] + apr * h0r - api * h0i
                    hbuf_im[...] = hbuf_im[...] + apr * h0i + api * h0r

                hr, hi = hbuf_re[...], hbuf_im[...]
                tail_re[:, sl] = hr[rows - SUBLANES:rows, :]
                tail_im[:, sl] = hi[rows - SUBLANES:rows, :]
                hre_ref[0, :, sl] = hr[valid - 1:valid, :]
                him_ref[0, :, sl] = hi[valid - 1:valid, :]
            else:
                h0r, h0i = h0re_ref[:, :, sl], h0im_ref[:, :, sl]
                apr, api = apre_ref[:, sl][None], apim_ref[:, sl][None]
                hr3 = xr.reshape(bb, rows, w_half) + apr * h0r - api * h0i
                hi3 = xi.reshape(bb, rows, w_half) + apr * h0i + api * h0r
                hre_ref[:, :, sl] = hr3[:, valid - 1:valid, :]
                him_ref[:, :, sl] = hi3[:, valid - 1:valid, :]
                hr, hi = hr3.reshape(m, w_half), hi3.reshape(m, w_half)
            rows_c = slice(hf * w_half, (hf + 1) * w_half)
            y_half = _mm(hr, cre_ref[j, rows_c, :]) - _mm(hi, cim_ref[j, rows_c, :])
            y_j = y_half if y_j is None else y_j + y_half
        parts.append(y_j)

    if n_tiles > 1:
        ubuf[:, 0:SUBLANES, :] = ubuf[:, rows:rows + SUBLANES, :]

    y = jnp.concatenate(parts, axis=1) + d_ref[...] * u_ref[...].reshape(m, D_GROUP)
    y = 0.5 * y * (1.0 + jnp.tanh(math.sqrt(2.0 / math.pi) * (y + 0.044715 * (y * y * y))))
    y = y * _sigmoid(_mm(y, wglu_ref[...]) + bglu_ref[...])
    y_ref[...] = y.reshape(bb, rows, D_GROUP)


def _s5_lag_kernel(pre_ref, pim_ref, bre_ref, bim_ref, ore_ref, oim_ref):
    br, bi = bre_ref[...], bim_ref[...]
    kb = br.shape[0]
    for t in range(S5_TAPS):
        pr, pi = pre_ref[t:t + 1, :], pim_ref[t:t + 1, :]
        ore_ref[t * kb:(t + 1) * kb, :] = (pr * br - pi * bi).astype(BF16)
        oim_ref[t * kb:(t + 1) * kb, :] = (pr * bi + pi * br).astype(BF16)


def _s5_lag_weights(pow_re, pow_im, b_re, b_im):
    n_blk, kb, w_blk = b_re.shape
    pw = pl.BlockSpec((S5_TAPS, w_blk), lambda j: (0, j))
    bs = pl.BlockSpec((None, kb, w_blk), lambda j: (j, 0, 0))
    out = pl.BlockSpec((None, S5_TAPS * kb, w_blk), lambda j: (j, 0, 0))
    shape = jax.ShapeDtypeStruct((n_blk, S5_TAPS * kb, w_blk), BF16)
    return pl.pallas_call(
        _s5_lag_kernel, grid=(n_blk,), in_specs=[pw, pw, bs, bs], out_specs=[out, out], out_shape=[shape, shape],
        compiler_params=_params("arbitrary"), name="s5_lag_weights",
    )(pow_re, pow_im, b_re, b_im)


def _s5(proj, state_re, state_im, p, *, bb, rows, valid):
    b, length, _ = proj.shape
    assert rows == S5_TAPS or bb == 1
    n_blk = D_GROUP // LANES
    w_blk = S5_LANES // n_blk
    vec = lambda n: pl.BlockSpec((1, n), lambda bi, i: (0, 0))
    full3 = lambda s: pl.BlockSpec(s, lambda bi, i: (0, 0, 0))
    w_half = w_blk // 2
    carry = []
    if rows > S5_TAPS:
        carry = [pltpu.VMEM((rows, w_half), F32)] * 2 + [pltpu.VMEM((SUBLANES, S5_LANES), F32)] * 2
    return _mixer_call(
        functools.partial(_s5_kernel, bb=bb, rows=rows, valid=valid),
        grid=(b // bb, length // rows),
        in_specs=[
            pl.BlockSpec((bb, rows, D_GROUP), lambda bi, i: (bi, i, COL_S5_U // D_GROUP)),
            state_re.in_spec(bb), state_im.in_spec(bb),
            pl.BlockSpec((rows, S5_LANES), lambda bi, i: (0, 0)),
            pl.BlockSpec((rows, S5_LANES), lambda bi, i: (0, 0)),
            full3((2 * n_blk, S5_TAPS * LANES // 2, w_half)), full3((2 * n_blk, S5_TAPS * LANES // 2, w_half)),
            full3((n_blk, w_blk, LANES)), full3((n_blk, w_blk, LANES)),
            vec(D_GROUP), pl.BlockSpec((D_GROUP, D_GROUP), lambda bi, i: (0, 0)), vec(D_GROUP),
        ],
        inputs=[proj, state_re.init, state_im.init, p["s5_apow_re"][:rows], p["s5_apow_im"][:rows], p["s5_wlag_re"],
                p["s5_wlag_im"], p["s5_cblk_re"], p["s5_cblk_im"], p["s5_d"], p["s5_w_glu"], p["s5_b_glu"]],
        out_specs=[pl.BlockSpec((bb, rows, D_GROUP), lambda bi, i: (bi, i, 0)),
                   state_re.out_spec(bb), state_im.out_spec(bb)],
        out_shape=[jax.ShapeDtypeStruct((b, length, D_GROUP), F32), state_re.out_shape(), state_im.out_shape()],
        states=[state_re, state_im], first_state_out=1,
        scratch=[pltpu.VMEM((bb, SUBLANES + rows, D_GROUP), F32)] + carry,
        name="s5")


def _gla_kernel(q_ref, k_ref, v_ref, r_ref, sm_ref, s0_ref, wg_ref, bg_ref, nrm_ref, o_ref, s_ref,
                *, bb, rows, chunk, valid):
    @pl.when(pl.program_id(1) == 0)
    def _():
        s_ref[...] = s0_ref[...]

    hk = GLA_HEADS * GLA_DK
    tril = _tril(chunk).astype(F32)
    causal = _tril(chunk)
    lane = _iota((1, hk), 1)
    lg_all, k_all = [], []
    for b in range(bb):
        lg = -_softplus(-(_mm(sm_ref[b], wg_ref[...]) + bg_ref[...])) * (1.0 / GLA_GATE_TEMP)
        k = k_ref[b]
        if valid < rows:
            live = _iota((rows, hk), 0) < valid
            lg = jnp.where(live, lg, 0.0)
            k = jnp.where(live, k, 0.0)
        lg_all.append(lg)
        k_all.append(k)

    n_c = rows // chunk
    seqs = [(b, c) for b in range(bb) for c in range(n_c)]
    cum = {(b, c): _mm_hi(tril, lg_all[b][c * chunk:(c + 1) * chunk, :]) for (b, c) in seqs}
    q_dec, k_dec, k_end, end_decay = {}, {}, {}, {}
    for (b, c) in seqs:
        r = slice(c * chunk, (c + 1) * chunk)
        cs = cum[b, c]
        cum_last = cs[chunk - 1:chunk, :]
        q_dec[b, c] = q_ref[b, r, :] * (GLA_DK ** -0.5) * jnp.exp(cs)
        k = k_all[b][r, :]
        k_dec[b, c] = k * jnp.exp(-cs)
        k_end[b, c] = k * jnp.exp(cum_last - cs)
        end_decay[b, c] = _row_to_col(jnp.exp(cum_last))
    units = [(b, c, h) for (b, c) in seqs for h in range(GLA_HEADS)]
    qh, vh = {}, {}
    for (b, c, h) in units:
        mine = (lane >= h * GLA_DK) & (lane < (h + 1) * GLA_DK)
        qh[b, c, h] = jnp.where(mine, q_dec[b, c], 0.0)
        vh[b, c, h] = v_ref[b, c * chunk:(c + 1) * chunk, h * GLA_DV:(h + 1) * GLA_DV]
    att = {x: jnp.where(causal, _mm_nt(qh[x], k_dec[x[:2]]), 0.0) for x in units}
    o_intra = {x: _mm(att[x], vh[x]) for x in units}
    kv = {x: _mm_tn(k_end[x[:2]], vh[x])[x[2] * GLA_DK:(x[2] + 1) * GLA_DK, :] for x in units}

    for c in range(n_c):
        for b in range(bb):
            st = s_ref[b]
            for h in range(GLA_HEADS):
                rs = slice(h * GLA_DK, (h + 1) * GLA_DK)
                o_ref[b, c * chunk:(c + 1) * chunk, h * GLA_DV:(h + 1) * GLA_DV] = (
                    o_intra[b, c, h] + _mm(qh[b, c, h], st))
                s_ref[b, rs, :] = st[rs, :] * end_decay[b, c][rs, :] + kv[b, c, h]

    for b in range(bb):
        o = _group_rmsnorm(o_ref[b], GLA_DV) * nrm_ref[...]
        o_ref[b] = o * _silu(r_ref[b])


def _gla(proj, state, p, *, bb, rows, chunk, valid):
    b, length, _ = proj.shape
    hk = GLA_HEADS * GLA_DK
    vec = lambda n: pl.BlockSpec((1, n), lambda bi, i: (0, 0))
    return _mixer_call(
        functools.partial(_gla_kernel, bb=bb, rows=rows, chunk=chunk, valid=valid),
        grid=(b // bb, length // rows),
        in_specs=[
            pl.BlockSpec((bb, rows, hk), lambda bi, i: (bi, i, COL_GLA_Q // hk)),
            pl.BlockSpec((bb, rows, hk), lambda bi, i: (bi, i, COL_GLA_K // hk)),
            pl.BlockSpec((bb, rows, D_GROUP), lambda bi, i: (bi, i, COL_GLA_V // D_GROUP)),
            pl.BlockSpec((bb, rows, D_GROUP), lambda bi, i: (bi, i, COL_GLA_R // D_GROUP)),
            pl.BlockSpec((bb, rows, LANES), lambda bi, i: (bi, i, COL_SMALL // LANES)),
            state.in_spec(bb),
            pl.BlockSpec((LANES, hk), lambda bi, i: (0, 0)), vec(hk), vec(D_GROUP),
        ],
        inputs=[proj, proj, proj, proj, proj, state.init, p["gla_w_gate2"], p["gla_b_gate2"], p["gla_norm"]],
        out_specs=[pl.BlockSpec((bb, rows, D_GROUP), lambda bi, i: (bi, i, 0)), state.out_spec(bb)],
        out_shape=[jax.ShapeDtypeStruct((b, length, D_GROUP), F32), state.out_shape()],
        states=[state], first_state_out=1, scratch=[], name="gla")


def _gdn_kernel(qkv_ref, z_ref, sm_ref, conv0_ref, s0_ref, cw_ref, alog_ref, dtb_ref, nrm_ref, o_ref, s_ref,
                cbuf, abuf, *, bb, rows, chunk, valid):
    @pl.when(pl.program_id(1) == 0)
    def _():
        cbuf[:, 0:SUBLANES, :] = conv0_ref[...]
        s_ref[...] = s0_ref[...]

    a_neg = -jnp.exp(alog_ref[...])
    tril = _tril(chunk).astype(F32)
    causal = _tril(chunk)
    strict = _tril(chunk, strict=True)
    eye_c = _eye(chunk)
    gate_rows = _head_rows(SM_BETA)
    hd = GDN_HEADS * GDN_DK
    n_c = rows // chunk

    def l2n(x):
        return x * lax.rsqrt(jnp.sum(x * x, axis=1, keepdims=True) + EPS)

    beta_all, g_all = [], []
    for b in range(bb):
        abuf[b] = _silu(_causal_conv(qkv_ref[b], cbuf.at[b], cw_ref[...], rows))
        sm = sm_ref[b]
        beta = _sigmoid(sm)
        g = a_neg * _softplus(sm + dtb_ref[...])
        if valid < rows:
            live = _iota((rows, LANES), 0) < valid
            beta = jnp.where(live, beta, 0.0)
            g = jnp.where(live, g, 0.0)
        beta_all.append(beta)
        g_all.append(g)

    seqs = [(b, c) for b in range(bb) for c in range(n_c)]
    gc = {s: _mm_hi(tril, g_all[s[0]][s[1] * chunk:(s[1] + 1) * chunk, :]) for s in seqs}
    gc_t = {s: _mm_nt_hi(gate_rows, gc[s]) for s in seqs}

    units = [(b, c, h) for (b, c) in seqs for h in range(GDN_HEADS)]
    q, k, v, g_col, b_col, decay = {}, {}, {}, {}, {}, {}
    for (b, c, h) in units:
        r = slice(c * chunk, (c + 1) * chunk)
        q[b, c, h] = l2n(abuf[b, r, h * GDN_DK:(h + 1) * GDN_DK]) * (GDN_DK ** -0.5)
        k[b, c, h] = l2n(abuf[b, r, hd + h * GDN_DK:hd + (h + 1) * GDN_DK])
        v[b, c, h] = abuf[b, r, 2 * hd + h * GDN_DK:2 * hd + (h + 1) * GDN_DK]
        g_col[b, c, h] = gc[b, c][:, SM_DECAY + h:SM_DECAY + h + 1]
        g_row = gc_t[b, c][SM_DECAY - SM_BETA + h:SM_DECAY - SM_BETA + h + 1, :]
        b_col[b, c, h] = beta_all[b][r, SM_BETA + h:SM_BETA + h + 1]
        decay[b, c, h] = jnp.exp(jnp.where(causal, g_col[b, c, h] - g_row, -jnp.inf))
    kk = {x: _mm_nt(k[x], k[x]) for x in units}
    npow = {x: jnp.where(strict, -(b_col[x] * kk[x] * decay[x]), 0.0) for x in units}
    t_mat = {x: eye_c + npow[x] for x in units}
    span = 2
    parts = {x: _split(npow[x]) for x in units}
    while span < chunk:
        parts = {x: _split(_mm3(parts[x], parts[x])) for x in units}
        t_mat = {x: t_mat[x] + _mm3(_split(t_mat[x]), parts[x]) for x in units}
        span *= 2
    uw = {x: _mm(t_mat[x], jnp.concatenate([v[x] * b_col[x], k[x] * (b_col[x] * jnp.exp(g_col[x]))], axis=1))
          for x in units}
    qk = {x: _mm_nt(q[x], k[x]) * decay[x] for x in units}
    q_dec = {x: q[x] * jnp.exp(g_col[x]) for x in units}

    for c in range(n_c):
        now = [(b, c, h) for b in range(bb) for h in range(GDN_HEADS)]
        st = {x: s_ref[x[0], x[2] * GDN_DK:(x[2] + 1) * GDN_DK, :] for x in now}
        v_new = {x: uw[x][:, :GDN_DK] - _mm(uw[x][:, GDN_DK:], st[x]) for x in now}
        for x in now:
            b, _, h = x
            g_last = gc[b, c][chunk - 1:chunk, SM_DECAY + h:SM_DECAY + h + 1]
            o_ref[b, c * chunk:(c + 1) * chunk, h * GDN_DK:(h + 1) * GDN_DK] = (
                _mm(q_dec[x], st[x]) + _mm(qk[x], v_new[x]))
            s_ref[b, h * GDN_DK:(h + 1) * GDN_DK, :] = (
                st[x] * jnp.exp(g_last) + _mm_tn(k[x] * jnp.exp(g_last - g_col[x]), v_new[x]))

    for b in range(bb):
        o = _group_rmsnorm(o_ref[b], GDN_DK) * nrm_ref[...]
        o_ref[b] = o * _silu(z_ref[b])


def _gdn(proj, conv0, state, p, *, bb, rows, chunk, valid):
    b, length, _ = proj.shape
    w_qkv = 3 * GDN_HEADS * GDN_DK
    vec = lambda n: pl.BlockSpec((1, n), lambda bi, i: (0, 0))
    return _mixer_call(
        functools.partial(_gdn_kernel, bb=bb, rows=rows, chunk=chunk, valid=valid),
        grid=(b // bb, length // rows),
        in_specs=[
            pl.BlockSpec((bb, rows, w_qkv), lambda bi, i: (bi, i, COL_GDN_QKV // w_qkv)),
            pl.BlockSpec((bb, rows, D_GROUP), lambda bi, i: (bi, i, COL_GDN_Z // D_GROUP)),
            pl.BlockSpec((bb, rows, LANES), lambda bi, i: (bi, i, COL_SMALL // LANES)),
            pl.BlockSpec((bb, SUBLANES, w_qkv), lambda bi, i: (bi, 0, 0)),
            state.in_spec(bb),
            pl.BlockSpec((CONV_W, w_qkv), lambda bi, i: (0, 0)),
            vec(LANES), vec(LANES), vec(D_GROUP),
        ],
        inputs=[proj, proj, proj, conv0, state.init, p["gdn_conv_w"], p["gdn_a_log"], p["gdn_dt_bias"], p["gdn_norm"]],
        out_specs=[pl.BlockSpec((bb, rows, D_GROUP), lambda bi, i: (bi, i, 0)), state.out_spec(bb)],
        out_shape=[jax.ShapeDtypeStruct((b, length, D_GROUP), F32), state.out_shape()],
        states=[state], first_state_out=1,
        scratch=[pltpu.VMEM((bb, rows + SUBLANES, w_qkv), F32), pltpu.VMEM((bb, rows, w_qkv), F32)],
        name="gdn")


def _lane_row(values, offset):
    return jnp.zeros((1, LANES), F32).at[0, offset:offset + values.shape[0]].set(values.astype(F32))


def _prep_layer(w, i, max_rows):
    row = lambda a: a[i].astype(F32).reshape(1, -1)
    p = {}
    for name in ("ffn1", "ffn2"):
        p[name + "_norm"] = row(w[name + "_norm"])
        for part in ("_w_gate", "_w_up", "_w_down"):
            p[name + part] = w[name + part].astype(F32)
    p["mix_norm"] = row(w["mix_norm"])

    p["w_in"] = _permute_w_in(w["w_in"].astype(F32), i)
    p["w_out"] = w["w_out"][i].astype(BF16)

    p["ssd_conv_w"] = w["ssd_conv_w"][i].astype(F32)
    p["ssd_conv_b"] = row(w["ssd_conv_b"])
    p["ssd_dt_bias"] = _lane_row(w["ssd_dt_bias"][i], SM_DT)
    p["ssd_a_log"] = _lane_row(w["ssd_a_log"][i], SM_DT)
    p["ssd_d"] = jnp.repeat(w["ssd_d"][i].astype(F32), SSD_HEAD_DIM).reshape(1, D_GROUP)
    p["ssd_norm"] = row(w["ssd_norm"])

    a_re, a_im = w["s5_a_re"][i].astype(F32), w["s5_a_im"][i].astype(F32)
    dt = jnp.exp(w["s5_log_dt"][i].astype(F32))[:, None]
    lam_re, lam_im = a_re * dt, a_im * dt

    def a_bar_pow(n):
        mag = jnp.exp(lam_re * n)
        return mag * jnp.cos(lam_im * n), mag * jnp.sin(lam_im * n)

    abar_re, abar_im = a_bar_pow(1.0)
    den = a_re * a_re + a_im * a_im
    coef_re = ((abar_re - 1.0) * a_re + abar_im * a_im) / den
    coef_im = (abar_im * a_re - (abar_re - 1.0) * a_im) / den
    b_re, b_im = w["s5_b_re"][i].astype(F32), w["s5_b_im"][i].astype(F32)
    bbar_re = coef_re[..., None] * b_re - coef_im[..., None] * b_im
    bbar_im = coef_re[..., None] * b_im + coef_im[..., None] * b_re
    steps = jnp.arange(1, max_rows + 1, dtype=F32)[:, None, None]
    apow_re, apow_im = a_bar_pow(steps)
    p["s5_apow_re"] = apow_re.reshape(max_rows, S5_LANES)
    p["s5_apow_im"] = apow_im.reshape(max_rows, S5_LANES)
    lag_re, lag_im = a_bar_pow(jnp.arange(S5_TAPS, dtype=F32)[:, None, None])
    n_blk = D_GROUP // LANES
    gpb = S5_GROUPS // n_blk
    eye = jnp.eye(gpb, dtype=F32)

    def b_blocks(x):
        half = gpb // 2
        x = jnp.swapaxes(x, 1, 2).reshape(2 * n_blk, half, S5_CH, S5_STATE)
        return jnp.einsum("jgip,gh->jgihp", x, eye[:half, :half]).reshape(2 * n_blk, LANES // 2, half * S5_STATE)

    def c_blocks(x):
        x = x.reshape(n_blk, gpb, S5_CH, S5_STATE)
        return jnp.einsum("jgip,gh->jgphi", x, eye).reshape(n_blk, gpb * S5_STATE, LANES).astype(BF16)

    p["s5_wlag_re"], p["s5_wlag_im"] = _s5_lag_weights(
        lag_re.reshape(S5_TAPS, S5_LANES), lag_im.reshape(S5_TAPS, S5_LANES), b_blocks(bbar_re), b_blocks(bbar_im))
    p["s5_cblk_re"] = c_blocks(w["s5_c_re"][i].astype(F32))
    p["s5_cblk_im"] = c_blocks(w["s5_c_im"][i].astype(F32))
    p["s5_d"] = row(w["s5_d"])
    p["s5_w_glu"] = w["s5_w_glu"][i].astype(BF16)
    p["s5_b_glu"] = row(w["s5_b_glu"])

    hk = GLA_HEADS * GLA_DK
    p["gla_w_gate2"] = jnp.zeros((LANES, hk), F32).at[SM_GR:SM_GR + GLA_GATE_RANK].set(w["gla_w_gate2"][i].astype(F32))
    p["gla_b_gate2"] = row(w["gla_b_gate2"])
    p["gla_norm"] = row(w["gla_norm"])

    p["gdn_conv_w"] = w["gdn_conv_w"][i].astype(F32)
    p["gdn_a_log"] = _lane_row(w["gdn_a_log"][i], SM_DECAY)
    p["gdn_dt_bias"] = _lane_row(w["gdn_dt_bias"][i], SM_DECAY)
    p["gdn_norm"] = row(w["gdn_norm"])
    return p


def _mixer_geometry(bsz, length):
    if length >= CHUNK:
        assert length % CHUNK == 0
        rows = 256 if length % 256 == 0 else CHUNK
        return 1, rows, CHUNK, length
    assert length <= SUBLANES
    bb = SUBLANES if bsz % SUBLANES == 0 else 1
    return bb, SUBLANES, SUBLANES, SUBLANES


def _conv_buffer(state):
    return jnp.pad(state.astype(F32), ((0, 0), (SUBLANES - (CONV_W - 1), 0), (0, 0)))


def _trunk(x, states, layers, final_norm):
    bsz, length, _ = x.shape
    depth = len(layers)
    bb, rows, chunk, lpad = _mixer_geometry(bsz, length)
    geo = dict(bb=bb, rows=rows, chunk=chunk, valid=min(length, rows))
    h = x.reshape(bsz * length, D_MODEL).astype(F32)
    fg = final_norm.astype(F32).reshape(1, D_MODEL)
    ssd_conv, ssd_h, s5_re, s5_im, gla_s, gdn_conv, gdn_s = [s.astype(F32) for s in states]
    lead = ssd_h.shape[0]
    recurrent = [ssd_h.reshape(lead, bsz, D_GROUP, SSD_STATE), s5_re.reshape(lead, bsz, 1, S5_LANES),
                 s5_im.reshape(lead, bsz, 1, S5_LANES), gla_s.reshape(lead, bsz, GLA_HEADS * GLA_DK, GLA_DV),
                 gdn_s.reshape(lead, bsz, GDN_HEADS * GDN_DK, GDN_DK)]
    new = [None] * len(recurrent)
    ssd_conv_new, gdn_conv_new = [], []
    for i, p in enumerate(layers):
        src = min(i, lead - 1)
        st = [_Layered(init, src, i, depth, prev) for init, prev in zip(recurrent, new)]
        h = _ffn(h, p["ffn1_norm"], p["ffn1_w_gate"], p["ffn1_w_up"], p["ffn1_w_down"], i, fg, False)
        proj = _in_proj(h, p["mix_norm"], p["w_in"]).reshape(bsz, length, PROJ_COLS)
        ssd_conv_new.append(proj[:, length - (CONV_W - 1):, COL_XBC:COL_XBC + 1024])
        gdn_conv_new.append(proj[:, length - (CONV_W - 1):, COL_GDN_QKV:COL_GDN_QKV + 1536])
        if lpad != length:
            proj = jnp.pad(proj, ((0, 0), (0, lpad - length), (0, 0)))
        y_ssd, new[0] = _ssd(proj, _conv_buffer(ssd_conv[src]), st[0], p, **geo)
        s5_bb = bsz if rows == S5_TAPS else bb
        y_s5, new[1], new[2] = _s5(proj, st[1], st[2], p, bb=s5_bb, rows=rows, valid=geo["valid"])
        y_gla, new[3] = _gla(proj, st[3], p, **geo)
        y_gdn, new[4] = _gdn(proj, _conv_buffer(gdn_conv[src]), st[4], p, **geo)
        ys = [y[:, :length].reshape(bsz * length, D_GROUP) for y in (y_ssd, y_s5, y_gla, y_gdn)]
        h = _out_proj(h, ys, p["w_out"])
        h = _ffn(h, p["ffn2_norm"], p["ffn2_w_gate"], p["ffn2_w_up"], p["ffn2_w_down"], i, fg, i == depth - 1)
    out_states = (jnp.stack(ssd_conv_new), new[0].reshape(depth, bsz, SSD_HEADS, SSD_HEAD_DIM, SSD_STATE),
                  new[1].reshape(depth, bsz, S5_GROUPS, S5_STATE), new[2].reshape(depth, bsz, S5_GROUPS, S5_STATE),
                  new[3].reshape(depth, bsz, GLA_HEADS, GLA_DK, GLA_DV), jnp.stack(gdn_conv_new),
                  new[4].reshape(depth, bsz, GDN_HEADS, GDN_DK, GDN_DK))
    return h.reshape(bsz, length, D_MODEL), out_states


def kernel(x_prompt, x_sample, state_ssd_conv, state_ssd, state_s5_re, state_s5_im, state_gla, state_gdn_conv, state_gdn, ffn1_norm, ffn1_w_gate, ffn1_w_up, ffn1_w_down, mix_norm, w_in, ssd_conv_w, ssd_conv_b, ssd_dt_bias, ssd_a_log, ssd_d, ssd_norm, s5_a_re, s5_a_im, s5_log_dt, s5_b_re, s5_b_im, s5_c_re, s5_c_im, s5_d, s5_w_glu, s5_b_glu, gla_w_gate2, gla_b_gate2, gla_norm, gdn_conv_w, gdn_a_log, gdn_dt_bias, gdn_norm, w_out, ffn2_norm, ffn2_w_gate, ffn2_w_up, ffn2_w_down, final_norm):
    w = dict(ffn1_norm=ffn1_norm, ffn1_w_gate=ffn1_w_gate, ffn1_w_up=ffn1_w_up, ffn1_w_down=ffn1_w_down,
             mix_norm=mix_norm, w_in=w_in, ssd_conv_w=ssd_conv_w, ssd_conv_b=ssd_conv_b, ssd_dt_bias=ssd_dt_bias,
             ssd_a_log=ssd_a_log, ssd_d=ssd_d, ssd_norm=ssd_norm, s5_a_re=s5_a_re, s5_a_im=s5_a_im,
             s5_log_dt=s5_log_dt, s5_b_re=s5_b_re, s5_b_im=s5_b_im, s5_c_re=s5_c_re, s5_c_im=s5_c_im, s5_d=s5_d,
             s5_w_glu=s5_w_glu, s5_b_glu=s5_b_glu, gla_w_gate2=gla_w_gate2, gla_b_gate2=gla_b_gate2,
             gla_norm=gla_norm, gdn_conv_w=gdn_conv_w, gdn_a_log=gdn_a_log, gdn_dt_bias=gdn_dt_bias,
             gdn_norm=gdn_norm, w_out=w_out, ffn2_norm=ffn2_norm, ffn2_w_gate=ffn2_w_gate, ffn2_w_up=ffn2_w_up,
             ffn2_w_down=ffn2_w_down)
    depth = w_in.shape[0]
    max_rows = max(_mixer_geometry(*x_prompt.shape[:2])[1], _mixer_geometry(*x_sample.shape[:2])[1])
    layers = [_prep_layer(w, i, max_rows) for i in range(depth)]

    sample_states = (state_ssd_conv, state_ssd, state_s5_re, state_s5_im, state_gla, state_gdn_conv, state_gdn)
    n_prompt = x_prompt.shape[0]
    prompt_states = tuple(jnp.zeros((1, n_prompt) + s.shape[2:], F32) for s in sample_states)
    y_prompt, p_states = _trunk(x_prompt, prompt_states, layers, final_norm)
    y_sample, s_states = _trunk(x_sample, sample_states, layers, final_norm)
    return (y_prompt, y_sample) + p_states + s_states
```

```python
import functools
import math

import numpy as np
import jax
import jax.numpy as jnp
from jax import lax
from jax.experimental import pallas as pl
from jax.experimental.pallas import tpu as pltpu

F32 = jnp.float32
BF16 = jnp.bfloat16
HIGHEST = lax.Precision.HIGHEST
EPS = 1e-6

LANES = 128
SUBLANES = 8
VMEM_LIMIT_BYTES = 56 * 1024 * 1024
FFN_VMEM_LIMIT_BYTES = 60 * 1024 * 1024

D_MODEL = 2048
D_GROUP = 512
D_FF = 5504
D_FF_PAD = 5632
CONV_W = 4
CHUNK = 64

SSD_HEADS = 8
SSD_HEAD_DIM = 64
SSD_GROUPS = 2
SSD_STATE = 128
S5_GROUPS = 32
S5_CH = 16
S5_STATE = 64
S5_LANES = S5_GROUPS * S5_STATE
S5_TAPS = 8
GLA_HEADS = 4
GLA_DK = 64
GLA_DV = 128
GLA_GATE_RANK = 16
GLA_GATE_TEMP = 16.0
GDN_HEADS = 4
GDN_DK = 128

COL_XBC = 0
COL_SSD_Z = 1024
COL_GDN_QKV = 1536
COL_S5_U = 3072
COL_GLA_V = 3584
COL_GLA_R = 4096
COL_GDN_Z = 4608
COL_GLA_Q = 5120
COL_GLA_K = 5376
COL_SMALL = 5632
PROJ_COLS = 5760
SM_DT = 0
SM_GR = 8
SM_BETA = 24
SM_DECAY = 28


def _mm(a, b):
    return jnp.dot(a.astype(BF16), b.astype(BF16), preferred_element_type=F32)


def _mm_nt(a, b):
    return lax.dot_general(a.astype(BF16), b.astype(BF16), (((1,), (1,)), ((), ())), preferred_element_type=F32)


def _mm_tn(a, b):
    return lax.dot_general(a.astype(BF16), b.astype(BF16), (((0,), (0,)), ((), ())), preferred_element_type=F32)


def _mm_hi(a, b):
    return jnp.dot(a, b, precision=HIGHEST, preferred_element_type=F32)


def _mm_nt_hi(a, b):
    return lax.dot_general(a, b, (((1,), (1,)), ((), ())), precision=HIGHEST, preferred_element_type=F32)


def _split(a):
    hi = a.astype(BF16)
    return hi, (a - hi.astype(F32)).astype(BF16)


def _mm3(a, b):
    (ah, al), (bh, bl) = a, b
    dot = functools.partial(jnp.dot, preferred_element_type=F32)
    return dot(ah, bh) + (dot(al, bh) + dot(ah, bl))


def _head_rows(first_lane):
    return (_iota((SUBLANES, LANES), 1) == _iota((SUBLANES, LANES), 0) + first_lane).astype(F32)


def _sigmoid(x):
    return 1.0 / (1.0 + jnp.exp(-x))


def _silu(x):
    return x * _sigmoid(x)


def _softplus(x):
    return jnp.maximum(x, 0.0) + jnp.log1p(jnp.exp(-jnp.abs(x)))


def _iota(shape, dim):
    return lax.broadcasted_iota(jnp.int32, shape, dim)


def _tril(n, strict=False):
    r, c = _iota((n, n), 0), _iota((n, n), 1)
    return (r > c) if strict else (r >= c)


def _eye(n):
    return (_iota((n, n), 0) == _iota((n, n), 1)).astype(F32)


def _row_to_col(row):
    n = row.shape[1]
    return jnp.sum(_eye(n) * row, axis=1, keepdims=True)


def _group_rmsnorm(y, width):
    parts = []
    for g in range(y.shape[1] // width):
        yg = y[:, g * width:(g + 1) * width]
        parts.append(yg * lax.rsqrt(jnp.mean(yg * yg, axis=1, keepdims=True) + EPS))
    return jnp.concatenate(parts, axis=1)


def _causal_conv(x, cbuf, w, rows):
    cbuf[SUBLANES:SUBLANES + rows, :] = x
    y = (w[0:1, :] * cbuf[5:5 + rows, :] + w[1:2, :] * cbuf[6:6 + rows, :]
         + w[2:3, :] * cbuf[7:7 + rows, :] + w[3:4, :] * x)
    cbuf[0:SUBLANES, :] = cbuf[rows:rows + SUBLANES, :]
    return y


def _params(*sem, vmem_limit_bytes=VMEM_LIMIT_BYTES):
    return pltpu.CompilerParams(dimension_semantics=sem, vmem_limit_bytes=vmem_limit_bytes)


class _Layered:
    def __init__(self, init, src, dst, depth, prev):
        self.init, self.src, self.dst, self.depth, self.prev = init, src, dst, depth, prev

    def in_spec(self, bb):
        tail = self.init.shape[2:]
        return pl.BlockSpec((None, bb) + tail, lambda bi, i, s=self.src, z=(0,) * len(tail): (s, bi) + z)

    def out_spec(self, bb):
        tail = self.init.shape[2:]
        return pl.BlockSpec((None, bb) + tail, lambda bi, i, d=self.dst, z=(0,) * len(tail): (d, bi) + z)

    def out_shape(self):
        return jax.ShapeDtypeStruct((self.depth,) + self.init.shape[1:], F32)


def _mixer_call(body, *, grid, in_specs, inputs, out_specs, out_shape, states, first_state_out, scratch, name):
    n_in = len(inputs)
    prevs = [s.prev for s in states if s.prev is not None]
    aliases = {}
    fn = body
    if prevs:
        assert len(prevs) == len(states)
        in_specs = list(in_specs) + [pl.BlockSpec(memory_space=pl.ANY)] * len(prevs)
        aliases = {n_in + j: first_state_out + j for j in range(len(prevs))}

        def fn(*refs):
            return body(*refs[:n_in], *refs[n_in + len(prevs):])

    return pl.pallas_call(
        fn, grid=grid, in_specs=in_specs, out_specs=out_specs, out_shape=out_shape, scratch_shapes=scratch,
        input_output_aliases=aliases, compiler_params=_params("arbitrary", "arbitrary"), name=name,
    )(*inputs, *prevs)


def _ffn_kernel(x_ref, g_ref, wg_ref, wu_ref, wd_ref, fg_ref, o_ref, xn_ref, *, n_f, tf, d_ff, final_norm):
    f = pl.program_id(1)

    @pl.when(f == 0)
    def _():
        x = x_ref[...]
        xn = x * lax.rsqrt(jnp.mean(x * x, axis=1, keepdims=True) + EPS) * g_ref[...]
        xn_ref[...] = xn.astype(BF16)
        o_ref[...] = jnp.zeros_like(o_ref)

    wg, wu, wd = wg_ref[0].astype(BF16), wu_ref[0].astype(BF16), wd_ref[0].astype(BF16)
    keep = None
    if d_ff % tf:
        repeat = f * tf - jnp.minimum(f * tf, d_ff - tf)
        keep = _iota((1, tf), 1) >= repeat
    xn = xn_ref[...]
    gate = jnp.dot(xn, wg, preferred_element_type=F32)
    up = jnp.dot(xn, wu, preferred_element_type=F32)
    h = _silu(gate) * up
    if keep is not None:
        h = jnp.where(keep, h, 0.0)
    o_ref[...] += jnp.dot(h.astype(BF16), wd, preferred_element_type=F32)

    @pl.when(f == n_f - 1)
    def _():
        y = x_ref[...] + 0.5 * o_ref[...]
        if final_norm:
            y = y * lax.rsqrt(jnp.mean(y * y, axis=1, keepdims=True) + EPS) * fg_ref[...]
        o_ref[...] = y


def _ffn(x, g, wg, wu, wd, layer, fg, final_norm):
    m = x.shape[0]
    d_ff = wg.shape[2]
    tm = min(1024, m)
    tf = 256
    n_f = pl.cdiv(d_ff, tf)
    assert d_ff % LANES == 0 and tf % LANES == 0
    start = lambda f: pl.multiple_of(jnp.minimum(f * tf, d_ff - tf), LANES)
    return pl.pallas_call(
        functools.partial(_ffn_kernel, n_f=n_f, tf=tf, d_ff=d_ff, final_norm=final_norm),
        grid=(m // tm, n_f),
        in_specs=[
            pl.BlockSpec((tm, D_MODEL), lambda i, f: (i, 0)),
            pl.BlockSpec((1, D_MODEL), lambda i, f: (0, 0)),
            pl.BlockSpec((pl.Element(1), pl.Element(D_MODEL), pl.Element(tf)), lambda i, f: (layer, 0, start(f))),
            pl.BlockSpec((pl.Element(1), pl.Element(D_MODEL), pl.Element(tf)), lambda i, f: (layer, 0, start(f))),
            pl.BlockSpec((pl.Element(1), pl.Element(tf), pl.Element(D_MODEL)), lambda i, f: (layer, start(f), 0)),
            pl.BlockSpec((1, D_MODEL), lambda i, f: (0, 0)),
        ],
        out_specs=pl.BlockSpec((tm, D_MODEL), lambda i, f: (i, 0)),
        out_shape=jax.ShapeDtypeStruct((m, D_MODEL), F32),
        scratch_shapes=[pltpu.VMEM((tm, D_MODEL), BF16)],
        compiler_params=_params("arbitrary", "arbitrary", vmem_limit_bytes=FFN_VMEM_LIMIT_BYTES),
        name="ffn",
    )(x, g, wg, wu, wd, fg)


def _in_proj_kernel(x_ref, g_ref, w_ref, o_ref, xn_ref):
    @pl.when(pl.program_id(1) == 0)
    def _():
        x = x_ref[...]
        xn = x * lax.rsqrt(jnp.mean(x * x, axis=1, keepdims=True) + EPS) * g_ref[...]
        xn_ref[...] = xn.astype(BF16)

    o_ref[...] = jnp.dot(xn_ref[...], w_ref[...], preferred_element_type=F32)


def _in_proj(x, g, w):
    m = x.shape[0]
    tm = min(1024, m)
    tn = 1920
    return pl.pallas_call(
        _in_proj_kernel,
        grid=(m // tm, PROJ_COLS // tn),
        in_specs=[
            pl.BlockSpec((tm, D_MODEL), lambda i, n: (i, 0)),
            pl.BlockSpec((1, D_MODEL), lambda i, n: (0, 0)),
            pl.BlockSpec((D_MODEL, tn), lambda i, n: (0, n)),
        ],
        out_specs=pl.BlockSpec((tm, tn), lambda i, n: (i, n)),
        out_shape=jax.ShapeDtypeStruct((m, PROJ_COLS), F32),
        scratch_shapes=[pltpu.VMEM((tm, D_MODEL), BF16)],
        compiler_params=_params("arbitrary", "arbitrary"),
        name="in_proj",
    )(x, g, w)


def _out_proj_kernel(x_ref, y0_ref, y1_ref, y2_ref, y3_ref, w_ref, o_ref):
    acc = x_ref[...]
    for j, y_ref in enumerate((y0_ref, y1_ref, y2_ref, y3_ref)):
        acc = acc + jnp.dot(y_ref[...].astype(BF16), w_ref[j * D_GROUP:(j + 1) * D_GROUP, :],
                            preferred_element_type=F32)
    o_ref[...] = acc


def _out_proj(x, ys, w):
    m = x.shape[0]
    tm = min(512, m)
    yspec = pl.BlockSpec((tm, D_GROUP), lambda i: (i, 0))
    return pl.pallas_call(
        _out_proj_kernel,
        grid=(m // tm,),
        in_specs=[pl.BlockSpec((tm, D_MODEL), lambda i: (i, 0)), yspec, yspec, yspec, yspec,
                  pl.BlockSpec((D_MODEL, D_MODEL), lambda i: (0, 0))],
        out_specs=pl.BlockSpec((tm, D_MODEL), lambda i: (i, 0)),
        out_shape=jax.ShapeDtypeStruct((m, D_MODEL), F32),
        compiler_params=_params("arbitrary"),
        name="out_proj",
    )(x, *ys, w)


W_IN_SEGMENTS = ((512, COL_XBC, 1024), (0, COL_SSD_Z, 512), (3608, COL_GDN_QKV, 1536), (1544, COL_S5_U, 512),
                 (2568, COL_GLA_V, 512), (3096, COL_GLA_R, 512), (5144, COL_GDN_Z, 512), (2056, COL_GLA_Q, 256),
                 (2312, COL_GLA_K, 256), (1536, COL_SMALL + SM_DT, 8), (3080, COL_SMALL + SM_GR, 16),
                 (5656, COL_SMALL + SM_BETA, 8))


def _permute_w_in_kernel(wt_ref, o_ref):
    tr = o_ref.shape[0]
    eye = _eye(tr).astype(BF16)

    def transposed(rows_bf16):
        return lax.dot_general(eye, rows_bf16, (((1,), (1,)), ((), ())), preferred_element_type=F32).astype(BF16)

    small = []
    for src, dst, width in W_IN_SEGMENTS:
        if width % LANES == 0:
            o_ref[:, dst:dst + width] = transposed(wt_ref[src:src + width, :].astype(BF16))
        else:
            small.append(wt_ref[src:src + width, :])
    used = sum(s.shape[0] for s in small)
    small.append(jnp.zeros((LANES - used, tr), F32))
    o_ref[:, COL_SMALL:COL_SMALL + LANES] = transposed(jnp.concatenate(small, axis=0).astype(BF16))


def _permute_w_in(w, layer):
    _, rows, cols = w.shape
    tr = 256
    return pl.pallas_call(
        _permute_w_in_kernel,
        grid=(rows // tr,),
        in_specs=[pl.BlockSpec((None, cols, tr), lambda r: (layer, 0, r))],
        out_specs=pl.BlockSpec((tr, PROJ_COLS), lambda r: (r, 0)),
        out_shape=jax.ShapeDtypeStruct((rows, PROJ_COLS), BF16),
        compiler_params=_params("arbitrary"),
        name="permute_w_in",
    )(jnp.swapaxes(w, 1, 2))


def _ssd_kernel(xbc_ref, z_ref, sm_ref, conv0_ref, h0_ref, cw_ref, cb_ref, dtb_ref, alog_ref, dex_ref, nrm_ref,
                y_ref, h_ref, cbuf, abuf, *, bb, rows, chunk, valid):
    @pl.when(pl.program_id(1) == 0)
    def _():
        cbuf[:, 0:SUBLANES, :] = conv0_ref[...]
        h_ref[...] = h0_ref[...]

    a_neg = -jnp.exp(alog_ref[...])
    tril = _tril(chunk).astype(F32)
    causal = _tril(chunk)
    left = _iota((1, LANES), 1) < SSD_HEAD_DIM
    top = _iota((LANES, 1), 0) < SSD_HEAD_DIM
    head_rows = _head_rows(SM_DT)

    def pair(v, h0):
        return jnp.where(left, v[:, h0:h0 + 1], v[:, h0 + 1:h0 + 2])

    dt_all = []
    for b in range(bb):
        conv = _causal_conv(xbc_ref[b], cbuf.at[b], cw_ref[...], rows) + cb_ref[...]
        abuf[b] = _silu(conv)
        dt = _softplus(sm_ref[b] + dtb_ref[...])
        if valid < rows:
            dt = jnp.where(_iota((rows, LANES), 0) < valid, dt, 0.0)
        dt_all.append(dt)

    n_c = rows // chunk
    seqs = [(b, c) for b in range(bb) for c in range(n_c)]
    dt = {(b, c): dt_all[b][c * chunk:(c + 1) * chunk, :] for (b, c) in seqs}
    acs = {s: _mm_hi(tril, dt[s] * a_neg) for s in seqs}
    acs_t = {s: _mm_nt_hi(head_rows, acs[s]) for s in seqs}
    groups = [(b, c, g) for (b, c) in seqs for g in range(SSD_GROUPS)]
    bm, cm = {}, {}
    for (b, c, g) in groups:
        r = slice(c * chunk, (c + 1) * chunk)
        bm[b, c, g] = abuf[b, r, D_GROUP + g * SSD_STATE:D_GROUP + (g + 1) * SSD_STATE]
        cm[b, c, g] = abuf[b, r, D_GROUP + (SSD_GROUPS + g) * SSD_STATE:D_GROUP + (SSD_GROUPS + g + 1) * SSD_STATE]
    cb = {x: _mm_nt(cm[x], bm[x]) for x in groups}
    pairs = [(b, c, g, j) for (b, c, g) in groups for j in range(2)]
    xdt, y_diag = {}, {}
    for (b, c, g, j) in pairs:
        h0 = 4 * g + 2 * j
        lo = (2 * g + j) * LANES
        xdt[b, c, g, j] = abuf[b, c * chunk:(c + 1) * chunk, lo:lo + LANES] * pair(dt[b, c], h0)
    for (b, c, g, j) in pairs:
        h0 = 4 * g + 2 * j
        total = None
        for hh, keep in ((h0, left), (h0 + 1, jnp.logical_not(left))):
            diff = acs[b, c][:, hh:hh + 1] - acs_t[b, c][hh:hh + 1, :]
            decay = jnp.exp(jnp.where(causal, diff, -jnp.inf))
            part = _mm(cb[b, c, g] * decay, jnp.where(keep, xdt[b, c, g, j], 0.0))
            total = part if total is None else total + part
        y_diag[b, c, g, j] = total

    for c in range(n_c):
        r0 = c * chunk
        for b in range(bb):
            a_c = acs[b, c]
            exp_acs = jnp.exp(a_c)
            to_end = jnp.exp(a_c[chunk - 1:chunk, :] - a_c)
            end_decay = jnp.exp(a_c[chunk - 1:chunk, :])
            for g in range(SSD_GROUPS):
                for j in range(2):
                    h0 = 4 * g + 2 * j
                    lo = (2 * g + j) * LANES
                    st = h_ref[b, lo:lo + LANES, :]
                    y_ref[b, r0:r0 + chunk, lo:lo + LANES] = (
                        y_diag[b, c, g, j] + _mm_nt(cm[b, c, g], st) * pair(exp_acs, h0))
                    dec = jnp.where(top, end_decay[:, h0:h0 + 1], end_decay[:, h0 + 1:h0 + 2])
                    h_ref[b, lo:lo + LANES, :] = st * dec + _mm_tn(xdt[b, c, g, j] * pair(to_end, h0), bm[b, c, g])

    for b in range(bb):
        y = y_ref[b] + dex_ref[...] * abuf[b, :, 0:D_GROUP]
        y = y * _silu(z_ref[b])
        y_ref[b] = _group_rmsnorm(y, D_GROUP // SSD_GROUPS) * nrm_ref[...]


def _ssd(proj, conv0, state, p, *, bb, rows, chunk, valid):
    b, length, _ = proj.shape
    w_xbc = D_GROUP + 2 * SSD_GROUPS * SSD_STATE
    vec = lambda n: pl.BlockSpec((1, n), lambda bi, i: (0, 0))
    return _mixer_call(
        functools.partial(_ssd_kernel, bb=bb, rows=rows, chunk=chunk, valid=valid),
        grid=(b // bb, length // rows),
        in_specs=[
            pl.BlockSpec((bb, rows, w_xbc), lambda bi, i: (bi, i, COL_XBC // w_xbc)),
            pl.BlockSpec((bb, rows, D_GROUP), lambda bi, i: (bi, i, COL_SSD_Z // D_GROUP)),
            pl.BlockSpec((bb, rows, LANES), lambda bi, i: (bi, i, COL_SMALL // LANES)),
            pl.BlockSpec((bb, SUBLANES, w_xbc), lambda bi, i: (bi, 0, 0)),
            state.in_spec(bb),
            pl.BlockSpec((CONV_W, w_xbc), lambda bi, i: (0, 0)),
            vec(w_xbc), vec(LANES), vec(LANES), vec(D_GROUP), vec(D_GROUP),
        ],
        inputs=[proj, proj, proj, conv0, state.init, p["ssd_conv_w"], p["ssd_conv_b"], p["ssd_dt_bias"],
                p["ssd_a_log"], p["ssd_d"], p["ssd_norm"]],
        out_specs=[pl.BlockSpec((bb, rows, D_GROUP), lambda bi, i: (bi, i, 0)), state.out_spec(bb)],
        out_shape=[jax.ShapeDtypeStruct((b, length, D_GROUP), F32), state.out_shape()],
        states=[state], first_state_out=1,
        scratch=[pltpu.VMEM((bb, rows + SUBLANES, w_xbc), F32), pltpu.VMEM((bb, rows, w_xbc), F32)],
        name="ssd")


def _s5_kernel(u_ref, h0re_ref, h0im_ref, apre_ref, apim_ref, wre_ref, wim_ref, cre_ref, cim_ref, d_ref, wglu_ref,
               bglu_ref, y_ref, hre_ref, him_ref, ubuf, *carry, bb, rows, valid):
    first = pl.program_id(1) == 0
    m = bb * rows
    n_blk = D_GROUP // LANES
    w_blk = S5_LANES // n_blk
    n_tiles = rows // SUBLANES

    @pl.when(first)
    def _():
        ubuf[:, 0:SUBLANES, :] = jnp.zeros((bb, SUBLANES, D_GROUP), F32)
        for buf in carry[2:]:
            buf[...] = jnp.zeros(buf.shape, F32)

    ubuf[:, SUBLANES:SUBLANES + rows, :] = u_ref[...]

    half = LANES // 2
    w_half = w_blk // 2
    left = _iota((1, LANES), 1) < half
    parts = []
    for j in range(n_blk):
        taps = [ubuf[:, SUBLANES - t:SUBLANES - t + rows, j * LANES:(j + 1) * LANES].reshape(m, LANES)
                for t in range(S5_TAPS)]
        packed = ([], [])
        for t in range(0, S5_TAPS, 2):
            packed[0].append(jnp.where(left, taps[t], pltpu.roll(taps[t + 1], half, 1)))
            packed[1].append(jnp.where(left, pltpu.roll(taps[t], half, 1), taps[t + 1]))
        y_j = None
        for hf in range(2):
            k = 2 * j + hf
            sl = slice(k * w_half, (k + 1) * w_half)
            lagged = jnp.concatenate(packed[hf], axis=1).astype(BF16)
            xr = jnp.dot(lagged, wre_ref[k], preferred_element_type=F32)
            xi = jnp.dot(lagged, wim_ref[k], preferred_element_type=F32)
            if n_tiles > 1:
                hbuf_re, hbuf_im, tail_re, tail_im = carry
                ar, ai = apre_ref[S5_TAPS - 1:S5_TAPS, sl], apim_ref[S5_TAPS - 1:S5_TAPS, sl]
                pr, pi = tail_re[:, sl], tail_im[:, sl]
                for q in range(n_tiles):
                    rs = slice(q * SUBLANES, (q + 1) * SUBLANES)
                    pr, pi = xr[rs, :] + ar * pr - ai * pi, xi[rs, :] + ar * pi + ai * pr
                    hbuf_re[rs, :] = pr
                    hbuf_im[rs, :] = pi

                @pl.when(first)
                def _():
                    h0r, h0i = h0re_ref[0, :, sl], h0im_ref[0, :, sl]
                    apr, api = apre_ref[:, sl], apim_ref[:, sl]
                    hbuf_re[...] = hbuf_re[...] + apr * h0r - api * h0i
                    hbuf_im[...] = hbuf_im[...] + apr * h0i + api * h0r

                hr, hi = hbuf_re[...], hbuf_im[...]
                tail_re[:, sl] = hr[rows - SUBLANES:rows, :]
                tail_im[:, sl] = hi[rows - SUBLANES:rows, :]
                hre_ref[0, :, sl] = hr[valid - 1:valid, :]
                him_ref[0, :, sl] = hi[valid - 1:valid, :]
            else:
                h0r, h0i = h0re_ref[:, :, sl], h0im_ref[:, :, sl]
                apr, api = apre_ref[:, sl][None], apim_ref[:, sl][None]
                hr3 = xr.reshape(bb, rows, w_half) + apr * h0r - api * h0i
                hi3 = xi.reshape(bb, rows, w_half) + apr * h0i + api * h0r
                hre_ref[:, :, sl] = hr3[:, valid - 1:valid, :]
                him_ref[:, :, sl] = hi3[:, valid - 1:valid, :]
                hr, hi = hr3.reshape(m, w_half), hi3.reshape(m, w_half)
            rows_c = slice(hf * w_half, (hf + 1) * w_half)
            y_half = _mm(hr, cre_ref[j, rows_c, :]) - _mm(hi, cim_ref[j, rows_c, :])
            y_j = y_half if y_j is None else y_j + y_half
        parts.append(y_j)

    if n_tiles > 1:
        ubuf[:, 0:SUBLANES, :] = ubuf[:, rows:rows + SUBLANES, :]

    y = jnp.concatenate(parts, axis=1) + d_ref[...] * u_ref[...].reshape(m, D_GROUP)
    y = 0.5 * y * (1.0 + jnp.tanh(math.sqrt(2.0 / math.pi) * (y + 0.044715 * (y * y * y))))
    y = y * _sigmoid(_mm(y, wglu_ref[...]) + bglu_ref[...])
    y_ref[...] = y.reshape(bb, rows, D_GROUP)


def _s5_lag_kernel(pre_ref, pim_ref, bre_ref, bim_ref, ore_ref, oim_ref):
    br, bi = bre_ref[...], bim_ref[...]
    kb = br.shape[0]
    for t in range(S5_TAPS):
        pr, pi = pre_ref[t:t + 1, :], pim_ref[t:t + 1, :]
        ore_ref[t * kb:(t + 1) * kb, :] = (pr * br - pi * bi).astype(BF16)
        oim_ref[t * kb:(t + 1) * kb, :] = (pr * bi + pi * br).astype(BF16)


def _s5_lag_weights(pow_re, pow_im, b_re, b_im):
    n_blk, kb, w_blk = b_re.shape
    pw = pl.BlockSpec((S5_TAPS, w_blk), lambda j: (0, j))
    bs = pl.BlockSpec((None, kb, w_blk), lambda j: (j, 0, 0))
    out = pl.BlockSpec((None, S5_TAPS * kb, w_blk), lambda j: (j, 0, 0))
    shape = jax.ShapeDtypeStruct((n_blk, S5_TAPS * kb, w_blk), BF16)
    return pl.pallas_call(
        _s5_lag_kernel, grid=(n_blk,), in_specs=[pw, pw, bs, bs], out_specs=[out, out], out_shape=[shape, shape],
        compiler_params=_params("arbitrary"), name="s5_lag_weights",
    )(pow_re, pow_im, b_re, b_im)


def _s5(proj, state_re, state_im, p, *, bb, rows, valid):
    b, length, _ = proj.shape
    assert rows == S5_TAPS or bb == 1
    n_blk = D_GROUP // LANES
    w_blk = S5_LANES // n_blk
    vec = lambda n: pl.BlockSpec((1, n), lambda bi, i: (0, 0))
    full3 = lambda s: pl.BlockSpec(s, lambda bi, i: (0, 0, 0))
    w_half = w_blk // 2
    carry = []
    if rows > S5_TAPS:
        carry = [pltpu.VMEM((rows, w_half), F32)] * 2 + [pltpu.VMEM((SUBLANES, S5_LANES), F32)] * 2
    return _mixer_call(
        functools.partial(_s5_kernel, bb=bb, rows=rows, valid=valid),
        grid=(b // bb, length // rows),
        in_specs=[
            pl.BlockSpec((bb, rows, D_GROUP), lambda bi, i: (bi, i, COL_S5_U // D_GROUP)),
            state_re.in_spec(bb), state_im.in_spec(bb),
            pl.BlockSpec((rows, S5_LANES), lambda bi, i: (0, 0)),
            pl.BlockSpec((rows, S5_LANES), lambda bi, i: (0, 0)),
            full3((2 * n_blk, S5_TAPS * LANES // 2, w_half)), full3((2 * n_blk, S5_TAPS * LANES // 2, w_half)),
            full3((n_blk, w_blk, LANES)), full3((n_blk, w_blk, LANES)),
            vec(D_GROUP), pl.BlockSpec((D_GROUP, D_GROUP), lambda bi, i: (0, 0)), vec(D_GROUP),
        ],
        inputs=[proj, state_re.init, state_im.init, p["s5_apow_re"][:rows], p["s5_apow_im"][:rows], p["s5_wlag_re"],
                p["s5_wlag_im"], p["s5_cblk_re"], p["s5_cblk_im"], p["s5_d"], p["s5_w_glu"], p["s5_b_glu"]],
        out_specs=[pl.BlockSpec((bb, rows, D_GROUP), lambda bi, i: (bi, i, 0)),
                   state_re.out_spec(bb), state_im.out_spec(bb)],
        out_shape=[jax.ShapeDtypeStruct((b, length, D_GROUP), F32), state_re.out_shape(), state_im.out_shape()],
        states=[state_re, state_im], first_state_out=1,
        scratch=[pltpu.VMEM((bb, SUBLANES + rows, D_GROUP), F32)] + carry,
        name="s5")


def _gla_kernel(q_ref, k_ref, v_ref, r_ref, sm_ref, s0_ref, wg_ref, bg_ref, nrm_ref, o_ref, s_ref,
                *, bb, rows, chunk, valid):
    @pl.when(pl.program_id(1) == 0)
    def _():
        s_ref[...] = s0_ref[...]

    hk = GLA_HEADS * GLA_DK
    tril = _tril(chunk).astype(F32)
    causal = _tril(chunk)
    lane = _iota((1, hk), 1)
    lg_all, k_all = [], []
    for b in range(bb):
        lg = -_softplus(-(_mm(sm_ref[b], wg_ref[...]) + bg_ref[...])) * (1.0 / GLA_GATE_TEMP)
        k = k_ref[b]
        if valid < rows:
            live = _iota((rows, hk), 0) < valid
            lg = jnp.where(live, lg, 0.0)
            k = jnp.where(live, k, 0.0)
        lg_all.append(lg)
        k_all.append(k)

    n_c = rows // chunk
    seqs = [(b, c) for b in range(bb) for c in range(n_c)]
    cum = {(b, c): _mm_hi(tril, lg_all[b][c * chunk:(c + 1) * chunk, :]) for (b, c) in seqs}
    q_dec, k_dec, k_end, end_decay = {}, {}, {}, {}
    for (b, c) in seqs:
        r = slice(c * chunk, (c + 1) * chunk)
        cs = cum[b, c]
        cum_last = cs[chunk - 1:chunk, :]
        q_dec[b, c] = q_ref[b, r, :] * (GLA_DK ** -0.5) * jnp.exp(cs)
        k = k_all[b][r, :]
        k_dec[b, c] = k * jnp.exp(-cs)
        k_end[b, c] = k * jnp.exp(cum_last - cs)
        end_decay[b, c] = _row_to_col(jnp.exp(cum_last))
    units = [(b, c, h) for (b, c) in seqs for h in range(GLA_HEADS)]
    qh, vh = {}, {}
    for (b, c, h) in units:
        mine = (lane >= h * GLA_DK) & (lane < (h + 1) * GLA_DK)
        qh[b, c, h] = jnp.where(mine, q_dec[b, c], 0.0)
        vh[b, c, h] = v_ref[b, c * chunk:(c + 1) * chunk, h * GLA_DV:(h + 1) * GLA_DV]
    att = {x: jnp.where(causal, _mm_nt(qh[x], k_dec[x[:2]]), 0.0) for x in units}
    o_intra = {x: _mm(att[x], vh[x]) for x in units}
    kv = {x: _mm_tn(k_end[x[:2]], vh[x])[x[2] * GLA_DK:(x[2] + 1) * GLA_DK, :] for x in units}

    for c in range(n_c):
        for b in range(bb):
            st = s_ref[b]
            for h in range(GLA_HEADS):
                rs = slice(h * GLA_DK, (h + 1) * GLA_DK)
                o_ref[b, c * chunk:(c + 1) * chunk, h * GLA_DV:(h + 1) * GLA_DV] = (
                    o_intra[b, c, h] + _mm(qh[b, c, h], st))
                s_ref[b, rs, :] = st[rs, :] * end_decay[b, c][rs, :] + kv[b, c, h]

    for b in range(bb):
        o = _group_rmsnorm(o_ref[b], GLA_DV) * nrm_ref[...]
        o_ref[b] = o * _silu(r_ref[b])


def _gla(proj, state, p, *, bb, rows, chunk, valid):
    b, length, _ = proj.shape
    hk = GLA_HEADS * GLA_DK
    vec = lambda n: pl.BlockSpec((1, n), lambda bi, i: (0, 0))
    return _mixer_call(
        functools.partial(_gla_kernel, bb=bb, rows=rows, chunk=chunk, valid=valid),
        grid=(b // bb, length // rows),
        in_specs=[
            pl.BlockSpec((bb, rows, hk), lambda bi, i: (bi, i, COL_GLA_Q // hk)),
            pl.BlockSpec((bb, rows, hk), lambda bi, i: (bi, i, COL_GLA_K // hk)),
            pl.BlockSpec((bb, rows, D_GROUP), lambda bi, i: (bi, i, COL_GLA_V // D_GROUP)),
            pl.BlockSpec((bb, rows, D_GROUP), lambda bi, i: (bi, i, COL_GLA_R // D_GROUP)),
            pl.BlockSpec((bb, rows, LANES), lambda bi, i: (bi, i, COL_SMALL // LANES)),
            state.in_spec(bb),
            pl.BlockSpec((LANES, hk), lambda bi, i: (0, 0)), vec(hk), vec(D_GROUP),
        ],
        inputs=[proj, proj, proj, proj, proj, state.init, p["gla_w_gate2"], p["gla_b_gate2"], p["gla_norm"]],
        out_specs=[pl.BlockSpec((bb, rows, D_GROUP), lambda bi, i: (bi, i, 0)), state.out_spec(bb)],
        out_shape=[jax.ShapeDtypeStruct((b, length, D_GROUP), F32), state.out_shape()],
        states=[state], first_state_out=1, scratch=[], name="gla")


def _gdn_kernel(qkv_ref, z_ref, sm_ref, conv0_ref, s0_ref, cw_ref, alog_ref, dtb_ref, nrm_ref, o_ref, s_ref,
                cbuf, abuf, *, bb, rows, chunk, valid):
    @pl.when(pl.program_id(1) == 0)
    def _():
        cbuf[:, 0:SUBLANES, :] = conv0_ref[...]
        s_ref[...] = s0_ref[...]

    a_neg = -jnp.exp(alog_ref[...])
    tril = _tril(chunk).astype(F32)
    causal = _tril(chunk)
    strict = _tril(chunk, strict=True)
    eye_c = _eye(chunk)
    gate_rows = _head_rows(SM_BETA)
    hd = GDN_HEADS * GDN_DK
    n_c = rows // chunk

    def l2n(x):
        return x * lax.rsqrt(jnp.sum(x * x, axis=1, keepdims=True) + EPS)

    beta_all, g_all = [], []
    for b in range(bb):
        abuf[b] = _silu(_causal_conv(qkv_ref[b], cbuf.at[b], cw_ref[...], rows))
        sm = sm_ref[b]
        beta = _sigmoid(sm)
        g = a_neg * _softplus(sm + dtb_ref[...])
        if valid < rows:
            live = _iota((rows, LANES), 0) < valid
            beta = jnp.where(live, beta, 0.0)
            g = jnp.where(live, g, 0.0)
        beta_all.append(beta)
        g_all.append(g)

    seqs = [(b, c) for b in range(bb) for c in range(n_c)]
    gc = {s: _mm_hi(tril, g_all[s[0]][s[1] * chunk:(s[1] + 1) * chunk, :]) for s in seqs}
    gc_t = {s: _mm_nt_hi(gate_rows, gc[s]) for s in seqs}

    units = [(b, c, h) for (b, c) in seqs for h in range(GDN_HEADS)]
    q, k, v, g_col, b_col, decay = {}, {}, {}, {}, {}, {}
    for (b, c, h) in units:
        r = slice(c * chunk, (c + 1) * chunk)
        q[b, c, h] = l2n(abuf[b, r, h * GDN_DK:(h + 1) * GDN_DK]) * (GDN_DK ** -0.5)
        k[b, c, h] = l2n(abuf[b, r, hd + h * GDN_DK:hd + (h + 1) * GDN_DK])
        v[b, c, h] = abuf[b, r, 2 * hd + h * GDN_DK:2 * hd + (h + 1) * GDN_DK]
        g_col[b, c, h] = gc[b, c][:, SM_DECAY + h:SM_DECAY + h + 1]
        g_row = gc_t[b, c][SM_DECAY - SM_BETA + h:SM_DECAY - SM_BETA + h + 1, :]
        b_col[b, c, h] = beta_all[b][r, SM_BETA + h:SM_BETA + h + 1]
        decay[b, c, h] = jnp.exp(jnp.where(causal, g_col[b, c, h] - g_row, -jnp.inf))
    kk = {x: _mm_nt(k[x], k[x]) for x in units}
    npow = {x: jnp.where(strict, -(b_col[x] * kk[x] * decay[x]), 0.0) for x in units}
    t_mat = {x: eye_c + npow[x] for x in units}
    span = 2
    parts = {x: _split(npow[x]) for x in units}
    while span < chunk:
        parts = {x: _split(_mm3(parts[x], parts[x])) for x in units}
        t_mat = {x: t_mat[x] + _mm3(_split(t_mat[x]), parts[x]) for x in units}
        span *= 2
    uw = {x: _mm(t_mat[x], jnp.concatenate([v[x] * b_col[x], k[x] * (b_col[x] * jnp.exp(g_col[x]))], axis=1))
          for x in units}
    qk = {x: _mm_nt(q[x], k[x]) * decay[x] for x in units}
    q_dec = {x: q[x] * jnp.exp(g_col[x]) for x in units}

    for c in range(n_c):
        now = [(b, c, h) for b in range(bb) for h in range(GDN_HEADS)]
        st = {x: s_ref[x[0], x[2] * GDN_DK:(x[2] + 1) * GDN_DK, :] for x in now}
        v_new = {x: uw[x][:, :GDN_DK] - _mm(uw[x][:, GDN_DK:], st[x]) for x in now}
        for x in now:
            b, _, h = x
            g_last = gc[b, c][chunk - 1:chunk, SM_DECAY + h:SM_DECAY + h + 1]
            o_ref[b, c * chunk:(c + 1) * chunk, h * GDN_DK:(h + 1) * GDN_DK] = (
                _mm(q_dec[x], st[x]) + _mm(qk[x], v_new[x]))
            s_ref[b, h * GDN_DK:(h + 1) * GDN_DK, :] = (
                st[x] * jnp.exp(g_last) + _mm_tn(k[x] * jnp.exp(g_last - g_col[x]), v_new[x]))

    for b in range(bb):
        o = _group_rmsnorm(o_ref[b], GDN_DK) * nrm_ref[...]
        o_ref[b] = o * _silu(z_ref[b])


def _gdn(proj, conv0, state, p, *, bb, rows, chunk, valid):
    b, length, _ = proj.shape
    w_qkv = 3 * GDN_HEADS * GDN_DK
    vec = lambda n: pl.BlockSpec((1, n), lambda bi, i: (0, 0))
    return _mixer_call(
        functools.partial(_gdn_kernel, bb=bb, rows=rows, chunk=chunk, valid=valid),
        grid=(b // bb, length // rows),
        in_specs=[
            pl.BlockSpec((bb, rows, w_qkv), lambda bi, i: (bi, i, COL_GDN_QKV // w_qkv)),
            pl.BlockSpec((bb, rows, D_GROUP), lambda bi, i: (bi, i, COL_GDN_Z // D_GROUP)),
            pl.BlockSpec((bb, rows, LANES), lambda bi, i: (bi, i, COL_SMALL // LANES)),
            pl.BlockSpec((bb, SUBLANES, w_qkv), lambda bi, i: (bi, 0, 0)),
            state.in_spec(bb),
            pl.BlockSpec((CONV_W, w_qkv), lambda bi, i: (0, 0)),
            vec(LANES), vec(LANES), vec(D_GROUP),
        ],
        inputs=[proj, proj, proj, conv0, state.init, p["gdn_conv_w"], p["gdn_a_log"], p["gdn_dt_bias"], p["gdn_norm"]],
        out_specs=[pl.BlockSpec((bb, rows, D_GROUP), lambda bi, i: (bi, i, 0)), state.out_spec(bb)],
        out_shape=[jax.ShapeDtypeStruct((b, length, D_GROUP), F32), state.out_shape()],
        states=[state], first_state_out=1,
        scratch=[pltpu.VMEM((bb, rows + SUBLANES, w_qkv), F32), pltpu.VMEM((bb, rows, w_qkv), F32)],
        name="gdn")


def _lane_row(values, offset):
    return jnp.zeros((1, LANES), F32).at[0, offset:offset + values.shape[0]].set(values.astype(F32))


def _prep_layer(w, i, max_rows):
    row = lambda a: a[i].astype(F32).reshape(1, -1)
    p = {}
    for name in ("ffn1", "ffn2"):
        p[name + "_norm"] = row(w[name + "_norm"])
        for part in ("_w_gate", "_w_up", "_w_down"):
            p[name + part] = w[name + part].astype(F32)
    p["mix_norm"] = row(w["mix_norm"])

    p["w_in"] = _permute_w_in(w["w_in"].astype(F32), i)
    p["w_out"] = w["w_out"][i].astype(BF16)

    p["ssd_conv_w"] = w["ssd_conv_w"][i].astype(F32)
    p["ssd_conv_b"] = row(w["ssd_conv_b"])
    p["ssd_dt_bias"] = _lane_row(w["ssd_dt_bias"][i], SM_DT)
    p["ssd_a_log"] = _lane_row(w["ssd_a_log"][i], SM_DT)
    p["ssd_d"] = jnp.repeat(w["ssd_d"][i].astype(F32), SSD_HEAD_DIM).reshape(1, D_GROUP)
    p["ssd_norm"] = row(w["ssd_norm"])

    a_re, a_im = w["s5_a_re"][i].astype(F32), w["s5_a_im"][i].astype(F32)
    dt = jnp.exp(w["s5_log_dt"][i].astype(F32))[:, None]
    lam_re, lam_im = a_re * dt, a_im * dt

    def a_bar_pow(n):
        mag = jnp.exp(lam_re * n)
        return mag * jnp.cos(lam_im * n), mag * jnp.sin(lam_im * n)

    abar_re, abar_im = a_bar_pow(1.0)
    den = a_re * a_re + a_im * a_im
    coef_re = ((abar_re - 1.0) * a_re + abar_im * a_im) / den
    coef_im = (abar_im * a_re - (abar_re - 1.0) * a_im) / den
    b_re, b_im = w["s5_b_re"][i].astype(F32), w["s5_b_im"][i].astype(F32)
    bbar_re = coef_re[..., None] * b_re - coef_im[..., None] * b_im
    bbar_im = coef_re[..., None] * b_im + coef_im[..., None] * b_re
    steps = jnp.arange(1, max_rows + 1, dtype=F32)[:, None, None]
    apow_re, apow_im = a_bar_pow(steps)
    p["s5_apow_re"] = apow_re.reshape(max_rows, S5_LANES)
    p["s5_apow_im"] = apow_im.reshape(max_rows, S5_LANES)
    lag_re, lag_im = a_bar_pow(jnp.arange(S5_TAPS, dtype=F32)[:, None, None])
    n_blk = D_GROUP // LANES
    gpb = S5_GROUPS // n_blk
    eye = jnp.eye(gpb, dtype=F32)

    def b_blocks(x):
        half = gpb // 2
        x = jnp.swapaxes(x, 1, 2).reshape(2 * n_blk, half, S5_CH, S5_STATE)
        return jnp.einsum("jgip,gh->jgihp", x, eye[:half, :half]).reshape(2 * n_blk, LANES // 2, half * S5_STATE)

    def c_blocks(x):
        x = x.reshape(n_blk, gpb, S5_CH, S5_STATE)
        return jnp.einsum("jgip,gh->jgphi", x, eye).reshape(n_blk, gpb * S5_STATE, LANES).astype(BF16)

    p["s5_wlag_re"], p["s5_wlag_im"] = _s5_lag_weights(
        lag_re.reshape(S5_TAPS, S5_LANES), lag_im.reshape(S5_TAPS, S5_LANES), b_blocks(bbar_re), b_blocks(bbar_im))
    p["s5_cblk_re"] = c_blocks(w["s5_c_re"][i].astype(F32))
    p["s5_cblk_im"] = c_blocks(w["s5_c_im"][i].astype(F32))
    p["s5_d"] = row(w["s5_d"])
    p["s5_w_glu"] = w["s5_w_glu"][i].astype(BF16)
    p["s5_b_glu"] = row(w["s5_b_glu"])

    hk = GLA_HEADS * GLA_DK
    p["gla_w_gate2"] = jnp.zeros((LANES, hk), F32).at[SM_GR:SM_GR + GLA_GATE_RANK].set(w["gla_w_gate2"][i].astype(F32))
    p["gla_b_gate2"] = row(w["gla_b_gate2"])
    p["gla_norm"] = row(w["gla_norm"])

    p["gdn_conv_w"] = w["gdn_conv_w"][i].astype(F32)
    p["gdn_a_log"] = _lane_row(w["gdn_a_log"][i], SM_DECAY)
    p["gdn_dt_bias"] = _lane_row(w["gdn_dt_bias"][i], SM_DECAY)
    p["gdn_norm"] = row(w["gdn_norm"])
    return p


def _mixer_geometry(bsz, length):
    if length >= CHUNK:
        assert length % CHUNK == 0
        rows = 256 if length % 256 == 0 else CHUNK
        return (2 if bsz % 2 == 0 else 1), rows, CHUNK, length
    assert length <= SUBLANES
    bb = SUBLANES if bsz % SUBLANES == 0 else 1
    return bb, SUBLANES, SUBLANES, SUBLANES


def _conv_buffer(state):
    return jnp.pad(state.astype(F32), ((0, 0), (SUBLANES - (CONV_W - 1), 0), (0, 0)))


def _trunk(x, states, layers, final_norm):
    bsz, length, _ = x.shape
    depth = len(layers)
    bb, rows, chunk, lpad = _mixer_geometry(bsz, length)
    geo = dict(bb=bb, rows=rows, chunk=chunk, valid=min(length, rows))
    h = x.reshape(bsz * length, D_MODEL).astype(F32)
    fg = final_norm.astype(F32).reshape(1, D_MODEL)
    ssd_conv, ssd_h, s5_re, s5_im, gla_s, gdn_conv, gdn_s = [s.astype(F32) for s in states]
    lead = ssd_h.shape[0]
    recurrent = [ssd_h.reshape(lead, bsz, D_GROUP, SSD_STATE), s5_re.reshape(lead, bsz, 1, S5_LANES),
                 s5_im.reshape(lead, bsz, 1, S5_LANES), gla_s.reshape(lead, bsz, GLA_HEADS * GLA_DK, GLA_DV),
                 gdn_s.reshape(lead, bsz, GDN_HEADS * GDN_DK, GDN_DK)]
    new = [jnp.zeros((depth,) + r.shape[1:], F32) for r in recurrent]
    ssd_conv_new, gdn_conv_new = [], []
    for i, p in enumerate(layers):
        src = min(i, lead - 1)
        st = [_Layered(init, src, i, depth, prev) for init, prev in zip(recurrent, new)]
        h = _ffn(h, p["ffn1_norm"], p["ffn1_w_gate"], p["ffn1_w_up"], p["ffn1_w_down"], i, fg, False)
        proj = _in_proj(h, p["mix_norm"], p["w_in"]).reshape(bsz, length, PROJ_COLS)
        ssd_conv_new.append(proj[:, length - (CONV_W - 1):, COL_XBC:COL_XBC + 1024])
        gdn_conv_new.append(proj[:, length - (CONV_W - 1):, COL_GDN_QKV:COL_GDN_QKV + 1536])
        if lpad != length:
            proj = jnp.pad(proj, ((0, 0), (0, lpad - length), (0, 0)))
        y_ssd, new[0] = _ssd(proj, _conv_buffer(ssd_conv[src]), st[0], p, **geo)
        s5_bb = bsz if rows == S5_TAPS else 1
        y_s5, new[1], new[2] = _s5(proj, st[1], st[2], p, bb=s5_bb, rows=rows, valid=geo["valid"])
        y_gla, new[3] = _gla(proj, st[3], p, **geo)
        y_gdn, new[4] = _gdn(proj, _conv_buffer(gdn_conv[src]), st[4], p, **geo)
        ys = [y[:, :length].reshape(bsz * length, D_GROUP) for y in (y_ssd, y_s5, y_gla, y_gdn)]
        h = _out_proj(h, ys, p["w_out"])
        h = _ffn(h, p["ffn2_norm"], p["ffn2_w_gate"], p["ffn2_w_up"], p["ffn2_w_down"], i, fg, i == depth - 1)
    out_states = (jnp.stack(ssd_conv_new), new[0].reshape(depth, bsz, SSD_HEADS, SSD_HEAD_DIM, SSD_STATE),
                  new[1].reshape(depth, bsz, S5_GROUPS, S5_STATE), new[2].reshape(depth, bsz, S5_GROUPS, S5_STATE),
                  new[3].reshape(depth, bsz, GLA_HEADS, GLA_DK, GLA_DV), jnp.stack(gdn_conv_new),
                  new[4].reshape(depth, bsz, GDN_HEADS, GDN_DK, GDN_DK))
    return h.reshape(bsz, length, D_MODEL), out_states


def kernel(x_prompt, x_sample, state_ssd_conv, state_ssd, state_s5_re, state_s5_im, state_gla, state_gdn_conv, state_gdn, ffn1_norm, ffn1_w_gate, ffn1_w_up, ffn1_w_down, mix_norm, w_in, ssd_conv_w, ssd_conv_b, ssd_dt_bias, ssd_a_log, ssd_d, ssd_norm, s5_a_re, s5_a_im, s5_log_dt, s5_b_re, s5_b_im, s5_c_re, s5_c_im, s5_d, s5_w_glu, s5_b_glu, gla_w_gate2, gla_b_gate2, gla_norm, gdn_conv_w, gdn_a_log, gdn_dt_bias, gdn_norm, w_out, ffn2_norm, ffn2_w_gate, ffn2_w_up, ffn2_w_down, final_norm):
    w = dict(ffn1_norm=ffn1_norm, ffn1_w_gate=ffn1_w_gate, ffn1_w_up=ffn1_w_up, ffn1_w_down=ffn1_w_down,
             mix_norm=mix_norm, w_in=w_in, ssd_conv_w=ssd_conv_w, ssd_conv_b=ssd_conv_b, ssd_dt_bias=ssd_dt_bias,
             ssd_a_log=ssd_a_log, ssd_d=ssd_d, ssd_norm=ssd_norm, s5_a_re=s5_a_re, s5_a_im=s5_a_im,
             s5_log_dt=s5_log_dt, s5_b_re=s5_b_re, s5_b_im=s5_b_im, s5_c_re=s5_c_re, s5_c_im=s5_c_im, s5_d=s5_d,
             s5_w_glu=s5_w_glu, s5_b_glu=s5_b_glu, gla_w_gate2=gla_w_gate2, gla_b_gate2=gla_b_gate2,
             gla_norm=gla_norm, gdn_conv_w=gdn_conv_w, gdn_a_log=gdn_a_log, gdn_dt_bias=gdn_dt_bias,
             gdn_norm=gdn_norm, w_out=w_out, ffn2_norm=ffn2_norm, ffn2_w_gate=ffn2_w_gate, ffn2_w_up=ffn2_w_up,
             ffn2_w_down=ffn2_w_down)
    depth = w_in.shape[0]
    max_rows = max(_mixer_geometry(*x_prompt.shape[:2])[1], _mixer_geometry(*x_sample.shape[:2])[1])
    layers = [_prep_layer(w, i, max_rows) for i in range(depth)]

    sample_states = (state_ssd_conv, state_ssd, state_s5_re, state_s5_im, state_gla, state_gdn_conv, state_gdn)
    n_prompt = x_prompt.shape[0]
    prompt_states = tuple(jnp.zeros((1, n_prompt) + s.shape[2:], F32) for s in sample_states)
    y_prompt, p_states = _trunk(x_prompt, prompt_states, layers, final_norm)
    y_sample, s_states = _trunk(x_sample, sample_states, layers, final_norm)
    return (y_prompt, y_sample) + p_states + s_states
```

```python
import functools
import math

import jax
import jax.numpy as jnp
from jax import lax
from jax.experimental import pallas as pl
from jax.experimental.pallas import tpu as pltpu

F32 = jnp.float32
BF16 = jnp.bfloat16
HIGHEST = lax.Precision.HIGHEST
EPS = 1e-6

LANES = 128
SUBLANES = 8
MXU_WIDTH = 256
VMEM_BYTES = 64 * 1024 * 1024
VMEM_LIMIT_BYTES = VMEM_BYTES - 8 * 1024 * 1024
FFN_VMEM_LIMIT_BYTES = VMEM_BYTES - 4 * 1024 * 1024

FFN_TOKEN_TILE = 1024
FFN_HIDDEN_TILE = MXU_WIDTH
FFN_HIDDEN_TILE_SMALL_M = 2 * MXU_WIDTH
IN_PROJ_TOKEN_TILE = 1024
IN_PROJ_COL_TILE = 1920
OUT_PROJ_TOKEN_TILE = 512
MIXER_ROWS = 256
LONG_SEQS_PER_STEP = 2

D_MODEL = 2048
D_GROUP = 512
CONV_W = 4
CHUNK = 64

SSD_HEADS = 8
SSD_HEAD_DIM = 64
SSD_GROUPS = 2
SSD_STATE = 128
S5_GROUPS = 32
S5_CH = 16
S5_STATE = 64
S5_LANES = S5_GROUPS * S5_STATE
S5_TAPS = 8
GLA_HEADS = 4
GLA_DK = 64
GLA_DV = 128
GLA_GATE_RANK = 16
GLA_GATE_TEMP = 16.0
GDN_HEADS = 4
GDN_DK = 128
SSD_CONV_CH = D_GROUP + 2 * SSD_GROUPS * SSD_STATE
GDN_CONV_CH = 3 * GDN_HEADS * GDN_DK

COL_XBC = 0
COL_SSD_Z = 1024
COL_GDN_QKV = 1536
COL_S5_U = 3072
COL_GLA_V = 3584
COL_GLA_R = 4096
COL_GDN_Z = 4608
COL_GLA_Q = 5120
COL_GLA_K = 5376
COL_SMALL = 5632
PROJ_COLS = 5760
SM_DT = 0
SM_GR = 8
SM_BETA = 24
SM_DECAY = 28


def _mm(a, b):
    return jnp.dot(a.astype(BF16), b.astype(BF16), preferred_element_type=F32)


def _mm_nt(a, b):
    return lax.dot_general(a.astype(BF16), b.astype(BF16), (((1,), (1,)), ((), ())), preferred_element_type=F32)


def _mm_tn(a, b):
    return lax.dot_general(a.astype(BF16), b.astype(BF16), (((0,), (0,)), ((), ())), preferred_element_type=F32)


def _mm_hi(a, b):
    return jnp.dot(a, b, precision=HIGHEST, preferred_element_type=F32)


def _mm_nt_hi(a, b):
    return lax.dot_general(a, b, (((1,), (1,)), ((), ())), precision=HIGHEST, preferred_element_type=F32)


def _split(a):
    hi = a.astype(BF16)
    return hi, (a - hi.astype(F32)).astype(BF16)


def _mm3(a, b):
    (ah, al), (bh, bl) = a, b
    dot = functools.partial(jnp.dot, preferred_element_type=F32)
    return dot(ah, bh) + (dot(al, bh) + dot(ah, bl))


def _head_rows(first_lane):
    return (_iota((SUBLANES, LANES), 1) == _iota((SUBLANES, LANES), 0) + first_lane).astype(F32)


def _sigmoid(x):
    return 1.0 / (1.0 + jnp.exp(-x))


def _silu(x):
    return x * _sigmoid(x)


def _softplus(x):
    return jnp.maximum(x, 0.0) + jnp.log1p(jnp.exp(-jnp.abs(x)))


def _iota(shape, dim):
    return lax.broadcasted_iota(jnp.int32, shape, dim)


def _tril(n, strict=False):
    r, c = _iota((n, n), 0), _iota((n, n), 1)
    return (r > c) if strict else (r >= c)


def _eye(n):
    return (_iota((n, n), 0) == _iota((n, n), 1)).astype(F32)


def _row_to_col(row):
    n = row.shape[1]
    return jnp.sum(_eye(n) * row, axis=1, keepdims=True)


def _group_rmsnorm(y, width):
    parts = []
    for g in range(y.shape[1] // width):
        yg = y[:, g * width:(g + 1) * width]
        parts.append(yg * lax.rsqrt(jnp.mean(yg * yg, axis=1, keepdims=True) + EPS))
    return jnp.concatenate(parts, axis=1)


def _causal_conv(x, cbuf, w, rows):
    cbuf[SUBLANES:SUBLANES + rows, :] = x
    y = (w[0:1, :] * cbuf[5:5 + rows, :] + w[1:2, :] * cbuf[6:6 + rows, :]
         + w[2:3, :] * cbuf[7:7 + rows, :] + w[3:4, :] * x)
    cbuf[0:SUBLANES, :] = cbuf[rows:rows + SUBLANES, :]
    return y


def _params(*sem, vmem_limit_bytes=VMEM_LIMIT_BYTES):
    return pltpu.CompilerParams(dimension_semantics=sem, vmem_limit_bytes=vmem_limit_bytes)


class _Layered:
    def __init__(self, init, src, dst, depth, prev):
        self.init, self.src, self.dst, self.depth, self.prev = init, src, dst, depth, prev

    def in_spec(self, bb):
        tail = self.init.shape[2:]
        return pl.BlockSpec((None, bb) + tail, lambda bi, i, s=self.src, z=(0,) * len(tail): (s, bi) + z)

    def out_spec(self, bb):
        tail = self.init.shape[2:]
        return pl.BlockSpec((None, bb) + tail, lambda bi, i, d=self.dst, z=(0,) * len(tail): (d, bi) + z)

    def out_shape(self):
        return jax.ShapeDtypeStruct((self.depth,) + self.init.shape[1:], F32)


def _mixer_call(body, *, grid, in_specs, inputs, out_specs, out_shape, states, first_state_out, scratch, name):
    n_in = len(inputs)
    prevs = [s.prev for s in states if s.prev is not None]
    aliases = {}
    fn = body
    if prevs:
        assert len(prevs) == len(states)
        in_specs = list(in_specs) + [pl.BlockSpec(memory_space=pl.ANY)] * len(prevs)
        aliases = {n_in + j: first_state_out + j for j in range(len(prevs))}

        def fn(*refs):
            return body(*refs[:n_in], *refs[n_in + len(prevs):])

    return pl.pallas_call(
        fn, grid=grid, in_specs=in_specs, out_specs=out_specs, out_shape=out_shape, scratch_shapes=scratch,
        input_output_aliases=aliases, compiler_params=_params("arbitrary", "arbitrary"), name=name,
    )(*inputs, *prevs)


def _ffn_kernel(x_ref, g_ref, wg_ref, wu_ref, wd_ref, fg_ref, o_ref, xn_ref, *, n_f, tf, d_ff, final_norm):
    f = pl.program_id(1)

    @pl.when(f == 0)
    def _():
        x = x_ref[...]
        xn = x * lax.rsqrt(jnp.mean(x * x, axis=1, keepdims=True) + EPS) * g_ref[...]
        xn_ref[...] = xn.astype(BF16)
        o_ref[...] = jnp.zeros_like(o_ref)

    wg, wu, wd = wg_ref[0].astype(BF16), wu_ref[0].astype(BF16), wd_ref[0].astype(BF16)
    keep = None
    if d_ff % tf:
        repeat = f * tf - jnp.minimum(f * tf, d_ff - tf)
        keep = _iota((1, tf), 1) >= repeat
    xn = xn_ref[...]
    gate = jnp.dot(xn, wg, preferred_element_type=F32)
    up = jnp.dot(xn, wu, preferred_element_type=F32)
    h = _silu(gate) * up
    if keep is not None:
        h = jnp.where(keep, h, 0.0)
    o_ref[...] += jnp.dot(h.astype(BF16), wd, preferred_element_type=F32)

    @pl.when(f == n_f - 1)
    def _():
        y = x_ref[...] + 0.5 * o_ref[...]
        if final_norm:
            y = y * lax.rsqrt(jnp.mean(y * y, axis=1, keepdims=True) + EPS) * fg_ref[...]
        o_ref[...] = y


def _ffn(x, g, wg, wu, wd, layer, fg, final_norm):
    m = x.shape[0]
    d_ff = wg.shape[2]
    tm = min(FFN_TOKEN_TILE, m)
    tf = FFN_HIDDEN_TILE if m >= FFN_TOKEN_TILE else FFN_HIDDEN_TILE_SMALL_M
    n_f = pl.cdiv(d_ff, tf)
    assert m % tm == 0 and d_ff % LANES == 0 and tf % LANES == 0 and d_ff >= tf
    start = lambda f: pl.multiple_of(jnp.minimum(f * tf, d_ff - tf), LANES)
    return pl.pallas_call(
        functools.partial(_ffn_kernel, n_f=n_f, tf=tf, d_ff=d_ff, final_norm=final_norm),
        grid=(m // tm, n_f),
        in_specs=[
            pl.BlockSpec((tm, D_MODEL), lambda i, f: (i, 0)),
            pl.BlockSpec((1, D_MODEL), lambda i, f: (0, 0)),
            pl.BlockSpec((pl.Element(1), pl.Element(D_MODEL), pl.Element(tf)), lambda i, f: (layer, 0, start(f))),
            pl.BlockSpec((pl.Element(1), pl.Element(D_MODEL), pl.Element(tf)), lambda i, f: (layer, 0, start(f))),
            pl.BlockSpec((pl.Element(1), pl.Element(tf), pl.Element(D_MODEL)), lambda i, f: (layer, start(f), 0)),
            pl.BlockSpec((1, D_MODEL), lambda i, f: (0, 0)),
        ],
        out_specs=pl.BlockSpec((tm, D_MODEL), lambda i, f: (i, 0)),
        out_shape=jax.ShapeDtypeStruct((m, D_MODEL), F32),
        scratch_shapes=[pltpu.VMEM((tm, D_MODEL), BF16)],
        compiler_params=_params("arbitrary", "arbitrary", vmem_limit_bytes=FFN_VMEM_LIMIT_BYTES),
        name="ffn",
    )(x, g, wg, wu, wd, fg)


def _in_proj_kernel(x_ref, g_ref, w_ref, o_ref, xn_ref):
    @pl.when(pl.program_id(1) == 0)
    def _():
        x = x_ref[...]
        xn = x * lax.rsqrt(jnp.mean(x * x, axis=1, keepdims=True) + EPS) * g_ref[...]
        xn_ref[...] = xn.astype(BF16)

    o_ref[...] = jnp.dot(xn_ref[...], w_ref[...], preferred_element_type=F32)


def _in_proj(x, g, w):
    m = x.shape[0]
    tm = min(IN_PROJ_TOKEN_TILE, m)
    tn = IN_PROJ_COL_TILE
    assert m % tm == 0 and PROJ_COLS % tn == 0
    return pl.pallas_call(
        _in_proj_kernel,
        grid=(m // tm, PROJ_COLS // tn),
        in_specs=[
            pl.BlockSpec((tm, D_MODEL), lambda i, n: (i, 0)),
            pl.BlockSpec((1, D_MODEL), lambda i, n: (0, 0)),
            pl.BlockSpec((D_MODEL, tn), lambda i, n: (0, n)),
        ],
        out_specs=pl.BlockSpec((tm, tn), lambda i, n: (i, n)),
        out_shape=jax.ShapeDtypeStruct((m, PROJ_COLS), F32),
        scratch_shapes=[pltpu.VMEM((tm, D_MODEL), BF16)],
        compiler_params=_params("arbitrary", "arbitrary"),
        name="in_proj",
    )(x, g, w)


def _out_proj_kernel(x_ref, y0_ref, y1_ref, y2_ref, y3_ref, w_ref, o_ref):
    acc = x_ref[...]
    for j, y_ref in enumerate((y0_ref, y1_ref, y2_ref, y3_ref)):
        acc = acc + jnp.dot(y_ref[...].astype(BF16), w_ref[j * D_GROUP:(j + 1) * D_GROUP, :],
                            preferred_element_type=F32)
    o_ref[...] = acc


def _out_proj(x, ys, w):
    m = x.shape[0]
    tm = min(OUT_PROJ_TOKEN_TILE, m)
    assert m % tm == 0
    yspec = pl.BlockSpec((tm, D_GROUP), lambda i: (i, 0))
    return pl.pallas_call(
        _out_proj_kernel,
        grid=(m // tm,),
        in_specs=[pl.BlockSpec((tm, D_MODEL), lambda i: (i, 0)), yspec, yspec, yspec, yspec,
                  pl.BlockSpec((D_MODEL, D_MODEL), lambda i: (0, 0))],
        out_specs=pl.BlockSpec((tm, D_MODEL), lambda i: (i, 0)),
        out_shape=jax.ShapeDtypeStruct((m, D_MODEL), F32),
        compiler_params=_params("arbitrary"),
        name="out_proj",
    )(x, *ys, w)


W_IN_SEGMENTS = ((512, COL_XBC, 1024), (0, COL_SSD_Z, 512), (3608, COL_GDN_QKV, 1536), (1544, COL_S5_U, 512),
                 (2568, COL_GLA_V, 512), (3096, COL_GLA_R, 512), (5144, COL_GDN_Z, 512), (2056, COL_GLA_Q, 256),
                 (2312, COL_GLA_K, 256), (1536, COL_SMALL + SM_DT, 8), (3080, COL_SMALL + SM_GR, 16),
                 (5656, COL_SMALL + SM_BETA, 8))


def _permute_w_in_kernel(wt_ref, o_ref):
    tr = o_ref.shape[0]
    eye = _eye(tr).astype(BF16)

    def transposed(rows_bf16):
        return lax.dot_general(eye, rows_bf16, (((1,), (1,)), ((), ())), preferred_element_type=F32).astype(BF16)

    small = []
    for src, dst, width in W_IN_SEGMENTS:
        if width % LANES == 0:
            o_ref[:, dst:dst + width] = transposed(wt_ref[src:src + width, :].astype(BF16))
        else:
            small.append(wt_ref[src:src + width, :])
    used = sum(s.shape[0] for s in small)
    small.append(jnp.zeros((LANES - used, tr), F32))
    o_ref[:, COL_SMALL:COL_SMALL + LANES] = transposed(jnp.concatenate(small, axis=0).astype(BF16))


def _permute_w_in(w, layer):
    _, rows, cols = w.shape
    tr = MXU_WIDTH
    return pl.pallas_call(
        _permute_w_in_kernel,
        grid=(rows // tr,),
        in_specs=[pl.BlockSpec((None, cols, tr), lambda r: (layer, 0, r))],
        out_specs=pl.BlockSpec((tr, PROJ_COLS), lambda r: (r, 0)),
        out_shape=jax.ShapeDtypeStruct((rows, PROJ_COLS), BF16),
        compiler_params=_params("arbitrary"),
        name="permute_w_in",
    )(jnp.swapaxes(w, 1, 2))


def _ssd_kernel(xbc_ref, z_ref, sm_ref, conv0_ref, h0_ref, cw_ref, cb_ref, dtb_ref, alog_ref, dex_ref, nrm_ref,
                y_ref, h_ref, cbuf, abuf, *, bb, rows, chunk, valid):
    @pl.when(pl.program_id(1) == 0)
    def _():
        cbuf[:, 0:SUBLANES, :] = conv0_ref[...]
        h_ref[...] = h0_ref[...]

    a_neg = -jnp.exp(alog_ref[...])
    tril = _tril(chunk).astype(F32)
    causal = _tril(chunk)
    left = _iota((1, LANES), 1) < SSD_HEAD_DIM
    top = _iota((LANES, 1), 0) < SSD_HEAD_DIM
    head_rows = _head_rows(SM_DT)

    def pair(v, h0):
        return jnp.where(left, v[:, h0:h0 + 1], v[:, h0 + 1:h0 + 2])

    dt_all = []
    for b in range(bb):
        conv = _causal_conv(xbc_ref[b], cbuf.at[b], cw_ref[...], rows) + cb_ref[...]
        abuf[b] = _silu(conv)
        dt = _softplus(sm_ref[b] + dtb_ref[...])
        if valid < rows:
            dt = jnp.where(_iota((rows, LANES), 0) < valid, dt, 0.0)
        dt_all.append(dt)

    n_c = rows // chunk
    seqs = [(b, c) for b in range(bb) for c in range(n_c)]
    dt = {(b, c): dt_all[b][c * chunk:(c + 1) * chunk, :] for (b, c) in seqs}
    acs = {s: _mm_hi(tril, dt[s] * a_neg) for s in seqs}
    acs_t = {s: _mm_nt_hi(head_rows, acs[s]) for s in seqs}
    groups = [(b, c, g) for (b, c) in seqs for g in range(SSD_GROUPS)]
    bm, cm = {}, {}
    for (b, c, g) in groups:
        r = slice(c * chunk, (c + 1) * chunk)
        bm[b, c, g] = abuf[b, r, D_GROUP + g * SSD_STATE:D_GROUP + (g + 1) * SSD_STATE]
        cm[b, c, g] = abuf[b, r, D_GROUP + (SSD_GROUPS + g) * SSD_STATE:D_GROUP + (SSD_GROUPS + g + 1) * SSD_STATE]
    cb = {x: _mm_nt(cm[x], bm[x]) for x in groups}
    pairs = [(b, c, g, j) for (b, c, g) in groups for j in range(2)]
    xdt, y_diag = {}, {}
    for (b, c, g, j) in pairs:
        h0 = 4 * g + 2 * j
        lo = (2 * g + j) * LANES
        xdt[b, c, g, j] = abuf[b, c * chunk:(c + 1) * chunk, lo:lo + LANES] * pair(dt[b, c], h0)
    for (b, c, g, j) in pairs:
        h0 = 4 * g + 2 * j
        total = None
        for hh, keep in ((h0, left), (h0 + 1, jnp.logical_not(left))):
            diff = acs[b, c][:, hh:hh + 1] - acs_t[b, c][hh:hh + 1, :]
            decay = jnp.exp(jnp.where(causal, diff, -jnp.inf))
            part = _mm(cb[b, c, g] * decay, jnp.where(keep, xdt[b, c, g, j], 0.0))
            total = part if total is None else total + part
        y_diag[b, c, g, j] = total

    for c in range(n_c):
        r0 = c * chunk
        for b in range(bb):
            a_c = acs[b, c]
            exp_acs = jnp.exp(a_c)
            to_end = jnp.exp(a_c[chunk - 1:chunk, :] - a_c)
            end_decay = jnp.exp(a_c[chunk - 1:chunk, :])
            for g in range(SSD_GROUPS):
                for j in range(2):
                    h0 = 4 * g + 2 * j
                    lo = (2 * g + j) * LANES
                    st = h_ref[b, lo:lo + LANES, :]
                    y_ref[b, r0:r0 + chunk, lo:lo + LANES] = (
                        y_diag[b, c, g, j] + _mm_nt(cm[b, c, g], st) * pair(exp_acs, h0))
                    dec = jnp.where(top, end_decay[:, h0:h0 + 1], end_decay[:, h0 + 1:h0 + 2])
                    h_ref[b, lo:lo + LANES, :] = st * dec + _mm_tn(xdt[b, c, g, j] * pair(to_end, h0), bm[b, c, g])

    for b in range(bb):
        y = y_ref[b] + dex_ref[...] * abuf[b, :, 0:D_GROUP]
        y = y * _silu(z_ref[b])
        y_ref[b] = _group_rmsnorm(y, D_GROUP // SSD_GROUPS) * nrm_ref[...]


def _ssd(proj, conv0, state, p, *, bb, rows, chunk, valid):
    b, length, _ = proj.shape
    w_xbc = D_GROUP + 2 * SSD_GROUPS * SSD_STATE
    vec = lambda n: pl.BlockSpec((1, n), lambda bi, i: (0, 0))
    return _mixer_call(
        functools.partial(_ssd_kernel, bb=bb, rows=rows, chunk=chunk, valid=valid),
        grid=(b // bb, length // rows),
        in_specs=[
            pl.BlockSpec((bb, rows, w_xbc), lambda bi, i: (bi, i, COL_XBC // w_xbc)),
            pl.BlockSpec((bb, rows, D_GROUP), lambda bi, i: (bi, i, COL_SSD_Z // D_GROUP)),
            pl.BlockSpec((bb, rows, LANES), lambda bi, i: (bi, i, COL_SMALL // LANES)),
            pl.BlockSpec((bb, SUBLANES, w_xbc), lambda bi, i: (bi, 0, 0)),
            state.in_spec(bb),
            pl.BlockSpec((CONV_W, w_xbc), lambda bi, i: (0, 0)),
            vec(w_xbc), vec(LANES), vec(LANES), vec(D_GROUP), vec(D_GROUP),
        ],
        inputs=[proj, proj, proj, conv0, state.init, p["ssd_conv_w"], p["ssd_conv_b"], p["ssd_dt_bias"],
                p["ssd_a_log"], p["ssd_d"], p["ssd_norm"]],
        out_specs=[pl.BlockSpec((bb, rows, D_GROUP), lambda bi, i: (bi, i, 0)), state.out_spec(bb)],
        out_shape=[jax.ShapeDtypeStruct((b, length, D_GROUP), F32), state.out_shape()],
        states=[state], first_state_out=1,
        scratch=[pltpu.VMEM((bb, rows + SUBLANES, w_xbc), F32), pltpu.VMEM((bb, rows, w_xbc), F32)],
        name="ssd")


def _s5_kernel(u_ref, h0re_ref, h0im_ref, apre_ref, apim_ref, wre_ref, wim_ref, cre_ref, cim_ref, d_ref, wglu_ref,
               bglu_ref, y_ref, hre_ref, him_ref, ubuf, *carry, bb, rows, valid):
    first = pl.program_id(1) == 0
    m = bb * rows
    n_blk = D_GROUP // LANES
    w_blk = S5_LANES // n_blk
    n_tiles = rows // SUBLANES

    @pl.when(first)
    def _():
        ubuf[:, 0:SUBLANES, :] = jnp.zeros((bb, SUBLANES, D_GROUP), F32)
        for buf in carry[2:]:
            buf[...] = jnp.zeros(buf.shape, F32)

    ubuf[:, SUBLANES:SUBLANES + rows, :] = u_ref[...]

    half = LANES // 2
    w_half = w_blk // 2
    left = _iota((1, LANES), 1) < half
    parts = []
    for j in range(n_blk):
        taps = [ubuf[:, SUBLANES - t:SUBLANES - t + rows, j * LANES:(j + 1) * LANES].reshape(m, LANES)
                for t in range(S5_TAPS)]
        packed = ([], [])
        for t in range(0, S5_TAPS, 2):
            packed[0].append(jnp.where(left, taps[t], pltpu.roll(taps[t + 1], half, 1)))
            packed[1].append(jnp.where(left, pltpu.roll(taps[t], half, 1), taps[t + 1]))
        y_j = None
        for hf in range(2):
            k = 2 * j + hf
            sl = slice(k * w_half, (k + 1) * w_half)
            lagged = jnp.concatenate(packed[hf], axis=1).astype(BF16)
            xr = jnp.dot(lagged, wre_ref[k], preferred_element_type=F32)
            xi = jnp.dot(lagged, wim_ref[k], preferred_element_type=F32)
            if n_tiles > 1:
                hbuf_re, hbuf_im, tail_re, tail_im = carry
                ar, ai = apre_ref[S5_TAPS - 1:S5_TAPS, sl], apim_ref[S5_TAPS - 1:S5_TAPS, sl]
                pr, pi = tail_re[:, sl], tail_im[:, sl]
                for q in range(n_tiles):
                    rs = slice(q * SUBLANES, (q + 1) * SUBLANES)
                    pr, pi = xr[rs, :] + ar * pr - ai * pi, xi[rs, :] + ar * pi + ai * pr
                    hbuf_re[rs, :] = pr
                    hbuf_im[rs, :] = pi

                @pl.when(first)
                def _():
                    h0r, h0i = h0re_ref[0, :, sl], h0im_ref[0, :, sl]
                    apr, api = apre_ref[:, sl], apim_ref[:, sl]
                    hbuf_re[...] = hbuf_re[...] + apr * h0r - api * h0i
                    hbuf_im[...] = hbuf_im[...] + apr * h0i + api * h0r

                hr, hi = hbuf_re[...], hbuf_im[...]
                tail_re[:, sl] = hr[rows - SUBLANES:rows, :]
                tail_im[:, sl] = hi[rows - SUBLANES:rows, :]
                hre_ref[0, :, sl] = hr[valid - 1:valid, :]
                him_ref[0, :, sl] = hi[valid - 1:valid, :]
            else:
                h0r, h0i = h0re_ref[:, :, sl], h0im_ref[:, :, sl]
                apr, api = apre_ref[:, sl][None], apim_ref[:, sl][None]
                hr3 = xr.reshape(bb, rows, w_half) + apr * h0r - api * h0i
                hi3 = xi.reshape(bb, rows, w_half) + apr * h0i + api * h0r
                hre_ref[:, :, sl] = hr3[:, valid - 1:valid, :]
                him_ref[:, :, sl] = hi3[:, valid - 1:valid, :]
                hr, hi = hr3.reshape(m, w_half), hi3.reshape(m, w_half)
            rows_c = slice(hf * w_half, (hf + 1) * w_half)
            y_half = _mm(hr, cre_ref[j, rows_c, :]) - _mm(hi, cim_ref[j, rows_c, :])
            y_j = y_half if y_j is None else y_j + y_half
        parts.append(y_j)

    if n_tiles > 1:
        ubuf[:, 0:SUBLANES, :] = ubuf[:, rows:rows + SUBLANES, :]

    y = jnp.concatenate(parts, axis=1) + d_ref[...] * u_ref[...].reshape(m, D_GROUP)
    y = 0.5 * y * (1.0 + jnp.tanh(math.sqrt(2.0 / math.pi) * (y + 0.044715 * (y * y * y))))
    y = y * _sigmoid(_mm(y, wglu_ref[...]) + bglu_ref[...])
    y_ref[...] = y.reshape(bb, rows, D_GROUP)


def _s5_lag_kernel(pre_ref, pim_ref, bre_ref, bim_ref, ore_ref, oim_ref):
    br, bi = bre_ref[...], bim_ref[...]
    kb = br.shape[0]
    for t in range(S5_TAPS):
        pr, pi = pre_ref[t:t + 1, :], pim_ref[t:t + 1, :]
        ore_ref[t * kb:(t + 1) * kb, :] = (pr * br - pi * bi).astype(BF16)
        oim_ref[t * kb:(t + 1) * kb, :] = (pr * bi + pi * br).astype(BF16)


def _s5_lag_weights(pow_re, pow_im, b_re, b_im):
    n_blk, kb, w_blk = b_re.shape
    pw = pl.BlockSpec((S5_TAPS, w_blk), lambda j: (0, j))
    bs = pl.BlockSpec((None, kb, w_blk), lambda j: (j, 0, 0))
    out = pl.BlockSpec((None, S5_TAPS * kb, w_blk), lambda j: (j, 0, 0))
    shape = jax.ShapeDtypeStruct((n_blk, S5_TAPS * kb, w_blk), BF16)
    return pl.pallas_call(
        _s5_lag_kernel, grid=(n_blk,), in_specs=[pw, pw, bs, bs], out_specs=[out, out], out_shape=[shape, shape],
        compiler_params=_params("arbitrary"), name="s5_lag_weights",
    )(pow_re, pow_im, b_re, b_im)


def _s5(proj, state_re, state_im, p, *, bb, rows, valid):
    b, length, _ = proj.shape
    assert rows == S5_TAPS or bb == 1
    n_blk = D_GROUP // LANES
    w_blk = S5_LANES // n_blk
    vec = lambda n: pl.BlockSpec((1, n), lambda bi, i: (0, 0))
    full3 = lambda s: pl.BlockSpec(s, lambda bi, i: (0, 0, 0))
    w_half = w_blk // 2
    carry = []
    if rows > S5_TAPS:
        carry = [pltpu.VMEM((rows, w_half), F32)] * 2 + [pltpu.VMEM((SUBLANES, S5_LANES), F32)] * 2
    return _mixer_call(
        functools.partial(_s5_kernel, bb=bb, rows=rows, valid=valid),
        grid=(b // bb, length // rows),
        in_specs=[
            pl.BlockSpec((bb, rows, D_GROUP), lambda bi, i: (bi, i, COL_S5_U // D_GROUP)),
            state_re.in_spec(bb), state_im.in_spec(bb),
            pl.BlockSpec((rows, S5_LANES), lambda bi, i: (0, 0)),
            pl.BlockSpec((rows, S5_LANES), lambda bi, i: (0, 0)),
            full3((2 * n_blk, S5_TAPS * LANES // 2, w_half)), full3((2 * n_blk, S5_TAPS * LANES // 2, w_half)),
            full3((n_blk, w_blk, LANES)), full3((n_blk, w_blk, LANES)),
            vec(D_GROUP), pl.BlockSpec((D_GROUP, D_GROUP), lambda bi, i: (0, 0)), vec(D_GROUP),
        ],
        inputs=[proj, state_re.init, state_im.init, p["s5_apow_re"][:rows], p["s5_apow_im"][:rows], p["s5_wlag_re"],
                p["s5_wlag_im"], p["s5_cblk_re"], p["s5_cblk_im"], p["s5_d"], p["s5_w_glu"], p["s5_b_glu"]],
        out_specs=[pl.BlockSpec((bb, rows, D_GROUP), lambda bi, i: (bi, i, 0)),
                   state_re.out_spec(bb), state_im.out_spec(bb)],
        out_shape=[jax.ShapeDtypeStruct((b, length, D_GROUP), F32), state_re.out_shape(), state_im.out_shape()],
        states=[state_re, state_im], first_state_out=1,
        scratch=[pltpu.VMEM((bb, SUBLANES + rows, D_GROUP), F32)] + carry,
        name="s5")


def _gla_kernel(q_ref, k_ref, v_ref, r_ref, sm_ref, s0_ref, wg_ref, bg_ref, nrm_ref, o_ref, s_ref,
                *, bb, rows, chunk, valid):
    @pl.when(pl.program_id(1) == 0)
    def _():
        s_ref[...] = s0_ref[...]

    hk = GLA_HEADS * GLA_DK
    tril = _tril(chunk).astype(F32)
    causal = _tril(chunk)
    lane = _iota((1, hk), 1)
    lg_all, k_all = [], []
    for b in range(bb):
        lg = -_softplus(-(_mm(sm_ref[b], wg_ref[...]) + bg_ref[...])) * (1.0 / GLA_GATE_TEMP)
        k = k_ref[b]
        if valid < rows:
            live = _iota((rows, hk), 0) < valid
            lg = jnp.where(live, lg, 0.0)
            k = jnp.where(live, k, 0.0)
        lg_all.append(lg)
        k_all.append(k)

    n_c = rows // chunk
    seqs = [(b, c) for b in range(bb) for c in range(n_c)]
    cum = {(b, c): _mm_hi(tril, lg_all[b][c * chunk:(c + 1) * chunk, :]) for (b, c) in seqs}
    q_dec, k_dec, k_end, end_decay = {}, {}, {}, {}
    for (b, c) in seqs:
        r = slice(c * chunk, (c + 1) * chunk)
        cs = cum[b, c]
        cum_last = cs[chunk - 1:chunk, :]
        q_dec[b, c] = q_ref[b, r, :] * (GLA_DK ** -0.5) * jnp.exp(cs)
        k = k_all[b][r, :]
        k_dec[b, c] = k * jnp.exp(-cs)
        k_end[b, c] = k * jnp.exp(cum_last - cs)
        end_decay[b, c] = _row_to_col(jnp.exp(cum_last))
    units = [(b, c, h) for (b, c) in seqs for h in range(GLA_HEADS)]
    qh, vh = {}, {}
    for (b, c, h) in units:
        mine = (lane >= h * GLA_DK) & (lane < (h + 1) * GLA_DK)
        qh[b, c, h] = jnp.where(mine, q_dec[b, c], 0.0)
        vh[b, c, h] = v_ref[b, c * chunk:(c + 1) * chunk, h * GLA_DV:(h + 1) * GLA_DV]
    att = {x: jnp.where(causal, _mm_nt(qh[x], k_dec[x[:2]]), 0.0) for x in units}
    o_intra = {x: _mm(att[x], vh[x]) for x in units}
    kv = {x: _mm_tn(k_end[x[:2]], vh[x])[x[2] * GLA_DK:(x[2] + 1) * GLA_DK, :] for x in units}

    for c in range(n_c):
        for b in range(bb):
            st = s_ref[b]
            for h in range(GLA_HEADS):
                rs = slice(h * GLA_DK, (h + 1) * GLA_DK)
                o_ref[b, c * chunk:(c + 1) * chunk, h * GLA_DV:(h + 1) * GLA_DV] = (
                    o_intra[b, c, h] + _mm(qh[b, c, h], st))
                s_ref[b, rs, :] = st[rs, :] * end_decay[b, c][rs, :] + kv[b, c, h]

    for b in range(bb):
        o = _group_rmsnorm(o_ref[b], GLA_DV) * nrm_ref[...]
        o_ref[b] = o * _silu(r_ref[b])


def _gla(proj, state, p, *, bb, rows, chunk, valid):
    b, length, _ = proj.shape
    hk = GLA_HEADS * GLA_DK
    vec = lambda n: pl.BlockSpec((1, n), lambda bi, i: (0, 0))
    return _mixer_call(
        functools.partial(_gla_kernel, bb=bb, rows=rows, chunk=chunk, valid=valid),
        grid=(b // bb, length // rows),
        in_specs=[
            pl.BlockSpec((bb, rows, hk), lambda bi, i: (bi, i, COL_GLA_Q // hk)),
            pl.BlockSpec((bb, rows, hk), lambda bi, i: (bi, i, COL_GLA_K // hk)),
            pl.BlockSpec((bb, rows, D_GROUP), lambda bi, i: (bi, i, COL_GLA_V // D_GROUP)),
            pl.BlockSpec((bb, rows, D_GROUP), lambda bi, i: (bi, i, COL_GLA_R // D_GROUP)),
            pl.BlockSpec((bb, rows, LANES), lambda bi, i: (bi, i, COL_SMALL // LANES)),
            state.in_spec(bb),
            pl.BlockSpec((LANES, hk), lambda bi, i: (0, 0)), vec(hk), vec(D_GROUP),
        ],
        inputs=[proj, proj, proj, proj, proj, state.init, p["gla_w_gate2"], p["gla_b_gate2"], p["gla_norm"]],
        out_specs=[pl.BlockSpec((bb, rows, D_GROUP), lambda bi, i: (bi, i, 0)), state.out_spec(bb)],
        out_shape=[jax.ShapeDtypeStruct((b, length, D_GROUP), F32), state.out_shape()],
        states=[state], first_state_out=1, scratch=[], name="gla")


def _gdn_kernel(qkv_ref, z_ref, sm_ref, conv0_ref, s0_ref, cw_ref, alog_ref, dtb_ref, nrm_ref, o_ref, s_ref,
                cbuf, abuf, *, bb, rows, chunk, valid):
    @pl.when(pl.program_id(1) == 0)
    def _():
        cbuf[:, 0:SUBLANES, :] = conv0_ref[...]
        s_ref[...] = s0_ref[...]

    a_neg = -jnp.exp(alog_ref[...])
    tril = _tril(chunk).astype(F32)
    causal = _tril(chunk)
    strict = _tril(chunk, strict=True)
    eye_c = _eye(chunk)
    gate_rows = _head_rows(SM_BETA)
    hd = GDN_HEADS * GDN_DK
    n_c = rows // chunk

    def l2n(x):
        return x * lax.rsqrt(jnp.sum(x * x, axis=1, keepdims=True) + EPS)

    beta_all, g_all = [], []
    for b in range(bb):
        abuf[b] = _silu(_causal_conv(qkv_ref[b], cbuf.at[b], cw_ref[...], rows))
        sm = sm_ref[b]
        beta = _sigmoid(sm)
        g = a_neg * _softplus(sm + dtb_ref[...])
        if valid < rows:
            live = _iota((rows, LANES), 0) < valid
            beta = jnp.where(live, beta, 0.0)
            g = jnp.where(live, g, 0.0)
        beta_all.append(beta)
        g_all.append(g)

    seqs = [(b, c) for b in range(bb) for c in range(n_c)]
    gc = {s: _mm_hi(tril, g_all[s[0]][s[1] * chunk:(s[1] + 1) * chunk, :]) for s in seqs}
    gc_t = {s: _mm_nt_hi(gate_rows, gc[s]) for s in seqs}

    units = [(b, c, h) for (b, c) in seqs for h in range(GDN_HEADS)]
    q, k, v, g_col, b_col, decay = {}, {}, {}, {}, {}, {}
    for (b, c, h) in units:
        r = slice(c * chunk, (c + 1) * chunk)
        q[b, c, h] = l2n(abuf[b, r, h * GDN_DK:(h + 1) * GDN_DK]) * (GDN_DK ** -0.5)
        k[b, c, h] = l2n(abuf[b, r, hd + h * GDN_DK:hd + (h + 1) * GDN_DK])
        v[b, c, h] = abuf[b, r, 2 * hd + h * GDN_DK:2 * hd + (h + 1) * GDN_DK]
        g_col[b, c, h] = gc[b, c][:, SM_DECAY + h:SM_DECAY + h + 1]
        g_row = gc_t[b, c][SM_DECAY - SM_BETA + h:SM_DECAY - SM_BETA + h + 1, :]
        b_col[b, c, h] = beta_all[b][r, SM_BETA + h:SM_BETA + h + 1]
        decay[b, c, h] = jnp.exp(jnp.where(causal, g_col[b, c, h] - g_row, -jnp.inf))
    kk = {x: _mm_nt(k[x], k[x]) for x in units}
    npow = {x: jnp.where(strict, -(b_col[x] * kk[x] * decay[x]), 0.0) for x in units}
    t_mat = {x: eye_c + npow[x] for x in units}
    span = 2
    parts = {x: _split(npow[x]) for x in units}
    while span < chunk:
        parts = {x: _split(_mm3(parts[x], parts[x])) for x in units}
        t_mat = {x: t_mat[x] + _mm3(_split(t_mat[x]), parts[x]) for x in units}
        span *= 2
    uw = {x: _mm(t_mat[x], jnp.concatenate([v[x] * b_col[x], k[x] * (b_col[x] * jnp.exp(g_col[x]))], axis=1))
          for x in units}
    qk = {x: _mm_nt(q[x], k[x]) * decay[x] for x in units}
    q_dec = {x: q[x] * jnp.exp(g_col[x]) for x in units}

    for c in range(n_c):
        now = [(b, c, h) for b in range(bb) for h in range(GDN_HEADS)]
        st = {x: s_ref[x[0], x[2] * GDN_DK:(x[2] + 1) * GDN_DK, :] for x in now}
        v_new = {x: uw[x][:, :GDN_DK] - _mm(uw[x][:, GDN_DK:], st[x]) for x in now}
        for x in now:
            b, _, h = x
            g_last = gc[b, c][chunk - 1:chunk, SM_DECAY + h:SM_DECAY + h + 1]
            o_ref[b, c * chunk:(c + 1) * chunk, h * GDN_DK:(h + 1) * GDN_DK] = (
                _mm(q_dec[x], st[x]) + _mm(qk[x], v_new[x]))
            s_ref[b, h * GDN_DK:(h + 1) * GDN_DK, :] = (
                st[x] * jnp.exp(g_last) + _mm_tn(k[x] * jnp.exp(g_last - g_col[x]), v_new[x]))

    for b in range(bb):
        o = _group_rmsnorm(o_ref[b], GDN_DK) * nrm_ref[...]
        o_ref[b] = o * _silu(z_ref[b])


def _gdn(proj, conv0, state, p, *, bb, rows, chunk, valid):
    b, length, _ = proj.shape
    w_qkv = 3 * GDN_HEADS * GDN_DK
    vec = lambda n: pl.BlockSpec((1, n), lambda bi, i: (0, 0))
    return _mixer_call(
        functools.partial(_gdn_kernel, bb=bb, rows=rows, chunk=chunk, valid=valid),
        grid=(b // bb, length // rows),
        in_specs=[
            pl.BlockSpec((bb, rows, w_qkv), lambda bi, i: (bi, i, COL_GDN_QKV // w_qkv)),
            pl.BlockSpec((bb, rows, D_GROUP), lambda bi, i: (bi, i, COL_GDN_Z // D_GROUP)),
            pl.BlockSpec((bb, rows, LANES), lambda bi, i: (bi, i, COL_SMALL // LANES)),
            pl.BlockSpec((bb, SUBLANES, w_qkv), lambda bi, i: (bi, 0, 0)),
            state.in_spec(bb),
            pl.BlockSpec((CONV_W, w_qkv), lambda bi, i: (0, 0)),
            vec(LANES), vec(LANES), vec(D_GROUP),
        ],
        inputs=[proj, proj, proj, conv0, state.init, p["gdn_conv_w"], p["gdn_a_log"], p["gdn_dt_bias"], p["gdn_norm"]],
        out_specs=[pl.BlockSpec((bb, rows, D_GROUP), lambda bi, i: (bi, i, 0)), state.out_spec(bb)],
        out_shape=[jax.ShapeDtypeStruct((b, length, D_GROUP), F32), state.out_shape()],
        states=[state], first_state_out=1,
        scratch=[pltpu.VMEM((bb, rows + SUBLANES, w_qkv), F32), pltpu.VMEM((bb, rows, w_qkv), F32)],
        name="gdn")


def _lane_row(values, offset):
    return jnp.zeros((1, LANES), F32).at[0, offset:offset + values.shape[0]].set(values.astype(F32))


def _prep_layer(w, i, max_rows):
    row = lambda a: a[i].astype(F32).reshape(1, -1)
    p = {}
    for name in ("ffn1", "ffn2"):
        p[name + "_norm"] = row(w[name + "_norm"])
        for part in ("_w_gate", "_w_up", "_w_down"):
            p[name + part] = w[name + part].astype(F32)
    p["mix_norm"] = row(w["mix_norm"])

    p["w_in"] = _permute_w_in(w["w_in"].astype(F32), i)
    p["w_out"] = w["w_out"][i].astype(BF16)

    p["ssd_conv_w"] = w["ssd_conv_w"][i].astype(F32)
    p["ssd_conv_b"] = row(w["ssd_conv_b"])
    p["ssd_dt_bias"] = _lane_row(w["ssd_dt_bias"][i], SM_DT)
    p["ssd_a_log"] = _lane_row(w["ssd_a_log"][i], SM_DT)
    p["ssd_d"] = jnp.repeat(w["ssd_d"][i].astype(F32), SSD_HEAD_DIM).reshape(1, D_GROUP)
    p["ssd_norm"] = row(w["ssd_norm"])

    a_re, a_im = w["s5_a_re"][i].astype(F32), w["s5_a_im"][i].astype(F32)
    dt = jnp.exp(w["s5_log_dt"][i].astype(F32))[:, None]
    lam_re, lam_im = a_re * dt, a_im * dt

    def a_bar_pow(n):
        mag = jnp.exp(lam_re * n)
        return mag * jnp.cos(lam_im * n), mag * jnp.sin(lam_im * n)

    abar_re, abar_im = a_bar_pow(1.0)
    den = a_re * a_re + a_im * a_im
    coef_re = ((abar_re - 1.0) * a_re + abar_im * a_im) / den
    coef_im = (abar_im * a_re - (abar_re - 1.0) * a_im) / den
    b_re, b_im = w["s5_b_re"][i].astype(F32), w["s5_b_im"][i].astype(F32)
    bbar_re = coef_re[..., None] * b_re - coef_im[..., None] * b_im
    bbar_im = coef_re[..., None] * b_im + coef_im[..., None] * b_re
    steps = jnp.arange(1, max_rows + 1, dtype=F32)[:, None, None]
    apow_re, apow_im = a_bar_pow(steps)
    p["s5_apow_re"] = apow_re.reshape(max_rows, S5_LANES)
    p["s5_apow_im"] = apow_im.reshape(max_rows, S5_LANES)
    lag_re, lag_im = a_bar_pow(jnp.arange(S5_TAPS, dtype=F32)[:, None, None])
    n_blk = D_GROUP // LANES
    gpb = S5_GROUPS // n_blk
    eye = jnp.eye(gpb, dtype=F32)

    def b_blocks(x):
        half = gpb // 2
        x = jnp.swapaxes(x, 1, 2).reshape(2 * n_blk, half, S5_CH, S5_STATE)
        return jnp.einsum("jgip,gh->jgihp", x, eye[:half, :half]).reshape(2 * n_blk, LANES // 2, half * S5_STATE)

    def c_blocks(x):
        x = x.reshape(n_blk, gpb, S5_CH, S5_STATE)
        return jnp.einsum("jgip,gh->jgphi", x, eye).reshape(n_blk, gpb * S5_STATE, LANES).astype(BF16)

    p["s5_wlag_re"], p["s5_wlag_im"] = _s5_lag_weights(
        lag_re.reshape(S5_TAPS, S5_LANES), lag_im.reshape(S5_TAPS, S5_LANES), b_blocks(bbar_re), b_blocks(bbar_im))
    p["s5_cblk_re"] = c_blocks(w["s5_c_re"][i].astype(F32))
    p["s5_cblk_im"] = c_blocks(w["s5_c_im"][i].astype(F32))
    p["s5_d"] = row(w["s5_d"])
    p["s5_w_glu"] = w["s5_w_glu"][i].astype(BF16)
    p["s5_b_glu"] = row(w["s5_b_glu"])

    hk = GLA_HEADS * GLA_DK
    p["gla_w_gate2"] = jnp.zeros((LANES, hk), F32).at[SM_GR:SM_GR + GLA_GATE_RANK].set(w["gla_w_gate2"][i].astype(F32))
    p["gla_b_gate2"] = row(w["gla_b_gate2"])
    p["gla_norm"] = row(w["gla_norm"])

    p["gdn_conv_w"] = w["gdn_conv_w"][i].astype(F32)
    p["gdn_a_log"] = _lane_row(w["gdn_a_log"][i], SM_DECAY)
    p["gdn_dt_bias"] = _lane_row(w["gdn_dt_bias"][i], SM_DECAY)
    p["gdn_norm"] = row(w["gdn_norm"])
    return p


def _mixer_geometry(bsz, length):
    if length >= CHUNK:
        assert length % CHUNK == 0
        rows = MIXER_ROWS if length % MIXER_ROWS == 0 else CHUNK
        return (LONG_SEQS_PER_STEP if bsz % LONG_SEQS_PER_STEP == 0 else 1), rows, CHUNK, length
    assert length <= SUBLANES
    bb = SUBLANES if bsz % SUBLANES == 0 else 1
    return bb, SUBLANES, SUBLANES, SUBLANES


def _conv_buffer(state):
    return jnp.pad(state.astype(F32), ((0, 0), (SUBLANES - (CONV_W - 1), 0), (0, 0)))


def _trunk(x, states, layers, final_norm):
    bsz, length, _ = x.shape
    depth = len(layers)
    bb, rows, chunk, lpad = _mixer_geometry(bsz, length)
    geo = dict(bb=bb, rows=rows, chunk=chunk, valid=min(length, rows))
    h = x.reshape(bsz * length, D_MODEL).astype(F32)
    fg = final_norm.astype(F32).reshape(1, D_MODEL)
    ssd_conv, ssd_h, s5_re, s5_im, gla_s, gdn_conv, gdn_s = [s.astype(F32) for s in states]
    lead = ssd_h.shape[0]
    recurrent = [ssd_h.reshape(lead, bsz, D_GROUP, SSD_STATE), s5_re.reshape(lead, bsz, 1, S5_LANES),
                 s5_im.reshape(lead, bsz, 1, S5_LANES), gla_s.reshape(lead, bsz, GLA_HEADS * GLA_DK, GLA_DV),
                 gdn_s.reshape(lead, bsz, GDN_HEADS * GDN_DK, GDN_DK)]
    new = [jnp.zeros((depth,) + r.shape[1:], F32) for r in recurrent]
    ssd_conv_new, gdn_conv_new = [], []
    for i, p in enumerate(layers):
        src = min(i, lead - 1)
        st = [_Layered(init, src, i, depth, prev) for init, prev in zip(recurrent, new)]
        h = _ffn(h, p["ffn1_norm"], p["ffn1_w_gate"], p["ffn1_w_up"], p["ffn1_w_down"], i, fg, False)
        proj = _in_proj(h, p["mix_norm"], p["w_in"]).reshape(bsz, length, PROJ_COLS)
        ssd_conv_new.append(proj[:, length - (CONV_W - 1):, COL_XBC:COL_XBC + SSD_CONV_CH])
        gdn_conv_new.append(proj[:, length - (CONV_W - 1):, COL_GDN_QKV:COL_GDN_QKV + GDN_CONV_CH])
        if lpad != length:
            proj = jnp.pad(proj, ((0, 0), (0, lpad - length), (0, 0)))
        y_ssd, new[0] = _ssd(proj, _conv_buffer(ssd_conv[src]), st[0], p, **geo)
        s5_bb = bsz if rows == S5_TAPS else 1
        y_s5, new[1], new[2] = _s5(proj, st[1], st[2], p, bb=s5_bb, rows=rows, valid=geo["valid"])
        y_gla, new[3] = _gla(proj, st[3], p, **geo)
        y_gdn, new[4] = _gdn(proj, _conv_buffer(gdn_conv[src]), st[4], p, **geo)
        ys = [y[:, :length].reshape(bsz * length, D_GROUP) for y in (y_ssd, y_s5, y_gla, y_gdn)]
        h = _out_proj(h, ys, p["w_out"])
        h = _ffn(h, p["ffn2_norm"], p["ffn2_w_gate"], p["ffn2_w_up"], p["ffn2_w_down"], i, fg, i == depth - 1)
    out_states = (jnp.stack(ssd_conv_new), new[0].reshape(depth, bsz, SSD_HEADS, SSD_HEAD_DIM, SSD_STATE),
                  new[1].reshape(depth, bsz, S5_GROUPS, S5_STATE), new[2].reshape(depth, bsz, S5_GROUPS, S5_STATE),
                  new[3].reshape(depth, bsz, GLA_HEADS, GLA_DK, GLA_DV), jnp.stack(gdn_conv_new),
                  new[4].reshape(depth, bsz, GDN_HEADS, GDN_DK, GDN_DK))
    return h.reshape(bsz, length, D_MODEL), out_states


def kernel(x_prompt, x_sample, state_ssd_conv, state_ssd, state_s5_re, state_s5_im, state_gla, state_gdn_conv, state_gdn, ffn1_norm, ffn1_w_gate, ffn1_w_up, ffn1_w_down, mix_norm, w_in, ssd_conv_w, ssd_conv_b, ssd_dt_bias, ssd_a_log, ssd_d, ssd_norm, s5_a_re, s5_a_im, s5_log_dt, s5_b_re, s5_b_im, s5_c_re, s5_c_im, s5_d, s5_w_glu, s5_b_glu, gla_w_gate2, gla_b_gate2, gla_norm, gdn_conv_w, gdn_a_log, gdn_dt_bias, gdn_norm, w_out, ffn2_norm, ffn2_w_gate, ffn2_w_up, ffn2_w_down, final_norm):
    w = dict(ffn1_norm=ffn1_norm, ffn1_w_gate=ffn1_w_gate, ffn1_w_up=ffn1_w_up, ffn1_w_down=ffn1_w_down,
             mix_norm=mix_norm, w_in=w_in, ssd_conv_w=ssd_conv_w, ssd_conv_b=ssd_conv_b, ssd_dt_bias=ssd_dt_bias,
             ssd_a_log=ssd_a_log, ssd_d=ssd_d, ssd_norm=ssd_norm, s5_a_re=s5_a_re, s5_a_im=s5_a_im,
             s5_log_dt=s5_log_dt, s5_b_re=s5_b_re, s5_b_im=s5_b_im, s5_c_re=s5_c_re, s5_c_im=s5_c_im, s5_d=s5_d,
             s5_w_glu=s5_w_glu, s5_b_glu=s5_b_glu, gla_w_gate2=gla_w_gate2, gla_b_gate2=gla_b_gate2,
             gla_norm=gla_norm, gdn_conv_w=gdn_conv_w, gdn_a_log=gdn_a_log, gdn_dt_bias=gdn_dt_bias,
             gdn_norm=gdn_norm, w_out=w_out, ffn2_norm=ffn2_norm, ffn2_w_gate=ffn2_w_gate, ffn2_w_up=ffn2_w_up,
             ffn2_w_down=ffn2_w_down)
    depth = w_in.shape[0]
    max_rows = max(_mixer_geometry(*x_prompt.shape[:2])[1], _mixer_geometry(*x_sample.shape[:2])[1])
    layers = [_prep_layer(w, i, max_rows) for i in range(depth)]

    sample_states = (state_ssd_conv, state_ssd, state_s5_re, state_s5_im, state_gla, state_gdn_conv, state_gdn)
    n_prompt = x_prompt.shape[0]
    prompt_states = tuple(jnp.zeros((1, n_prompt) + s.shape[2:], F32) for s in sample_states)
    y_prompt, p_states = _trunk(x_prompt, prompt_states, layers, final_norm)
    y_sample, s_states = _trunk(x_sample, sample_states, layers, final_norm)
    return (y_prompt, y_sample) + p_states + s_states
```

```python
import functools
import math

import jax
import jax.numpy as jnp
from jax import lax
from jax.experimental import pallas as pl
from jax.experimental.pallas import tpu as pltpu

F32 = jnp.float32
BF16 = jnp.bfloat16
HIGHEST = lax.Precision.HIGHEST
EPS = 1e-6

LANES = 128
SUBLANES = 8
MXU_WIDTH = 256
VMEM_BYTES = 64 * 1024 * 1024
VMEM_LIMIT_BYTES = VMEM_BYTES - 8 * 1024 * 1024
FFN_VMEM_LIMIT_BYTES = VMEM_BYTES - 4 * 1024 * 1024

FFN_TOKEN_TILE = 1024
FFN_HIDDEN_TILE = MXU_WIDTH
FFN_HIDDEN_TILE_SMALL_M = 2 * MXU_WIDTH
IN_PROJ_TOKEN_TILE = 1024
IN_PROJ_COL_TILE = 1920
OUT_PROJ_TOKEN_TILE = 512
MIXER_ROWS = 256
LONG_SEQS_PER_STEP = 2

D_MODEL = 2048
D_GROUP = 512
CONV_W = 4
CHUNK = 64

SSD_HEADS = 8
SSD_HEAD_DIM = 64
SSD_GROUPS = 2
SSD_STATE = 128
S5_GROUPS = 32
S5_CH = 16
S5_STATE = 64
S5_LANES = S5_GROUPS * S5_STATE
S5_TAPS = 8
GLA_HEADS = 4
GLA_DK = 64
GLA_DV = 128
GLA_GATE_RANK = 16
GLA_GATE_TEMP = 16.0
GDN_HEADS = 4
GDN_DK = 128
SSD_CONV_CH = D_GROUP + 2 * SSD_GROUPS * SSD_STATE
GDN_CONV_CH = 3 * GDN_HEADS * GDN_DK

COL_XBC = 0
COL_SSD_Z = 1024
COL_GDN_QKV = 1536
COL_S5_U = 3072
COL_GLA_V = 3584
COL_GLA_R = 4096
COL_GDN_Z = 4608
COL_GLA_Q = 5120
COL_GLA_K = 5376
COL_SMALL = 5632
PROJ_COLS = 5760
SM_DT = 0
SM_GR = 8
SM_BETA = 24
SM_DECAY = 28


def _mm(a, b):
    return jnp.dot(a.astype(BF16), b.astype(BF16), preferred_element_type=F32)


def _mm_nt(a, b):
    return lax.dot_general(a.astype(BF16), b.astype(BF16), (((1,), (1,)), ((), ())), preferred_element_type=F32)


def _mm_tn(a, b):
    return lax.dot_general(a.astype(BF16), b.astype(BF16), (((0,), (0,)), ((), ())), preferred_element_type=F32)


def _mm_hi(a, b):
    return jnp.dot(a, b, precision=HIGHEST, preferred_element_type=F32)


def _mm_nt_hi(a, b):
    return lax.dot_general(a, b, (((1,), (1,)), ((), ())), precision=HIGHEST, preferred_element_type=F32)


def _split(a):
    hi = a.astype(BF16)
    return hi, (a - hi.astype(F32)).astype(BF16)


def _mm3(a, b):
    (ah, al), (bh, bl) = a, b
    dot = functools.partial(jnp.dot, preferred_element_type=F32)
    return dot(ah, bh) + (dot(al, bh) + dot(ah, bl))


def _head_rows(first_lane):
    return (_iota((SUBLANES, LANES), 1) == _iota((SUBLANES, LANES), 0) + first_lane).astype(F32)


def _sigmoid(x):
    return 1.0 / (1.0 + jnp.exp(-x))


def _silu(x):
    return x * _sigmoid(x)


def _softplus(x):
    return jnp.maximum(x, 0.0) + jnp.log1p(jnp.exp(-jnp.abs(x)))


def _iota(shape, dim):
    return lax.broadcasted_iota(jnp.int32, shape, dim)


def _tril(n, strict=False):
    r, c = _iota((n, n), 0), _iota((n, n), 1)
    return (r > c) if strict else (r >= c)


def _eye(n):
    return (_iota((n, n), 0) == _iota((n, n), 1)).astype(F32)


def _row_to_col(row):
    n = row.shape[1]
    return jnp.sum(_eye(n) * row, axis=1, keepdims=True)


def _group_rmsnorm(y, width):
    parts = []
    for g in range(y.shape[1] // width):
        yg = y[:, g * width:(g + 1) * width]
        parts.append(yg * lax.rsqrt(jnp.mean(yg * yg, axis=1, keepdims=True) + EPS))
    return jnp.concatenate(parts, axis=1)


def _causal_conv(x, cbuf, w, rows):
    cbuf[SUBLANES:SUBLANES + rows, :] = x
    y = (w[0:1, :] * cbuf[5:5 + rows, :] + w[1:2, :] * cbuf[6:6 + rows, :]
         + w[2:3, :] * cbuf[7:7 + rows, :] + w[3:4, :] * x)
    cbuf[0:SUBLANES, :] = cbuf[rows:rows + SUBLANES, :]
    return y


def _params(*sem, vmem_limit_bytes=VMEM_LIMIT_BYTES):
    return pltpu.CompilerParams(dimension_semantics=sem, vmem_limit_bytes=vmem_limit_bytes)


class _Layered:
    def __init__(self, init, src, dst, depth, prev):
        self.init, self.src, self.dst, self.depth, self.prev = init, src, dst, depth, prev

    def in_spec(self, bb):
        tail = self.init.shape[2:]
        return pl.BlockSpec((None, bb) + tail, lambda bi, i, s=self.src, z=(0,) * len(tail): (s, bi) + z)

    def out_spec(self, bb):
        tail = self.init.shape[2:]
        return pl.BlockSpec((None, bb) + tail, lambda bi, i, d=self.dst, z=(0,) * len(tail): (d, bi) + z)

    def out_shape(self):
        return jax.ShapeDtypeStruct((self.depth,) + self.init.shape[1:], F32)


def _mixer_call(body, *, grid, in_specs, inputs, out_specs, out_shape, states, first_state_out, scratch, name):
    n_in = len(inputs)
    prevs = [s.prev for s in states if s.prev is not None]
    aliases = {}
    fn = body
    if prevs:
        assert len(prevs) == len(states)
        in_specs = list(in_specs) + [pl.BlockSpec(memory_space=pl.ANY)] * len(prevs)
        aliases = {n_in + j: first_state_out + j for j in range(len(prevs))}

        def fn(*refs):
            return body(*refs[:n_in], *refs[n_in + len(prevs):])

    return pl.pallas_call(
        fn, grid=grid, in_specs=in_specs, out_specs=out_specs, out_shape=out_shape, scratch_shapes=scratch,
        input_output_aliases=aliases, compiler_params=_params("arbitrary", "arbitrary"), name=name,
    )(*inputs, *prevs)


def _ffn_kernel(x_ref, g_ref, wg_ref, wu_ref, wd_ref, fg_ref, o_ref, xn_ref, *, n_f, tf, d_ff, final_norm):
    f = pl.program_id(1)

    @pl.when(f == 0)
    def _():
        x = x_ref[...]
        xn = x * lax.rsqrt(jnp.mean(x * x, axis=1, keepdims=True) + EPS) * g_ref[...]
        xn_ref[...] = xn.astype(BF16)
        o_ref[...] = jnp.zeros_like(o_ref)

    wg, wu, wd = wg_ref[0].astype(BF16), wu_ref[0].astype(BF16), wd_ref[0].astype(BF16)
    keep = None
    if d_ff % tf:
        repeat = f * tf - jnp.minimum(f * tf, d_ff - tf)
        keep = _iota((1, tf), 1) >= repeat
    xn = xn_ref[...]
    gate = jnp.dot(xn, wg, preferred_element_type=F32)
    up = jnp.dot(xn, wu, preferred_element_type=F32)
    h = _silu(gate) * up
    if keep is not None:
        h = jnp.where(keep, h, 0.0)
    o_ref[...] += jnp.dot(h.astype(BF16), wd, preferred_element_type=F32)

    @pl.when(f == n_f - 1)
    def _():
        y = x_ref[...] + 0.5 * o_ref[...]
        if final_norm:
            y = y * lax.rsqrt(jnp.mean(y * y, axis=1, keepdims=True) + EPS) * fg_ref[...]
        o_ref[...] = y


def _ffn(x, g, wg, wu, wd, layer, fg, final_norm):
    m = x.shape[0]
    d_ff = wg.shape[2]
    tm = min(FFN_TOKEN_TILE, m)
    tf = FFN_HIDDEN_TILE if m >= FFN_TOKEN_TILE else FFN_HIDDEN_TILE_SMALL_M
    n_f = pl.cdiv(d_ff, tf)
    assert m % tm == 0 and d_ff % LANES == 0 and tf % LANES == 0 and d_ff >= tf
    start = lambda f: pl.multiple_of(jnp.minimum(f * tf, d_ff - tf), LANES)
    return pl.pallas_call(
        functools.partial(_ffn_kernel, n_f=n_f, tf=tf, d_ff=d_ff, final_norm=final_norm),
        grid=(m // tm, n_f),
        in_specs=[
            pl.BlockSpec((tm, D_MODEL), lambda i, f: (i, 0)),
            pl.BlockSpec((1, D_MODEL), lambda i, f: (0, 0)),
            pl.BlockSpec((pl.Element(1), pl.Element(D_MODEL), pl.Element(tf)), lambda i, f: (layer, 0, start(f))),
            pl.BlockSpec((pl.Element(1), pl.Element(D_MODEL), pl.Element(tf)), lambda i, f: (layer, 0, start(f))),
            pl.BlockSpec((pl.Element(1), pl.Element(tf), pl.Element(D_MODEL)), lambda i, f: (layer, start(f), 0)),
            pl.BlockSpec((1, D_MODEL), lambda i, f: (0, 0)),
        ],
        out_specs=pl.BlockSpec((tm, D_MODEL), lambda i, f: (i, 0)),
        out_shape=jax.ShapeDtypeStruct((m, D_MODEL), F32),
        scratch_shapes=[pltpu.VMEM((tm, D_MODEL), BF16)],
        compiler_params=_params("arbitrary", "arbitrary", vmem_limit_bytes=FFN_VMEM_LIMIT_BYTES),
        name="ffn",
    )(x, g, wg, wu, wd, fg)


def _in_proj_kernel(x_ref, g_ref, w_ref, o_ref, xn_ref):
    @pl.when(pl.program_id(1) == 0)
    def _():
        x = x_ref[...]
        xn = x * lax.rsqrt(jnp.mean(x * x, axis=1, keepdims=True) + EPS) * g_ref[...]
        xn_ref[...] = xn.astype(BF16)

    o_ref[...] = jnp.dot(xn_ref[...], w_ref[...], preferred_element_type=F32)


def _in_proj(x, g, w):
    m = x.shape[0]
    tm = min(IN_PROJ_TOKEN_TILE, m)
    tn = IN_PROJ_COL_TILE
    assert m % tm == 0 and PROJ_COLS % tn == 0
    return pl.pallas_call(
        _in_proj_kernel,
        grid=(m // tm, PROJ_COLS // tn),
        in_specs=[
            pl.BlockSpec((tm, D_MODEL), lambda i, n: (i, 0)),
            pl.BlockSpec((1, D_MODEL), lambda i, n: (0, 0)),
            pl.BlockSpec((D_MODEL, tn), lambda i, n: (0, n)),
        ],
        out_specs=pl.BlockSpec((tm, tn), lambda i, n: (i, n)),
        out_shape=jax.ShapeDtypeStruct((m, PROJ_COLS), F32),
        scratch_shapes=[pltpu.VMEM((tm, D_MODEL), BF16)],
        compiler_params=_params("arbitrary", "arbitrary"),
        name="in_proj",
    )(x, g, w)


def _out_proj_kernel(x_ref, y0_ref, y1_ref, y2_ref, y3_ref, w_ref, o_ref):
    acc = x_ref[...]
    for j, y_ref in enumerate((y0_ref, y1_ref, y2_ref, y3_ref)):
        acc = acc + jnp.dot(y_ref[...].astype(BF16), w_ref[j * D_GROUP:(j + 1) * D_GROUP, :],
                            preferred_element_type=F32)
    o_ref[...] = acc


def _out_proj(x, ys, w):
    m = x.shape[0]
    tm = min(OUT_PROJ_TOKEN_TILE, m)
    assert m % tm == 0
    yspec = pl.BlockSpec((tm, D_GROUP), lambda i: (i, 0))
    return pl.pallas_call(
        _out_proj_kernel,
        grid=(m // tm,),
        in_specs=[pl.BlockSpec((tm, D_MODEL), lambda i: (i, 0)), yspec, yspec, yspec, yspec,
                  pl.BlockSpec((D_MODEL, D_MODEL), lambda i: (0, 0))],
        out_specs=pl.BlockSpec((tm, D_MODEL), lambda i: (i, 0)),
        out_shape=jax.ShapeDtypeStruct((m, D_MODEL), F32),
        compiler_params=_params("arbitrary"),
        name="out_proj",
    )(x, *ys, w)


W_IN_SEGMENTS = ((512, COL_XBC, 1024), (0, COL_SSD_Z, 512), (3608, COL_GDN_QKV, 1536), (1544, COL_S5_U, 512),
                 (2568, COL_GLA_V, 512), (3096, COL_GLA_R, 512), (5144, COL_GDN_Z, 512), (2056, COL_GLA_Q, 256),
                 (2312, COL_GLA_K, 256), (1536, COL_SMALL + SM_DT, 8), (3080, COL_SMALL + SM_GR, 16),
                 (5656, COL_SMALL + SM_BETA, 8))


def _permute_w_in_kernel(wt_ref, o_ref):
    tr = o_ref.shape[0]
    eye = _eye(tr).astype(BF16)

    def transposed(rows_bf16):
        return lax.dot_general(eye, rows_bf16, (((1,), (1,)), ((), ())), preferred_element_type=F32).astype(BF16)

    small = []
    for src, dst, width in W_IN_SEGMENTS:
        if width % LANES == 0:
            o_ref[:, dst:dst + width] = transposed(wt_ref[src:src + width, :].astype(BF16))
        else:
            small.append(wt_ref[src:src + width, :])
    used = sum(s.shape[0] for s in small)
    small.append(jnp.zeros((LANES - used, tr), F32))
    o_ref[:, COL_SMALL:COL_SMALL + LANES] = transposed(jnp.concatenate(small, axis=0).astype(BF16))


def _permute_w_in(w, layer):
    _, rows, cols = w.shape
    tr = MXU_WIDTH
    return pl.pallas_call(
        _permute_w_in_kernel,
        grid=(rows // tr,),
        in_specs=[pl.BlockSpec((None, cols, tr), lambda r: (layer, 0, r))],
        out_specs=pl.BlockSpec((tr, PROJ_COLS), lambda r: (r, 0)),
        out_shape=jax.ShapeDtypeStruct((rows, PROJ_COLS), BF16),
        compiler_params=_params("arbitrary"),
        name="permute_w_in",
    )(jnp.swapaxes(w, 1, 2))


def _ssd_kernel(xbc_ref, z_ref, sm_ref, conv0_ref, h0_ref, cw_ref, cb_ref, dtb_ref, alog_ref, dex_ref, nrm_ref,
                y_ref, h_ref, cbuf, abuf, *, bb, rows, chunk, valid):
    @pl.when(pl.program_id(1) == 0)
    def _():
        cbuf[:, 0:SUBLANES, :] = conv0_ref[...]
        h_ref[...] = h0_ref[...]

    a_neg = -jnp.exp(alog_ref[...])
    tril = _tril(chunk).astype(F32)
    causal = _tril(chunk)
    left = _iota((1, LANES), 1) < SSD_HEAD_DIM
    top = _iota((LANES, 1), 0) < SSD_HEAD_DIM
    head_rows = _head_rows(SM_DT)

    def pair(v, h0):
        return jnp.where(left, v[:, h0:h0 + 1], v[:, h0 + 1:h0 + 2])

    dt_all = []
    for b in range(bb):
        conv = _causal_conv(xbc_ref[b], cbuf.at[b], cw_ref[...], rows) + cb_ref[...]
        abuf[b] = _silu(conv)
        dt = _softplus(sm_ref[b] + dtb_ref[...])
        if valid < rows:
            dt = jnp.where(_iota((rows, LANES), 0) < valid, dt, 0.0)
        dt_all.append(dt)

    n_c = rows // chunk
    seqs = [(b, c) for b in range(bb) for c in range(n_c)]
    dt = {(b, c): dt_all[b][c * chunk:(c + 1) * chunk, :] for (b, c) in seqs}
    acs = {s: _mm_hi(tril, dt[s] * a_neg) for s in seqs}
    acs_t = {s: _mm_nt_hi(head_rows, acs[s]) for s in seqs}
    groups = [(b, c, g) for (b, c) in seqs for g in range(SSD_GROUPS)]
    bm, cm = {}, {}
    for (b, c, g) in groups:
        r = slice(c * chunk, (c + 1) * chunk)
        bm[b, c, g] = abuf[b, r, D_GROUP + g * SSD_STATE:D_GROUP + (g + 1) * SSD_STATE]
        cm[b, c, g] = abuf[b, r, D_GROUP + (SSD_GROUPS + g) * SSD_STATE:D_GROUP + (SSD_GROUPS + g + 1) * SSD_STATE]
    cb = {x: _mm_nt(cm[x], bm[x]) for x in groups}
    pairs = [(b, c, g, j) for (b, c, g) in groups for j in range(2)]
    xdt, y_diag = {}, {}
    for (b, c, g, j) in pairs:
        h0 = 4 * g + 2 * j
        lo = (2 * g + j) * LANES
        xdt[b, c, g, j] = abuf[b, c * chunk:(c + 1) * chunk, lo:lo + LANES] * pair(dt[b, c], h0)
    for (b, c, g, j) in pairs:
        h0 = 4 * g + 2 * j
        total = None
        for hh, keep in ((h0, left), (h0 + 1, jnp.logical_not(left))):
            diff = acs[b, c][:, hh:hh + 1] - acs_t[b, c][hh:hh + 1, :]
            decay = jnp.exp(jnp.where(causal, diff, -jnp.inf))
            part = _mm(cb[b, c, g] * decay, jnp.where(keep, xdt[b, c, g, j], 0.0))
            total = part if total is None else total + part
        y_diag[b, c, g, j] = total

    for c in range(n_c):
        r0 = c * chunk
        for b in range(bb):
            a_c = acs[b, c]
            exp_acs = jnp.exp(a_c)
            to_end = jnp.exp(a_c[chunk - 1:chunk, :] - a_c)
            end_decay = jnp.exp(a_c[chunk - 1:chunk, :])
            for g in range(SSD_GROUPS):
                for j in range(2):
                    h0 = 4 * g + 2 * j
                    lo = (2 * g + j) * LANES
                    st = h_ref[b, lo:lo + LANES, :]
                    y_ref[b, r0:r0 + chunk, lo:lo + LANES] = (
                        y_diag[b, c, g, j] + _mm_nt(cm[b, c, g], st) * pair(exp_acs, h0))
                    dec = jnp.where(top, end_decay[:, h0:h0 + 1], end_decay[:, h0 + 1:h0 + 2])
                    h_ref[b, lo:lo + LANES, :] = st * dec + _mm_tn(xdt[b, c, g, j] * pair(to_end, h0), bm[b, c, g])

    for b in range(bb):
        y = y_ref[b] + dex_ref[...] * abuf[b, :, 0:D_GROUP]
        y = y * _silu(z_ref[b])
        y_ref[b] = _group_rmsnorm(y, D_GROUP // SSD_GROUPS) * nrm_ref[...]


def _ssd(proj, conv0, state, p, *, bb, rows, chunk, valid):
    b, length, _ = proj.shape
    w_xbc = D_GROUP + 2 * SSD_GROUPS * SSD_STATE
    vec = lambda n: pl.BlockSpec((1, n), lambda bi, i: (0, 0))
    return _mixer_call(
        functools.partial(_ssd_kernel, bb=bb, rows=rows, chunk=chunk, valid=valid),
        grid=(b // bb, length // rows),
        in_specs=[
            pl.BlockSpec((bb, rows, w_xbc), lambda bi, i: (bi, i, COL_XBC // w_xbc)),
            pl.BlockSpec((bb, rows, D_GROUP), lambda bi, i: (bi, i, COL_SSD_Z // D_GROUP)),
            pl.BlockSpec((bb, rows, LANES), lambda bi, i: (bi, i, COL_SMALL // LANES)),
            pl.BlockSpec((bb, SUBLANES, w_xbc), lambda bi, i: (bi, 0, 0)),
            state.in_spec(bb),
            pl.BlockSpec((CONV_W, w_xbc), lambda bi, i: (0, 0)),
            vec(w_xbc), vec(LANES), vec(LANES), vec(D_GROUP), vec(D_GROUP),
        ],
        inputs=[proj, proj, proj, conv0, state.init, p["ssd_conv_w"], p["ssd_conv_b"], p["ssd_dt_bias"],
                p["ssd_a_log"], p["ssd_d"], p["ssd_norm"]],
        out_specs=[pl.BlockSpec((bb, rows, D_GROUP), lambda bi, i: (bi, i, 0)), state.out_spec(bb)],
        out_shape=[jax.ShapeDtypeStruct((b, length, D_GROUP), F32), state.out_shape()],
        states=[state], first_state_out=1,
        scratch=[pltpu.VMEM((bb, rows + SUBLANES, w_xbc), F32), pltpu.VMEM((bb, rows, w_xbc), F32)],
        name="ssd")


def _s5_kernel(u_ref, h0re_ref, h0im_ref, apre_ref, apim_ref, wre_ref, wim_ref, cre_ref, cim_ref, d_ref, wglu_ref,
               bglu_ref, y_ref, hre_ref, him_ref, ubuf, *carry, bb, rows, valid):
    first = pl.program_id(1) == 0
    m = bb * rows
    n_blk = D_GROUP // LANES
    w_blk = S5_LANES // n_blk
    n_tiles = rows // SUBLANES

    @pl.when(first)
    def _():
        ubuf[:, 0:SUBLANES, :] = jnp.zeros((bb, SUBLANES, D_GROUP), F32)
        for buf in carry[2:]:
            buf[...] = jnp.zeros(buf.shape, F32)

    ubuf[:, SUBLANES:SUBLANES + rows, :] = u_ref[...]

    half = LANES // 2
    w_half = w_blk // 2
    left = _iota((1, LANES), 1) < half
    parts = []
    for j in range(n_blk):
        taps = [ubuf[:, SUBLANES - t:SUBLANES - t + rows, j * LANES:(j + 1) * LANES].reshape(m, LANES)
                for t in range(S5_TAPS)]
        packed = ([], [])
        for t in range(0, S5_TAPS, 2):
            packed[0].append(jnp.where(left, taps[t], pltpu.roll(taps[t + 1], half, 1)))
            packed[1].append(jnp.where(left, pltpu.roll(taps[t], half, 1), taps[t + 1]))
        y_j = None
        for hf in range(2):
            k = 2 * j + hf
            sl = slice(k * w_half, (k + 1) * w_half)
            lagged = jnp.concatenate(packed[hf], axis=1).astype(BF16)
            xr = jnp.dot(lagged, wre_ref[k], preferred_element_type=F32)
            xi = jnp.dot(lagged, wim_ref[k], preferred_element_type=F32)
            if n_tiles > 1:
                hbuf_re, hbuf_im, tail_re, tail_im = carry
                ar, ai = apre_ref[S5_TAPS - 1:S5_TAPS, sl], apim_ref[S5_TAPS - 1:S5_TAPS, sl]
                pr, pi = tail_re[:, sl], tail_im[:, sl]
                for q in range(n_tiles):
                    rs = slice(q * SUBLANES, (q + 1) * SUBLANES)
                    pr, pi = xr[rs, :] + ar * pr - ai * pi, xi[rs, :] + ar * pi + ai * pr
                    hbuf_re[rs, :] = pr
                    hbuf_im[rs, :] = pi

                @pl.when(first)
                def _():
                    h0r, h0i = h0re_ref[0, :, sl], h0im_ref[0, :, sl]
                    apr, api = apre_ref[:, sl], apim_ref[:, sl]
                    hbuf_re[...] = hbuf_re[...] + apr * h0r - api * h0i
                    hbuf_im[...] = hbuf_im[...] + apr * h0i + api * h0r

                hr, hi = hbuf_re[...], hbuf_im[...]
                tail_re[:, sl] = hr[rows - SUBLANES:rows, :]
                tail_im[:, sl] = hi[rows - SUBLANES:rows, :]
                hre_ref[0, :, sl] = hr[valid - 1:valid, :]
                him_ref[0, :, sl] = hi[valid - 1:valid, :]
            else:
                h0r, h0i = h0re_ref[:, :, sl], h0im_ref[:, :, sl]
                apr, api = apre_ref[:, sl][None], apim_ref[:, sl][None]
                hr3 = xr.reshape(bb, rows, w_half) + apr * h0r - api * h0i
                hi3 = xi.reshape(bb, rows, w_half) + apr * h0i + api * h0r
                hre_ref[:, :, sl] = hr3[:, valid - 1:valid, :]
                him_ref[:, :, sl] = hi3[:, valid - 1:valid, :]
                hr, hi = hr3.reshape(m, w_half), hi3.reshape(m, w_half)
            rows_c = slice(hf * w_half, (hf + 1) * w_half)
            y_half = _mm(hr, cre_ref[j, rows_c, :]) - _mm(hi, cim_ref[j, rows_c, :])
            y_j = y_half if y_j is None else y_j + y_half
        parts.append(y_j)

    if n_tiles > 1:
        ubuf[:, 0:SUBLANES, :] = ubuf[:, rows:rows + SUBLANES, :]

    y = jnp.concatenate(parts, axis=1) + d_ref[...] * u_ref[...].reshape(m, D_GROUP)
    y = 0.5 * y * (1.0 + jnp.tanh(math.sqrt(2.0 / math.pi) * (y + 0.044715 * (y * y * y))))
    y = y * _sigmoid(_mm(y, wglu_ref[...]) + bglu_ref[...])
    y_ref[...] = y.reshape(bb, rows, D_GROUP)


def _s5_lag_kernel(pre_ref, pim_ref, bre_ref, bim_ref, ore_ref, oim_ref):
    br, bi = bre_ref[...], bim_ref[...]
    kb = br.shape[0]
    for t in range(S5_TAPS):
        pr, pi = pre_ref[t:t + 1, :], pim_ref[t:t + 1, :]
        ore_ref[t * kb:(t + 1) * kb, :] = (pr * br - pi * bi).astype(BF16)
        oim_ref[t * kb:(t + 1) * kb, :] = (pr * bi + pi * br).astype(BF16)


def _s5_lag_weights(pow_re, pow_im, b_re, b_im):
    n_blk, kb, w_blk = b_re.shape
    pw = pl.BlockSpec((S5_TAPS, w_blk), lambda j: (0, j))
    bs = pl.BlockSpec((None, kb, w_blk), lambda j: (j, 0, 0))
    out = pl.BlockSpec((None, S5_TAPS * kb, w_blk), lambda j: (j, 0, 0))
    shape = jax.ShapeDtypeStruct((n_blk, S5_TAPS * kb, w_blk), BF16)
    return pl.pallas_call(
        _s5_lag_kernel, grid=(n_blk,), in_specs=[pw, pw, bs, bs], out_specs=[out, out], out_shape=[shape, shape],
        compiler_params=_params("arbitrary"), name="s5_lag_weights",
    )(pow_re, pow_im, b_re, b_im)


def _s5(proj, state_re, state_im, p, *, bb, rows, valid):
    b, length, _ = proj.shape
    assert rows == S5_TAPS or bb == 1
    n_blk = D_GROUP // LANES
    w_blk = S5_LANES // n_blk
    vec = lambda n: pl.BlockSpec((1, n), lambda bi, i: (0, 0))
    full3 = lambda s: pl.BlockSpec(s, lambda bi, i: (0, 0, 0))
    w_half = w_blk // 2
    carry = []
    if rows > S5_TAPS:
        carry = [pltpu.VMEM((rows, w_half), F32)] * 2 + [pltpu.VMEM((SUBLANES, S5_LANES), F32)] * 2
    return _mixer_call(
        functools.partial(_s5_kernel, bb=bb, rows=rows, valid=valid),
        grid=(b // bb, length // rows),
        in_specs=[
            pl.BlockSpec((bb, rows, D_GROUP), lambda bi, i: (bi, i, COL_S5_U // D_GROUP)),
            state_re.in_spec(bb), state_im.in_spec(bb),
            pl.BlockSpec((rows, S5_LANES), lambda bi, i: (0, 0)),
            pl.BlockSpec((rows, S5_LANES), lambda bi, i: (0, 0)),
            full3((2 * n_blk, S5_TAPS * LANES // 2, w_half)), full3((2 * n_blk, S5_TAPS * LANES // 2, w_half)),
            full3((n_blk, w_blk, LANES)), full3((n_blk, w_blk, LANES)),
            vec(D_GROUP), pl.BlockSpec((D_GROUP, D_GROUP), lambda bi, i: (0, 0)), vec(D_GROUP),
        ],
        inputs=[proj, state_re.init, state_im.init, p["s5_apow_re"][:rows], p["s5_apow_im"][:rows], p["s5_wlag_re"],
                p["s5_wlag_im"], p["s5_cblk_re"], p["s5_cblk_im"], p["s5_d"], p["s5_w_glu"], p["s5_b_glu"]],
        out_specs=[pl.BlockSpec((bb, rows, D_GROUP), lambda bi, i: (bi, i, 0)),
                   state_re.out_spec(bb), state_im.out_spec(bb)],
        out_shape=[jax.ShapeDtypeStruct((b, length, D_GROUP), F32), state_re.out_shape(), state_im.out_shape()],
        states=[state_re, state_im], first_state_out=1,
        scratch=[pltpu.VMEM((bb, SUBLANES + rows, D_GROUP), F32)] + carry,
        name="s5")


def _gla_kernel(q_ref, k_ref, v_ref, r_ref, sm_ref, s0_ref, wg_ref, bg_ref, nrm_ref, o_ref, s_ref,
                *, bb, rows, chunk, valid):
    @pl.when(pl.program_id(1) == 0)
    def _():
        s_ref[...] = s0_ref[...]

    hk = GLA_HEADS * GLA_DK
    tril = _tril(chunk).astype(F32)
    causal = _tril(chunk)
    lane = _iota((1, hk), 1)
    lg_all, k_all = [], []
    for b in range(bb):
        lg = -_softplus(-(_mm(sm_ref[b], wg_ref[...]) + bg_ref[...])) * (1.0 / GLA_GATE_TEMP)
        k = k_ref[b]
        if valid < rows:
            live = _iota((rows, hk), 0) < valid
            lg = jnp.where(live, lg, 0.0)
            k = jnp.where(live, k, 0.0)
        lg_all.append(lg)
        k_all.append(k)

    n_c = rows // chunk
    seqs = [(b, c) for b in range(bb) for c in range(n_c)]
    cum = {(b, c): _mm_hi(tril, lg_all[b][c * chunk:(c + 1) * chunk, :]) for (b, c) in seqs}
    q_dec, k_dec, k_end, end_decay = {}, {}, {}, {}
    for (b, c) in seqs:
        r = slice(c * chunk, (c + 1) * chunk)
        cs = cum[b, c]
        cum_last = cs[chunk - 1:chunk, :]
        q_dec[b, c] = q_ref[b, r, :] * (GLA_DK ** -0.5) * jnp.exp(cs)
        k = k_all[b][r, :]
        k_dec[b, c] = k * jnp.exp(-cs)
        k_end[b, c] = k * jnp.exp(cum_last - cs)
        end_decay[b, c] = _row_to_col(jnp.exp(cum_last))
    units = [(b, c, h) for (b, c) in seqs for h in range(GLA_HEADS)]
    qh, vh = {}, {}
    for (b, c, h) in units:
        mine = (lane >= h * GLA_DK) & (lane < (h + 1) * GLA_DK)
        qh[b, c, h] = jnp.where(mine, q_dec[b, c], 0.0)
        vh[b, c, h] = v_ref[b, c * chunk:(c + 1) * chunk, h * GLA_DV:(h + 1) * GLA_DV]
    att = {x: jnp.where(causal, _mm_nt(qh[x], k_dec[x[:2]]), 0.0) for x in units}
    o_intra = {x: _mm(att[x], vh[x]) for x in units}
    kv = {x: _mm_tn(k_end[x[:2]], vh[x])[x[2] * GLA_DK:(x[2] + 1) * GLA_DK, :] for x in units}

    for c in range(n_c):
        for b in range(bb):
            st = s_ref[b]
            for h in range(GLA_HEADS):
                rs = slice(h * GLA_DK, (h + 1) * GLA_DK)
                o_ref[b, c * chunk:(c + 1) * chunk, h * GLA_DV:(h + 1) * GLA_DV] = (
                    o_intra[b, c, h] + _mm(qh[b, c, h], st))
                s_ref[b, rs, :] = st[rs, :] * end_decay[b, c][rs, :] + kv[b, c, h]

    for b in range(bb):
        o = _group_rmsnorm(o_ref[b], GLA_DV) * nrm_ref[...]
        o_ref[b] = o * _silu(r_ref[b])


def _gla(proj, state, p, *, bb, rows, chunk, valid):
    b, length, _ = proj.shape
    hk = GLA_HEADS * GLA_DK
    vec = lambda n: pl.BlockSpec((1, n), lambda bi, i: (0, 0))
    return _mixer_call(
        functools.partial(_gla_kernel, bb=bb, rows=rows, chunk=chunk, valid=valid),
        grid=(b // bb, length // rows),
        in_specs=[
            pl.BlockSpec((bb, rows, hk), lambda bi, i: (bi, i, COL_GLA_Q // hk)),
            pl.BlockSpec((bb, rows, hk), lambda bi, i: (bi, i, COL_GLA_K // hk)),
            pl.BlockSpec((bb, rows, D_GROUP), lambda bi, i: (bi, i, COL_GLA_V // D_GROUP)),
            pl.BlockSpec((bb, rows, D_GROUP), lambda bi, i: (bi, i, COL_GLA_R // D_GROUP)),
            pl.BlockSpec((bb, rows, LANES), lambda bi, i: (bi, i, COL_SMALL // LANES)),
            state.in_spec(bb),
            pl.BlockSpec((LANES, hk), lambda bi, i: (0, 0)), vec(hk), vec(D_GROUP),
        ],
        inputs=[proj, proj, proj, proj, proj, state.init, p["gla_w_gate2"], p["gla_b_gate2"], p["gla_norm"]],
        out_specs=[pl.BlockSpec((bb, rows, D_GROUP), lambda bi, i: (bi, i, 0)), state.out_spec(bb)],
        out_shape=[jax.ShapeDtypeStruct((b, length, D_GROUP), F32), state.out_shape()],
        states=[state], first_state_out=1, scratch=[], name="gla")


def _gdn_kernel(qkv_ref, z_ref, sm_ref, conv0_ref, s0_ref, cw_ref, alog_ref, dtb_ref, nrm_ref, o_ref, s_ref,
                cbuf, abuf, *, bb, rows, chunk, valid):
    @pl.when(pl.program_id(1) == 0)
    def _():
        cbuf[:, 0:SUBLANES, :] = conv0_ref[...]
        s_ref[...] = s0_ref[...]

    a_neg = -jnp.exp(alog_ref[...])
    tril = _tril(chunk).astype(F32)
    causal = _tril(chunk)
    strict = _tril(chunk, strict=True)
    eye_c = _eye(chunk)
    gate_rows = _head_rows(SM_BETA)
    hd = GDN_HEADS * GDN_DK
    n_c = rows // chunk

    def l2n(x):
        return x * lax.rsqrt(jnp.sum(x * x, axis=1, keepdims=True) + EPS)

    beta_all, g_all = [], []
    for b in range(bb):
        abuf[b] = _silu(_causal_conv(qkv_ref[b], cbuf.at[b], cw_ref[...], rows))
        sm = sm_ref[b]
        beta = _sigmoid(sm)
        g = a_neg * _softplus(sm + dtb_ref[...])
        if valid < rows:
            live = _iota((rows, LANES), 0) < valid
            beta = jnp.where(live, beta, 0.0)
            g = jnp.where(live, g, 0.0)
        beta_all.append(beta)
        g_all.append(g)

    seqs = [(b, c) for b in range(bb) for c in range(n_c)]
    gc = {s: _mm_hi(tril, g_all[s[0]][s[1] * chunk:(s[1] + 1) * chunk, :]) for s in seqs}
    gc_t = {s: _mm_nt_hi(gate_rows, gc[s]) for s in seqs}

    units = [(b, c, h) for (b, c) in seqs for h in range(GDN_HEADS)]
    q, k, v, g_col, b_col, decay = {}, {}, {}, {}, {}, {}
    for (b, c, h) in units:
        r = slice(c * chunk, (c + 1) * chunk)
        q[b, c, h] = l2n(abuf[b, r, h * GDN_DK:(h + 1) * GDN_DK]) * (GDN_DK ** -0.5)
        k[b, c, h] = l2n(abuf[b, r, hd + h * GDN_DK:hd + (h + 1) * GDN_DK])
        v[b, c, h] = abuf[b, r, 2 * hd + h * GDN_DK:2 * hd + (h + 1) * GDN_DK]
        g_col[b, c, h] = gc[b, c][:, SM_DECAY + h:SM_DECAY + h + 1]
        g_row = gc_t[b, c][SM_DECAY - SM_BETA + h:SM_DECAY - SM_BETA + h + 1, :]
        b_col[b, c, h] = beta_all[b][r, SM_BETA + h:SM_BETA + h + 1]
        decay[b, c, h] = jnp.exp(jnp.where(causal, g_col[b, c, h] - g_row, -jnp.inf))
    kk = {x: _mm_nt(k[x], k[x]) for x in units}
    npow = {x: jnp.where(strict, -(b_col[x] * kk[x] * decay[x]), 0.0) for x in units}
    t_mat = {x: eye_c + npow[x] for x in units}
    span = 2
    parts = {x: _split(npow[x]) for x in units}
    while span < chunk:
        parts = {x: _split(_mm3(parts[x], parts[x])) for x in units}
        t_mat = {x: t_mat[x] + _mm3(_split(t_mat[x]), parts[x]) for x in units}
        span *= 2
    uw = {x: _mm(t_mat[x], jnp.concatenate([v[x] * b_col[x], k[x] * (b_col[x] * jnp.exp(g_col[x]))], axis=1))
          for x in units}
    qk = {x: _mm_nt(q[x], k[x]) * decay[x] for x in units}
    q_dec = {x: q[x] * jnp.exp(g_col[x]) for x in units}
    stack_rows = n_c == 1

    for c in range(n_c):
        now = [(b, c, h) for b in range(bb) for h in range(GDN_HEADS)]
        st = {x: s_ref[x[0], x[2] * GDN_DK:(x[2] + 1) * GDN_DK, :] for x in now}
        if stack_rows:
            ws_qs = {x: _mm(jnp.concatenate([uw[x][:, GDN_DK:], q_dec[x]], axis=0), st[x]) for x in now}
            w_s = {x: ws_qs[x][:chunk, :] for x in now}
            q_s = {x: ws_qs[x][chunk:, :] for x in now}
        else:
            w_s = {x: _mm(uw[x][:, GDN_DK:], st[x]) for x in now}
            q_s = None
        v_new = {x: uw[x][:, :GDN_DK] - w_s[x] for x in now}
        for x in now:
            b, _, h = x
            g_last = gc[b, c][chunk - 1:chunk, SM_DECAY + h:SM_DECAY + h + 1]
            q_state = q_s[x] if stack_rows else _mm(q_dec[x], st[x])
            o_ref[b, c * chunk:(c + 1) * chunk, h * GDN_DK:(h + 1) * GDN_DK] = q_state + _mm(qk[x], v_new[x])
            s_ref[b, h * GDN_DK:(h + 1) * GDN_DK, :] = (
                st[x] * jnp.exp(g_last) + _mm_tn(k[x] * jnp.exp(g_last - g_col[x]), v_new[x]))

    for b in range(bb):
        o = _group_rmsnorm(o_ref[b], GDN_DK) * nrm_ref[...]
        o_ref[b] = o * _silu(z_ref[b])


def _gdn(proj, conv0, state, p, *, bb, rows, chunk, valid):
    b, length, _ = proj.shape
    w_qkv = 3 * GDN_HEADS * GDN_DK
    vec = lambda n: pl.BlockSpec((1, n), lambda bi, i: (0, 0))
    return _mixer_call(
        functools.partial(_gdn_kernel, bb=bb, rows=rows, chunk=chunk, valid=valid),
        grid=(b // bb, length // rows),
        in_specs=[
            pl.BlockSpec((bb, rows, w_qkv), lambda bi, i: (bi, i, COL_GDN_QKV // w_qkv)),
            pl.BlockSpec((bb, rows, D_GROUP), lambda bi, i: (bi, i, COL_GDN_Z // D_GROUP)),
            pl.BlockSpec((bb, rows, LANES), lambda bi, i: (bi, i, COL_SMALL // LANES)),
            pl.BlockSpec((bb, SUBLANES, w_qkv), lambda bi, i: (bi, 0, 0)),
            state.in_spec(bb),
            pl.BlockSpec((CONV_W, w_qkv), lambda bi, i: (0, 0)),
            vec(LANES), vec(LANES), vec(D_GROUP),
        ],
        inputs=[proj, proj, proj, conv0, state.init, p["gdn_conv_w"], p["gdn_a_log"], p["gdn_dt_bias"], p["gdn_norm"]],
        out_specs=[pl.BlockSpec((bb, rows, D_GROUP), lambda bi, i: (bi, i, 0)), state.out_spec(bb)],
        out_shape=[jax.ShapeDtypeStruct((b, length, D_GROUP), F32), state.out_shape()],
        states=[state], first_state_out=1,
        scratch=[pltpu.VMEM((bb, rows + SUBLANES, w_qkv), F32), pltpu.VMEM((bb, rows, w_qkv), F32)],
        name="gdn")


def _lane_row(values, offset):
    return jnp.zeros((1, LANES), F32).at[0, offset:offset + values.shape[0]].set(values.astype(F32))


def _prep_layer(w, i, max_rows):
    row = lambda a: a[i].astype(F32).reshape(1, -1)
    p = {}
    for name in ("ffn1", "ffn2"):
        p[name + "_norm"] = row(w[name + "_norm"])
        for part in ("_w_gate", "_w_up", "_w_down"):
            p[name + part] = w[name + part].astype(F32)
    p["mix_norm"] = row(w["mix_norm"])

    p["w_in"] = _permute_w_in(w["w_in"].astype(F32), i)
    p["w_out"] = w["w_out"][i].astype(BF16)

    p["ssd_conv_w"] = w["ssd_conv_w"][i].astype(F32)
    p["ssd_conv_b"] = row(w["ssd_conv_b"])
    p["ssd_dt_bias"] = _lane_row(w["ssd_dt_bias"][i], SM_DT)
    p["ssd_a_log"] = _lane_row(w["ssd_a_log"][i], SM_DT)
    p["ssd_d"] = jnp.repeat(w["ssd_d"][i].astype(F32), SSD_HEAD_DIM).reshape(1, D_GROUP)
    p["ssd_norm"] = row(w["ssd_norm"])

    a_re, a_im = w["s5_a_re"][i].astype(F32), w["s5_a_im"][i].astype(F32)
    dt = jnp.exp(w["s5_log_dt"][i].astype(F32))[:, None]
    lam_re, lam_im = a_re * dt, a_im * dt

    def a_bar_pow(n):
        mag = jnp.exp(lam_re * n)
        return mag * jnp.cos(lam_im * n), mag * jnp.sin(lam_im * n)

    abar_re, abar_im = a_bar_pow(1.0)
    den = a_re * a_re + a_im * a_im
    coef_re = ((abar_re - 1.0) * a_re + abar_im * a_im) / den
    coef_im = (abar_im * a_re - (abar_re - 1.0) * a_im) / den
    b_re, b_im = w["s5_b_re"][i].astype(F32), w["s5_b_im"][i].astype(F32)
    bbar_re = coef_re[..., None] * b_re - coef_im[..., None] * b_im
    bbar_im = coef_re[..., None] * b_im + coef_im[..., None] * b_re
    steps = jnp.arange(1, max_rows + 1, dtype=F32)[:, None, None]
    apow_re, apow_im = a_bar_pow(steps)
    p["s5_apow_re"] = apow_re.reshape(max_rows, S5_LANES)
    p["s5_apow_im"] = apow_im.reshape(max_rows, S5_LANES)
    lag_re, lag_im = a_bar_pow(jnp.arange(S5_TAPS, dtype=F32)[:, None, None])
    n_blk = D_GROUP // LANES
    gpb = S5_GROUPS // n_blk
    eye = jnp.eye(gpb, dtype=F32)

    def b_blocks(x):
        half = gpb // 2
        x = jnp.swapaxes(x, 1, 2).reshape(2 * n_blk, half, S5_CH, S5_STATE)
        return jnp.einsum("jgip,gh->jgihp", x, eye[:half, :half]).reshape(2 * n_blk, LANES // 2, half * S5_STATE)

    def c_blocks(x):
        x = x.reshape(n_blk, gpb, S5_CH, S5_STATE)
        return jnp.einsum("jgip,gh->jgphi", x, eye).reshape(n_blk, gpb * S5_STATE, LANES).astype(BF16)

    p["s5_wlag_re"], p["s5_wlag_im"] = _s5_lag_weights(
        lag_re.reshape(S5_TAPS, S5_LANES), lag_im.reshape(S5_TAPS, S5_LANES), b_blocks(bbar_re), b_blocks(bbar_im))
    p["s5_cblk_re"] = c_blocks(w["s5_c_re"][i].astype(F32))
    p["s5_cblk_im"] = c_blocks(w["s5_c_im"][i].astype(F32))
    p["s5_d"] = row(w["s5_d"])
    p["s5_w_glu"] = w["s5_w_glu"][i].astype(BF16)
    p["s5_b_glu"] = row(w["s5_b_glu"])

    hk = GLA_HEADS * GLA_DK
    p["gla_w_gate2"] = jnp.zeros((LANES, hk), F32).at[SM_GR:SM_GR + GLA_GATE_RANK].set(w["gla_w_gate2"][i].astype(F32))
    p["gla_b_gate2"] = row(w["gla_b_gate2"])
    p["gla_norm"] = row(w["gla_norm"])

    p["gdn_conv_w"] = w["gdn_conv_w"][i].astype(F32)
    p["gdn_a_log"] = _lane_row(w["gdn_a_log"][i], SM_DECAY)
    p["gdn_dt_bias"] = _lane_row(w["gdn_dt_bias"][i], SM_DECAY)
    p["gdn_norm"] = row(w["gdn_norm"])
    return p


def _mixer_geometry(bsz, length):
    if length >= CHUNK:
        assert length % CHUNK == 0
        rows = MIXER_ROWS if length % MIXER_ROWS == 0 else CHUNK
        return (LONG_SEQS_PER_STEP if bsz % LONG_SEQS_PER_STEP == 0 else 1), rows, CHUNK, length
    assert length <= SUBLANES
    bb = SUBLANES if bsz % SUBLANES == 0 else 1
    return bb, SUBLANES, SUBLANES, SUBLANES


def _conv_buffer(state):
    return jnp.pad(state.astype(F32), ((0, 0), (SUBLANES - (CONV_W - 1), 0), (0, 0)))


def _trunk(x, states, layers, final_norm):
    bsz, length, _ = x.shape
    depth = len(layers)
    bb, rows, chunk, lpad = _mixer_geometry(bsz, length)
    geo = dict(bb=bb, rows=rows, chunk=chunk, valid=min(length, rows))
    h = x.reshape(bsz * length, D_MODEL).astype(F32)
    fg = final_norm.astype(F32).reshape(1, D_MODEL)
    ssd_conv, ssd_h, s5_re, s5_im, gla_s, gdn_conv, gdn_s = [s.astype(F32) for s in states]
    lead = ssd_h.shape[0]
    recurrent = [ssd_h.reshape(lead, bsz, D_GROUP, SSD_STATE), s5_re.reshape(lead, bsz, 1, S5_LANES),
                 s5_im.reshape(lead, bsz, 1, S5_LANES), gla_s.reshape(lead, bsz, GLA_HEADS * GLA_DK, GLA_DV),
                 gdn_s.reshape(lead, bsz, GDN_HEADS * GDN_DK, GDN_DK)]
    new = [jnp.zeros((depth,) + r.shape[1:], F32) for r in recurrent]
    ssd_conv_new, gdn_conv_new = [], []
    for i, p in enumerate(layers):
        src = min(i, lead - 1)
        st = [_Layered(init, src, i, depth, prev) for init, prev in zip(recurrent, new)]
        h = _ffn(h, p["ffn1_norm"], p["ffn1_w_gate"], p["ffn1_w_up"], p["ffn1_w_down"], i, fg, False)
        proj = _in_proj(h, p["mix_norm"], p["w_in"]).reshape(bsz, length, PROJ_COLS)
        ssd_conv_new.append(proj[:, length - (CONV_W - 1):, COL_XBC:COL_XBC + SSD_CONV_CH])
        gdn_conv_new.append(proj[:, length - (CONV_W - 1):, COL_GDN_QKV:COL_GDN_QKV + GDN_CONV_CH])
        if lpad != length:
            proj = jnp.pad(proj, ((0, 0), (0, lpad - length), (0, 0)))
        y_ssd, new[0] = _ssd(proj, _conv_buffer(ssd_conv[src]), st[0], p, **geo)
        s5_bb = bsz if rows == S5_TAPS else 1
        y_s5, new[1], new[2] = _s5(proj, st[1], st[2], p, bb=s5_bb, rows=rows, valid=geo["valid"])
        y_gla, new[3] = _gla(proj, st[3], p, **geo)
        y_gdn, new[4] = _gdn(proj, _conv_buffer(gdn_conv[src]), st[4], p, **geo)
        ys = [y[:, :length].reshape(bsz * length, D_GROUP) for y in (y_ssd, y_s5, y_gla, y_gdn)]
        h = _out_proj(h, ys, p["w_out"])
        h = _ffn(h, p["ffn2_norm"], p["ffn2_w_gate"], p["ffn2_w_up"], p["ffn2_w_down"], i, fg, i == depth - 1)
    out_states = (jnp.stack(ssd_conv_new), new[0].reshape(depth, bsz, SSD_HEADS, SSD_HEAD_DIM, SSD_STATE),
                  new[1].reshape(depth, bsz, S5_GROUPS, S5_STATE), new[2].reshape(depth, bsz, S5_GROUPS, S5_STATE),
                  new[3].reshape(depth, bsz, GLA_HEADS, GLA_DK, GLA_DV), jnp.stack(gdn_conv_new),
                  new[4].reshape(depth, bsz, GDN_HEADS, GDN_DK, GDN_DK))
    return h.reshape(bsz, length, D_MODEL), out_states


def kernel(x_prompt, x_sample, state_ssd_conv, state_ssd, state_s5_re, state_s5_im, state_gla, state_gdn_conv, state_gdn, ffn1_norm, ffn1_w_gate, ffn1_w_up, ffn1_w_down, mix_norm, w_in, ssd_conv_w, ssd_conv_b, ssd_dt_bias, ssd_a_log, ssd_d, ssd_norm, s5_a_re, s5_a_im, s5_log_dt, s5_b_re, s5_b_im, s5_c_re, s5_c_im, s5_d, s5_w_glu, s5_b_glu, gla_w_gate2, gla_b_gate2, gla_norm, gdn_conv_w, gdn_a_log, gdn_dt_bias, gdn_norm, w_out, ffn2_norm, ffn2_w_gate, ffn2_w_up, ffn2_w_down, final_norm):
    w = dict(ffn1_norm=ffn1_norm, ffn1_w_gate=ffn1_w_gate, ffn1_w_up=ffn1_w_up, ffn1_w_down=ffn1_w_down,
             mix_norm=mix_norm, w_in=w_in, ssd_conv_w=ssd_conv_w, ssd_conv_b=ssd_conv_b, ssd_dt_bias=ssd_dt_bias,
             ssd_a_log=ssd_a_log, ssd_d=ssd_d, ssd_norm=ssd_norm, s5_a_re=s5_a_re, s5_a_im=s5_a_im,
             s5_log_dt=s5_log_dt, s5_b_re=s5_b_re, s5_b_im=s5_b_im, s5_c_re=s5_c_re, s5_c_im=s5_c_im, s5_d=s5_d,
             s5_w_glu=s5_w_glu, s5_b_glu=s5_b_glu, gla_w_gate2=gla_w_gate2, gla_b_gate2=gla_b_gate2,
             gla_norm=gla_norm, gdn_conv_w=gdn_conv_w, gdn_a_log=gdn_a_log, gdn_dt_bias=gdn_dt_bias,
             gdn_norm=gdn_norm, w_out=w_out, ffn2_norm=ffn2_norm, ffn2_w_gate=ffn2_w_gate, ffn2_w_up=ffn2_w_up,
             ffn2_w_down=ffn2_w_down)
    depth = w_in.shape[0]
    max_rows = max(_mixer_geometry(*x_prompt.shape[:2])[1], _mixer_geometry(*x_sample.shape[:2])[1])
    layers = [_prep_layer(w, i, max_rows) for i in range(depth)]

    sample_states = (state_ssd_conv, state_ssd, state_s5_re, state_s5_im, state_gla, state_gdn_conv, state_gdn)
    n_prompt = x_prompt.shape[0]
    prompt_states = tuple(jnp.zeros((1, n_prompt) + s.shape[2:], F32) for s in sample_states)
    y_prompt, p_states = _trunk(x_prompt, prompt_states, layers, final_norm)
    y_sample, s_states = _trunk(x_sample, sample_states, layers, final_norm)
    return (y_prompt, y_sample) + p_states + s_states
```
